```python
import jax, jax.numpy as jnp
from jax import lax
import numpy as np

D_MODEL = 2048
BATCH = 2
SEQ = 8192
DEPTH = 2

GRID_W = 64
CTX_LEN = 256
N_MIXERS = 2
N_HEADS = 16
N_KV_HEADS = 4
HEAD_DIM = D_MODEL // N_HEADS
Q_PER_KV = N_HEADS // N_KV_HEADS
KV_DIM = N_KV_HEADS * HEAD_DIM
QKV_DIM = D_MODEL + 2 * KV_DIM
ATTN_SCALE = HEAD_DIM ** -0.5
ROPE_THETA = 10000.0
ROPE_PAIRS = HEAD_DIM // 4
Q_BLOCK = 128
POOL_WINDOWS = (2, 4, 8, 16)
N_POOL_GROUPS = len(POOL_WINDOWS)
POOL_GROUP_DIM = D_MODEL // N_POOL_GROUPS
N_EXPERTS = 32
N_EXPERT_GROUPS = 8
EXPERTS_PER_GROUP = N_EXPERTS // N_EXPERT_GROUPS
TOP_K = 2
D_EXPERT = D_MODEL // 2
MOE_BLOCK = 256
N_ATTN_LAYERS = (DEPTH + 1) // 2
N_POOL_LAYERS = DEPTH // 2
DEEPNORM_ALPHA = (2 * DEPTH) ** 0.25
DEEPNORM_BETA = (8 * DEPTH) ** -0.25
LN_EPS = 1e-6
RMS_EPS = 1e-6

kernel_name = 'hybrid_gqa_pool_grouped_moe_deepnorm_dit'


def layer_norm(x, g, b):
    xf = x.astype(jnp.float32)
    mu = jnp.mean(xf, axis=-1, keepdims=True)
    var = jnp.mean(jnp.square(xf - mu), axis=-1, keepdims=True)
    y = (xf - mu) * lax.rsqrt(var + LN_EPS) * g.astype(jnp.float32) + b.astype(jnp.float32)
    return y.astype(x.dtype)


def rms_norm(x, g):
    xf = x.astype(jnp.float32)
    y = xf * lax.rsqrt(jnp.mean(jnp.square(xf), axis=-1, keepdims=True) + RMS_EPS) * g.astype(jnp.float32)
    return y.astype(x.dtype)


def modulate(x, shift, scale):
    return x * (1 + scale) + shift


def axial_rope_tables(n_rows, dtype):
    t = jnp.arange(n_rows * GRID_W)
    row = (t // GRID_W).astype(jnp.float32)
    col = (t % GRID_W).astype(jnp.float32)
    inv_freq = ROPE_THETA ** (-jnp.arange(ROPE_PAIRS, dtype=jnp.float32) / ROPE_PAIRS)
    ang_r = row[:, None] * inv_freq
    ang_c = col[:, None] * inv_freq
    ang = jnp.concatenate([ang_r, ang_r, ang_c, ang_c], axis=-1)
    return jnp.cos(ang).astype(dtype), jnp.sin(ang).astype(dtype)


def apply_axial_rope(x, cos, sin):
    xr = x.reshape(x.shape[:-1] + (2, 2, ROPE_PAIRS))
    rot = jnp.stack([-xr[..., 1, :], xr[..., 0, :]], axis=-2).reshape(x.shape)
    return x * cos[:, None, :] + rot * sin[:, None, :]


def project_qkv(u, w_qkv, q_gain, k_gain):
    B, L, _ = u.shape
    qkv = u @ w_qkv
    q = qkv[..., :D_MODEL].reshape(B, L, N_HEADS, HEAD_DIM)
    k = qkv[..., D_MODEL:D_MODEL + KV_DIM].reshape(B, L, N_KV_HEADS, HEAD_DIM)
    v = qkv[..., D_MODEL + KV_DIM:].reshape(B, L, N_KV_HEADS, HEAD_DIM)
    return rms_norm(q, q_gain), rms_norm(k, k_gain), v


def project_kv(u, w_kv, k_gain):
    B, L, _ = u.shape
    kv = u @ w_kv
    k = kv[..., :KV_DIM].reshape(B, L, N_KV_HEADS, HEAD_DIM)
    v = kv[..., KV_DIM:].reshape(B, L, N_KV_HEADS, HEAD_DIM)
    return rms_norm(k, k_gain), v


def gqa_attend(q, k, v):
    s = jnp.einsum('bqkgd,bskd->bkgqs', q, k, preferred_element_type=jnp.float32) * ATTN_SCALE
    p = jax.nn.softmax(s, axis=-1).astype(v.dtype)
    return jnp.einsum('bkgqs,bskd->bqkgd', p, v)


def attention_mixer(u, uc, rope_cos, rope_sin, w_qkv, q_gain, k_gain, w_o, need_ctx):
    B, L, _ = u.shape
    C = uc.shape[1]
    q, k, v = project_qkv(u, w_qkv, q_gain, k_gain)
    q = apply_axial_rope(q, rope_cos, rope_sin)
    k = apply_axial_rope(k, rope_cos, rope_sin)
    if need_ctx:
        qc, kc, vc = project_qkv(uc, w_qkv, q_gain, k_gain)
    else:
        kc, vc = project_kv(uc, w_qkv[:, D_MODEL:], k_gain)
    k_all = jnp.concatenate([kc, k], axis=1)
    v_all = jnp.concatenate([vc, v], axis=1)
    n_blk = L // Q_BLOCK
    q_blocks = q.reshape(B, n_blk, Q_BLOCK, N_KV_HEADS, Q_PER_KV, HEAD_DIM).swapaxes(0, 1)
    o = lax.map(lambda qb: gqa_attend(qb, k_all, v_all), q_blocks)
    y = o.swapaxes(0, 1).reshape(B, L, D_MODEL) @ w_o
    if not need_ctx:
        return y, None
    oc = gqa_attend(qc.reshape(B, C, N_KV_HEADS, Q_PER_KV, HEAD_DIM), kc, vc)
    return y, oc.reshape(B, C, D_MODEL) @ w_o


def pool_mixer(u, w_pool, pool_scale):
    B, L, D = u.shape
    uf = u.astype(jnp.float32)
    cs = jnp.concatenate([jnp.zeros((B, 1, D), jnp.float32), jnp.cumsum(uf, axis=1)], axis=1)
    t = jnp.arange(L)
    outs = []
    for g, w in enumerate(POOL_WINDOWS):
        sl = slice(g * POOL_GROUP_DIM, (g + 1) * POOL_GROUP_DIM)
        lo = jnp.clip(t - w // 2, 0, L)
        hi = jnp.clip(t + w // 2, 0, L)
        cnt = (hi - lo).astype(jnp.float32)[:, None]
        csg = cs[..., sl]
        mean = (csg[:, hi] - csg[:, lo]) / cnt
        d = (mean - uf[..., sl]).astype(u.dtype)
        outs.append(d @ w_pool[g])
    return jnp.concatenate(outs, axis=-1) * pool_scale


def moe_ffn(h, w_router, router_bias, w_gate, w_up, w_down):
    T = h.shape[0]
    n_assign = T * TOP_K
    logits = jnp.einsum('td,de->te', h, w_router, preferred_element_type=jnp.float32)
    aff = jax.nn.sigmoid(logits)
    sel = (aff + router_bias.astype(jnp.float32)).reshape(T, N_EXPERT_GROUPS, EXPERTS_PER_GROUP)
    group_score = jnp.sum(lax.top_k(sel, 2)[0], axis=-1)
    g_idx = jnp.argmax(group_score, axis=-1)
    in_group = sel[jnp.arange(T), g_idx]
    _, local = lax.top_k(in_group, TOP_K)
    e_idx = g_idx[:, None] * EXPERTS_PER_GROUP + local
    gate = jnp.take_along_axis(aff, e_idx, axis=1)
    gate = gate / jnp.sum(gate, axis=-1, keepdims=True)
    flat_e = e_idx.reshape(-1).astype(jnp.int32)
    flat_tok = jnp.repeat(jnp.arange(T, dtype=jnp.int32), TOP_K)
    flat_gate = gate.reshape(-1)
    order = jnp.argsort(flat_e, stable=True)
    se = flat_e[order]
    counts = jnp.bincount(flat_e, length=N_EXPERTS)
    padded = (counts + MOE_BLOCK - 1) // MOE_BLOCK * MOE_BLOCK
    starts = jnp.cumsum(counts) - counts
    p_ends = jnp.cumsum(padded)
    p_starts = p_ends - padded
    dest = p_starts[se] + jnp.arange(n_assign, dtype=jnp.int32) - starts[se]
    n_blocks = (n_assign + N_EXPERTS * (MOE_BLOCK - 1) + MOE_BLOCK - 1) // MOE_BLOCK
    slots = n_blocks * MOE_BLOCK
    buf_tok = jnp.zeros((slots,), jnp.int32).at[dest].set(flat_tok[order])
    buf_gate = jnp.zeros((slots,), jnp.float32).at[dest].set(flat_gate[order])
    blk_start = jnp.arange(n_blocks, dtype=jnp.int32) * MOE_BLOCK
    blk_expert = jnp.minimum(jnp.searchsorted(p_ends, blk_start, side='right'), N_EXPERTS - 1)

    def expert_block(args):
        e, tok, g = args
        xb = h[tok]
        a = xb @ w_gate[e]
        b = xb @ w_up[e]
        yb = (jax.nn.silu(a) * b) @ w_down[e]
        return yb * g[:, None].astype(yb.dtype)

    ys = lax.map(expert_block, (blk_expert, buf_tok.reshape(n_blocks, MOE_BLOCK), buf_gate.reshape(n_blocks, MOE_BLOCK)))
    return jax.ops.segment_sum(ys.reshape(slots, -1), buf_tok, num_segments=T)


def setup_inputs(seed: int = 0) -> dict:
    key = jax.random.key(seed)
    ks = jax.random.split(key, 19)
    D = D_MODEL
    f32 = jnp.float32

    def nrm(k, shape, scale):
        return jax.random.normal(k, shape, f32) * scale

    return {
        'x': nrm(ks[0], (BATCH, SEQ, D), 1.0),
        'c': nrm(ks[1], (BATCH, D), 1.0),
        'ctx': nrm(ks[2], (BATCH, CTX_LEN, D), 1.0),
        'c_ctx': nrm(ks[3], (D,), 1.0),
        'w_mod': nrm(ks[4], (DEPTH, D, 6 * D), 0.5 * D ** -0.5),
        'b_mod': nrm(ks[5], (DEPTH, 6 * D), 0.02),
        'ln_g': 1.0 + nrm(ks[6], (DEPTH, 2, D), 0.02),
        'ln_b': nrm(ks[7], (DEPTH, 2, D), 0.02),
        'w_qkv': nrm(ks[8], (N_ATTN_LAYERS, D, QKV_DIM), D ** -0.5),
        'q_gain': 1.0 + nrm(ks[9], (N_ATTN_LAYERS, HEAD_DIM), 0.02),
        'k_gain': 1.0 + nrm(ks[10], (N_ATTN_LAYERS, HEAD_DIM), 0.02),
        'w_o': nrm(ks[11], (N_ATTN_LAYERS, D, D), DEEPNORM_BETA * D ** -0.5),
        'w_pool': nrm(ks[12], (N_POOL_LAYERS, N_POOL_GROUPS, POOL_GROUP_DIM, POOL_GROUP_DIM), DEEPNORM_BETA * POOL_GROUP_DIM ** -0.5),
        'pool_scale': 1.0 + nrm(ks[13], (N_POOL_LAYERS, D), 0.02),
        'w_router': nrm(ks[14], (D, N_EXPERTS), D ** -0.5),
        'router_bias': nrm(ks[15], (N_EXPERTS,), 0.01),
        'w_gate': nrm(ks[16], (DEPTH, N_EXPERTS, D, D_EXPERT), D ** -0.5),
        'w_up': nrm(ks[17], (DEPTH, N_EXPERTS, D, D_EXPERT), D ** -0.5),
        'w_down': nrm(ks[18], (DEPTH, N_EXPERTS, D_EXPERT, D), DEEPNORM_BETA * D_EXPERT ** -0.5),
    }


def reference(x, c, ctx, c_ctx, w_mod, b_mod, ln_g, ln_b, w_qkv, q_gain, k_gain, w_o, w_pool, pool_scale, w_router, router_bias, w_gate, w_up, w_down):
    B, L, D = x.shape
    C = ctx.shape[1]
    n_rows = L // GRID_W
    rope_cos, rope_sin = axial_rope_tables(n_rows, x.dtype)
    xc = ctx
    s_lat = jax.nn.silu(c)
    s_ctx = jax.nn.silu(c_ctx)
    for i in range(DEPTH):
        need_ctx = i < DEPTH - 1
        mixer = i % N_MIXERS
        idx = i // N_MIXERS
        mod = (s_lat @ w_mod[i] + b_mod[i])[:, None, :]
        sh1, sc1, g1, sh2, sc2, g2 = jnp.split(mod, 6, axis=-1)
        u = modulate(x, sh1, sc1)
        if mixer == 0 or need_ctx:
            modc = s_ctx @ w_mod[i] + b_mod[i]
            csh1, csc1, cg1, csh2, csc2, cg2 = jnp.split(modc, 6, axis=-1)
            uc = modulate(xc, csh1, csc1)
        if mixer == 0:
            y, yc = attention_mixer(u, uc, rope_cos, rope_sin, w_qkv[idx], q_gain[idx], k_gain[idx], w_o[idx], need_ctx)
        else:
            y = pool_mixer(u, w_pool[idx], pool_scale[idx])
            yc = pool_mixer(uc, w_pool[idx], pool_scale[idx]) if need_ctx else None
        x = layer_norm(DEEPNORM_ALPHA * x + g1 * y, ln_g[i, 0], ln_b[i, 0])
        h = modulate(x, sh2, sc2).reshape(B * L, D)
        if need_ctx:
            xc = layer_norm(DEEPNORM_ALPHA * xc + cg1 * yc, ln_g[i, 0], ln_b[i, 0])
            hc = modulate(xc, csh2, csc2).reshape(B * C, D)
            f_all = moe_ffn(jnp.concatenate([hc, h], axis=0), w_router, router_bias, w_gate[i], w_up[i], w_down[i])
            fc = f_all[:B * C].reshape(B, C, D)
            f = f_all[B * C:].reshape(B, L, D)
            xc = layer_norm(DEEPNORM_ALPHA * xc + cg2 * fc, ln_g[i, 1], ln_b[i, 1])
        else:
            f = moe_ffn(h, w_router, router_bias, w_gate[i], w_up[i], w_down[i]).reshape(B, L, D)
        x = layer_norm(DEEPNORM_ALPHA * x + g2 * f, ln_g[i, 1], ln_b[i, 1])
    return x
```

```python
import functools

import jax
import jax.numpy as jnp
from jax import lax
from jax.experimental import pallas as pl
from jax.experimental.pallas import tpu as pltpu

N_HEADS = 16
N_KV_HEADS = 4
HEAD_DIM = 128
Q_PER_KV = N_HEADS // N_KV_HEADS
GRID_W = 64
ROPE_THETA = 10000.0
ROPE_PAIRS = HEAD_DIM // 4
ATTN_SCALE = HEAD_DIM ** -0.5
POOL_WINDOWS = (2, 4, 8, 16)
POOL_HALO = 8
N_EXPERTS = 32
N_EXPERT_GROUPS = 8
EXPERTS_PER_GROUP = N_EXPERTS // N_EXPERT_GROUPS
DEPTH = 2
DEEPNORM_ALPHA = (2 * DEPTH) ** 0.25
LN_EPS = 1e-6
RMS_EPS = 1e-6

LANES = 128
SUBLANES = 8
ROW_TILE = 256
MOE_BLOCK = 256
MOD_ROWS = 8
MOD_TN = 512
VMEM_LIMIT = 56 * 1024 * 1024

f32 = jnp.float32
bf16 = jnp.bfloat16


def _cparams(*sem):
    return pltpu.CompilerParams(dimension_semantics=sem, vmem_limit_bytes=VMEM_LIMIT)


def _mod_kernel(c_ref, w_ref, b_ref, o_ref, *, n_rows):
    d = w_ref.shape[1]
    tn = w_ref.shape[2]
    nj = tn // LANES

    def body(k, accs):
        r0 = pl.multiple_of(k * SUBLANES, SUBLANES)
        w8 = w_ref[0, pl.ds(r0, SUBLANES), :]
        out = []
        for r in range(n_rows):
            c8 = c_ref[r, pl.ds(r0, SUBLANES), :]
            s8 = c8 * jax.nn.sigmoid(c8)
            for j in range(nj):
                out.append(accs[r * nj + j] + s8 * w8[:, j * LANES:(j + 1) * LANES])
        return tuple(out)

    init = tuple(jnp.zeros((SUBLANES, LANES), f32) for _ in range(n_rows * nj))
    accs = lax.fori_loop(0, d // SUBLANES, body, init, unroll=2)
    rows = []
    for r in range(n_rows):
        row = jnp.concatenate(
            [jnp.sum(accs[r * nj + j], axis=0, keepdims=True) for j in range(nj)], axis=1)
        rows.append(row + b_ref[0])
    rows.append(jnp.zeros((MOD_ROWS - n_rows, tn), f32))
    o_ref[0] = jnp.concatenate(rows, axis=0)


def _modulation(cond, w_mod, b_mod):
    n_rows, d = cond.shape
    depth, _, n6 = w_mod.shape
    c_b = jnp.broadcast_to(cond[:, :, None], (n_rows, d, LANES))
    return pl.pallas_call(
        functools.partial(_mod_kernel, n_rows=n_rows),
        out_shape=jax.ShapeDtypeStruct((depth, MOD_ROWS, n6), f32),
        grid=(depth, n6 // MOD_TN),
        in_specs=[
            pl.BlockSpec((n_rows, d, LANES), lambda l, j: (0, 0, 0)),
            pl.BlockSpec((1, d, MOD_TN), lambda l, j: (l, 0, j)),
            pl.BlockSpec((1, 1, MOD_TN), lambda l, j: (l, 0, j)),
        ],
        out_specs=pl.BlockSpec((1, MOD_ROWS, MOD_TN), lambda l, j: (l, 0, j)),
        compiler_params=_cparams("arbitrary", "arbitrary"),
        name="adaln_mod",
    )(c_b, w_mod, b_mod.reshape(depth, 1, n6))


def _qkv_kernel(x_ref, mod_ref, w_ref, qg_ref, kg_ref, cos_ref, sin_ref, q_ref, k_ref, v_ref):
    tm = x_ref.shape[0]
    d = x_ref.shape[1]
    kv_dim = k_ref.shape[1]
    sh1 = mod_ref[0, 0:1, :]
    sc1 = mod_ref[0, 1:2, :]
    u = (x_ref[...] * (1.0 + sc1) + sh1).astype(bf16)
    qkv = jnp.dot(u, w_ref[...], preferred_element_type=f32)
    cos = cos_ref[...]
    sin = sin_ref[...]
    lane = lax.broadcasted_iota(jnp.int32, (tm, HEAD_DIM), 1)
    first = (lane % (2 * ROPE_PAIRS)) < ROPE_PAIRS

    def norm_rope(xh, gain, scale):
        ms = jnp.mean(xh * xh, axis=-1, keepdims=True)
        xn = xh * lax.rsqrt(ms + RMS_EPS) * gain
        rot = jnp.where(first, pltpu.roll(xn, HEAD_DIM - ROPE_PAIRS, 1), pltpu.roll(xn, ROPE_PAIRS, 1))
        y = xn * cos + rot * sin
        return y * scale if scale != 1.0 else y

    qg = qg_ref[...]
    kg = kg_ref[...]
    for h in range(N_HEADS):
        sl = slice(h * HEAD_DIM, (h + 1) * HEAD_DIM)
        q_ref[:, sl] = norm_rope(qkv[:, sl], qg, ATTN_SCALE).astype(bf16)
    for h in range(N_KV_HEADS):
        sl = slice(h * HEAD_DIM, (h + 1) * HEAD_DIM)
        k_ref[:, sl] = norm_rope(qkv[:, d + h * HEAD_DIM:d + (h + 1) * HEAD_DIM], kg, 1.0).astype(bf16)
    v_ref[...] = qkv[:, d + kv_dim:].astype(bf16)


def _qkv_proj(x_all, mod_l, w_qkv_bf, q_gain, k_gain, cos_t, sin_t, tiles_per_batch, seg_of_tile):
    t_all, d = x_all.shape
    kv_dim = N_KV_HEADS * HEAD_DIM
    n_tiles = t_all // ROW_TILE
    return pl.pallas_call(
        _qkv_kernel,
        out_shape=(jax.ShapeDtypeStruct((t_all, d), bf16),
                   jax.ShapeDtypeStruct((t_all, kv_dim), bf16),
                   jax.ShapeDtypeStruct((t_all, kv_dim), bf16)),
        grid=(n_tiles,),
        in_specs=[
            pl.BlockSpec((ROW_TILE, d), lambda i: (i, 0)),
            pl.BlockSpec((1, 6, d), lambda i: (seg_of_tile(i), 0, 0)),
            pl.BlockSpec((d, d + 2 * kv_dim), lambda i: (0, 0)),
            pl.BlockSpec((1, HEAD_DIM), lambda i: (0, 0)),
            pl.BlockSpec((1, HEAD_DIM), lambda i: (0, 0)),
            pl.BlockSpec((ROW_TILE, HEAD_DIM), lambda i: (i % tiles_per_batch, 0)),
            pl.BlockSpec((ROW_TILE, HEAD_DIM), lambda i: (i % tiles_per_batch, 0)),
        ],
        out_specs=(pl.BlockSpec((ROW_TILE, d), lambda i: (i, 0)),
                   pl.BlockSpec((ROW_TILE, kv_dim), lambda i: (i, 0)),
                   pl.BlockSpec((ROW_TILE, kv_dim), lambda i: (i, 0))),
        compiler_params=_cparams("arbitrary"),
        name="qkv_rope",
    )(x_all, mod_l, w_qkv_bf, q_gain, k_gain, cos_t, sin_t)


def _attn_kernel(q_ref, k_ref, v_ref, o_ref, m_sc, l_sc, acc_sc, *, tk, n_ctx_tiles, n_ctx_chunks):
    tq = q_ref.shape[0]
    n_chunks = k_ref.shape[0] // tk
    qi = pl.program_id(2)
    n_kv = jnp.where(qi < n_ctx_tiles, n_ctx_chunks, n_chunks)
    qs = jnp.concatenate(
        [q_ref[:, h * HEAD_DIM:(h + 1) * HEAD_DIM] for h in range(Q_PER_KV)], axis=0)
    m_sc[...] = jnp.full(m_sc.shape, -jnp.inf, f32)
    l_sc[...] = jnp.zeros(l_sc.shape, f32)
    acc_sc[...] = jnp.zeros(acc_sc.shape, f32)

    def body(c, carry):
        off = pl.multiple_of(c * tk, tk)
        kc = k_ref[pl.ds(off, tk), :]
        vc = v_ref[pl.ds(off, tk), :]
        s = lax.dot_general(qs, kc, (((1,), (1,)), ((), ())), preferred_element_type=f32)
        m_prev = m_sc[...]
        m_new = jnp.maximum(m_prev, jnp.max(s, axis=-1, keepdims=True))
        p = jnp.exp(s - m_new)
        a = jnp.exp(m_prev - m_new)
        l_sc[...] = a * l_sc[...] + jnp.sum(p, axis=-1, keepdims=True)
        acc_sc[...] = a * acc_sc[...] + jnp.dot(p.astype(bf16), vc, preferred_element_type=f32)
        m_sc[...] = m_new
        return carry

    lax.fori_loop(0, n_kv, body, 0)
    o = acc_sc[...] / l_sc[...]
    for h in range(Q_PER_KV):
        o_ref[:, h * HEAD_DIM:(h + 1) * HEAD_DIM] = o[h * tq:(h + 1) * tq].astype(bf16)


def _attention(q, k, v, n_batch, tiles_per_batch, n_ctx_tiles, ctx_len):
    t_all, d = q.shape
    rows_per_batch = t_all // n_batch
    tk = ctx_len
    gw = Q_PER_KV * HEAD_DIM
    kern = functools.partial(_attn_kernel, tk=tk, n_ctx_tiles=n_ctx_tiles, n_ctx_chunks=ctx_len // tk)
    return pl.pallas_call(
        kern,
        out_shape=jax.ShapeDtypeStruct((t_all, d), bf16),
        grid=(n_batch, N_KV_HEADS, tiles_per_batch),
        in_specs=[
            pl.BlockSpec((ROW_TILE, gw), lambda b, g, i: (b * tiles_per_batch + i, g)),
            pl.BlockSpec((rows_per_batch, HEAD_DIM), lambda b, g, i: (b, g)),
            pl.BlockSpec((rows_per_batch, HEAD_DIM), lambda b, g, i: (b, g)),
        ],
        out_specs=pl.BlockSpec((ROW_TILE, gw), lambda b, g, i: (b * tiles_per_batch + i, g)),
        scratch_shapes=[
            pltpu.VMEM((Q_PER_KV * ROW_TILE, 1), f32),
            pltpu.VMEM((Q_PER_KV * ROW_TILE, 1), f32),
            pltpu.VMEM((Q_PER_KV * ROW_TILE, HEAD_DIM), f32),
        ],
        compiler_params=_cparams("arbitrary", "arbitrary", "arbitrary"),
        name="gqa_attention",
    )(q, k, v)


def _post_epilogue(x, y, mod_ref, lng_ref, lnb_ref, wr_ref, rb_ref, x1_ref, h_ref, e_ref, g_ref):
    g1 = mod_ref[0, 2:3, :]
    sh2 = mod_ref[0, 3:4, :]
    sc2 = mod_ref[0, 4:5, :]
    z = DEEPNORM_ALPHA * x + g1 * y
    mu = jnp.mean(z, axis=-1, keepdims=True)
    zc = z - mu
    var = jnp.mean(zc * zc, axis=-1, keepdims=True)
    x1 = zc * lax.rsqrt(var + LN_EPS) * lng_ref[...] + lnb_ref[...]
    h = x1 * (1.0 + sc2) + sh2
    x1_ref[...] = x1
    h_ref[...] = h

    logits = jnp.dot(h, wr_ref[...], preferred_element_type=f32, precision=lax.Precision.HIGHEST)
    lt = logits.T
    aff = jax.nn.sigmoid(lt)
    sel = aff + rb_ref[...]
    ng = N_EXPERT_GROUPS
    a = [aff[j * ng:(j + 1) * ng] for j in range(EXPERTS_PER_GROUP)]
    s = [sel[j * ng:(j + 1) * ng] for j in range(EXPERTS_PER_GROUP)]
    gs = None
    for i in range(EXPERTS_PER_GROUP):
        for j in range(i + 1, EXPERTS_PER_GROUP):
            pij = s[i] + s[j]
            gs = pij if gs is None else jnp.maximum(gs, pij)
    gio = lax.broadcasted_iota(jnp.int32, gs.shape, 0)
    gmax = jnp.max(gs, axis=0, keepdims=True)
    g_idx = jnp.min(jnp.where(gs == gmax, gio, ng), axis=0, keepdims=True)
    onehot = gio == g_idx
    sin_g = [jnp.sum(jnp.where(onehot, s[j], 0.0), axis=0, keepdims=True) for j in range(EXPERTS_PER_GROUP)]
    aff_g = [jnp.sum(jnp.where(onehot, a[j], 0.0), axis=0, keepdims=True) for j in range(EXPERTS_PER_GROUP)]
    loc = [jnp.zeros_like(g_idx), jnp.zeros_like(g_idx)]
    gat = [jnp.zeros_like(gmax), jnp.zeros_like(gmax)]
    for j in range(EXPERTS_PER_GROUP):
        rank = jnp.zeros_like(g_idx)
        for i in range(EXPERTS_PER_GROUP):
            if i == j:
                continue
            ahead = (sin_g[i] >= sin_g[j]) if i < j else (sin_g[i] > sin_g[j])
            rank = rank + ahead.astype(jnp.int32)
        for kk in range(2):
            hit = rank == kk
            loc[kk] = jnp.where(hit, j, loc[kk])
            gat[kk] = jnp.where(hit, aff_g[j], gat[kk])
    den = gat[0] + gat[1]
    e_ref[...] = jnp.concatenate(
        [g_idx * EXPERTS_PER_GROUP + loc[0], g_idx * EXPERTS_PER_GROUP + loc[1]], axis=0)
    g_ref[...] = jnp.concatenate([gat[0] / den, gat[1] / den], axis=0)


def _attn_post_kernel(x_ref, o_ref, wo_ref, mod_ref, lng_ref, lnb_ref, wr_ref, rb_ref,
                      x1_ref, h_ref, e_ref, g_ref):
    y = jnp.dot(o_ref[...], wo_ref[...], preferred_element_type=f32)
    _post_epilogue(x_ref[...], y, mod_ref, lng_ref, lnb_ref, wr_ref, rb_ref, x1_ref, h_ref, e_ref, g_ref)


def _post_out(n_rows, d):
    shapes = (jax.ShapeDtypeStruct((n_rows, d), f32),
              jax.ShapeDtypeStruct((n_rows, d), f32),
              jax.ShapeDtypeStruct((2, n_rows), jnp.int32),
              jax.ShapeDtypeStruct((2, n_rows), f32))
    specs = (pl.BlockSpec((ROW_TILE, d), lambda i: (i, 0)),
             pl.BlockSpec((ROW_TILE, d), lambda i: (i, 0)),
             pl.BlockSpec((2, ROW_TILE), lambda i: (0, i)),
             pl.BlockSpec((2, ROW_TILE), lambda i: (0, i)))
    return shapes, specs


def _attn_post(x_all, o, w_o_bf, mod_l, ln_g, ln_b, wr_p, rb_p, seg_of_tile):
    t_all, d = x_all.shape
    shapes, specs = _post_out(t_all, d)
    return pl.pallas_call(
        _attn_post_kernel,
        out_shape=shapes,
        grid=(t_all // ROW_TILE,),
        in_specs=[
            pl.BlockSpec((ROW_TILE, d), lambda i: (i, 0)),
            pl.BlockSpec((ROW_TILE, d), lambda i: (i, 0)),
            pl.BlockSpec((d, d), lambda i: (0, 0)),
            pl.BlockSpec((1, 6, d), lambda i: (seg_of_tile(i), 0, 0)),
            pl.BlockSpec((1, d), lambda i: (0, 0)),
            pl.BlockSpec((1, d), lambda i: (0, 0)),
            pl.BlockSpec((d, N_EXPERTS), lambda i: (0, 0)),
            pl.BlockSpec((N_EXPERTS, 1), lambda i: (0, 0)),
        ],
        out_specs=specs,
        compiler_params=_cparams("arbitrary"),
        name="attn_out_norm_route",
    )(x_all, o, w_o_bf, mod_l, ln_g, ln_b, wr_p, rb_p)


def _pool_post_kernel(x_ref, xp_ref, xn_ref, wp_ref, ps_ref, mod_ref, lng_ref, lnb_ref, wr_ref, rb_ref,
                      x1_ref, h_ref, e_ref, g_ref, *, seq_len, tiles_per_seq):
    tm, d = x_ref.shape
    gd = d // len(POOL_WINDOWS)
    r = pl.program_id(0) % tiles_per_seq
    sh1 = mod_ref[0, 0:1, :]
    sc1 = mod_ref[0, 1:2, :]
    x = x_ref[...]
    ext = jnp.concatenate([xp_ref[...], x, xn_ref[...]], axis=0)
    n_ext = tm + 2 * POOL_HALO
    pos = r * tm - POOL_HALO + lax.broadcasted_iota(jnp.int32, (n_ext, 1), 0)
    inside = (pos >= 0) & (pos < seq_len)
    u_ext = jnp.where(inside, ext * (1.0 + sc1) + sh1, 0.0)
    t = pos[POOL_HALO:POOL_HALO + tm]
    ys = []
    for g, w in enumerate(POOL_WINDOWS):
        e = u_ext[:, g * gd:(g + 1) * gd]
        p = e + pltpu.roll(e, 1, 0)
        step = 1
        while 2 * step < w:
            p = pltpu.roll(p, step, 0) + pltpu.roll(p, n_ext - step, 0)
            step *= 2
        cnt = (jnp.clip(t + w // 2, 0, seq_len) - jnp.clip(t - w // 2, 0, seq_len)).astype(f32)
        mean = p[POOL_HALO:POOL_HALO + tm] / cnt
        dlt = (mean - e[POOL_HALO:POOL_HALO + tm]).astype(bf16)
        ys.append(jnp.dot(dlt, wp_ref[g], preferred_element_type=f32))
    y = jnp.concatenate(ys, axis=-1) * ps_ref[...]
    _post_epilogue(x, y, mod_ref, lng_ref, lnb_ref, wr_ref, rb_ref, x1_ref, h_ref, e_ref, g_ref)


def _pool_post(x_all, w_pool_bf, pool_scale, mod_l, ln_g, ln_b, wr_p, rb_p,
               n_batch, seq_len, tiles_per_batch, n_ctx_tiles):
    t_all, d = x_all.shape
    tiles_per_seq = seq_len // ROW_TILE
    n_rows = n_batch * seq_len
    halo_per_tile = ROW_TILE // POOL_HALO
    last_halo = t_all // POOL_HALO - 1
    ng = len(POOL_WINDOWS)

    def src_tile(i):
        return (i // tiles_per_seq) * tiles_per_batch + n_ctx_tiles + i % tiles_per_seq

    shapes, specs = _post_out(n_rows, d)
    kern = functools.partial(_pool_post_kernel, seq_len=seq_len, tiles_per_seq=tiles_per_seq)
    return pl.pallas_call(
        kern,
        out_shape=shapes,
        grid=(n_rows // ROW_TILE,),
        in_specs=[
            pl.BlockSpec((ROW_TILE, d), lambda i: (src_tile(i), 0)),
            pl.BlockSpec((POOL_HALO, d), lambda i: (jnp.maximum(src_tile(i) * halo_per_tile - 1, 0), 0)),
            pl.BlockSpec((POOL_HALO, d), lambda i: (jnp.minimum((src_tile(i) + 1) * halo_per_tile, last_halo), 0)),
            pl.BlockSpec((ng, d // ng, d // ng), lambda i: (0, 0, 0)),
            pl.BlockSpec((1, d), lambda i: (0, 0)),
            pl.BlockSpec((1, 6, d), lambda i: (1 + i // tiles_per_seq, 0, 0)),
            pl.BlockSpec((1, d), lambda i: (0, 0)),
            pl.BlockSpec((1, d), lambda i: (0, 0)),
            pl.BlockSpec((d, N_EXPERTS), lambda i: (0, 0)),
            pl.BlockSpec((N_EXPERTS, 1), lambda i: (0, 0)),
        ],
        out_specs=specs,
        compiler_params=_cparams("arbitrary"),
        name="pool_norm_route",
    )(x_all, x_all, x_all, w_pool_bf, pool_scale, mod_l, ln_g, ln_b, wr_p, rb_p)


def _ffn_kernel(be_ref, nu_ref, tok_cur, tok_nxt, gate_ref, h_hbm, wg_ref, wu_ref, wd_ref, y_ref,
                xbuf, sem):
    b = pl.program_id(0)
    n_blocks = pl.num_programs(0)
    nu = nu_ref[0]
    slot = b % 2

    def start_gather(tok_ref, dst_slot):
        for r in range(MOE_BLOCK):
            pltpu.make_async_copy(
                h_hbm.at[pl.ds(tok_ref[0, 0, r], 1), :],
                xbuf.at[dst_slot, pl.ds(r, 1), :],
                sem.at[dst_slot]).start()

    def wait_gather(dst_slot):
        pltpu.make_async_copy(h_hbm.at[pl.ds(0, MOE_BLOCK), :], xbuf.at[dst_slot], sem.at[dst_slot]).wait()

    @pl.when(b == 0)
    def _():
        start_gather(tok_cur, 0)

    @pl.when(b < nu)
    def _():
        start_gather(tok_nxt, 1 - slot)
        wait_gather(slot)
        xb = xbuf[slot].astype(bf16)
        a = jnp.dot(xb, wg_ref[0], preferred_element_type=f32)
        u = jnp.dot(xb, wu_ref[0], preferred_element_type=f32)
        z = (a * jax.nn.sigmoid(a) * u).astype(bf16)
        y = jnp.dot(z, wd_ref[0], preferred_element_type=f32)
        y_ref[...] = y * gate_ref[...]

    @pl.when(b >= nu)
    def _():
        y_ref[...] = jnp.zeros(y_ref.shape, f32)

    @pl.when((b == nu) | ((b == n_blocks - 1) & (nu == n_blocks)))
    def _():
        wait_gather(jnp.where(b == nu, slot, 1 - slot))


def _expert_ffn(h, blk_expert, n_used, buf_tok, buf_gate, wg, wu, wd):
    t, d = h.shape
    de = wg.shape[2]
    n_blocks = blk_expert.shape[0]
    slots = n_blocks * MOE_BLOCK
    tok3 = buf_tok.reshape(n_blocks, 1, MOE_BLOCK)
    grid_spec = pltpu.PrefetchScalarGridSpec(
        num_scalar_prefetch=2,
        grid=(n_blocks,),
        in_specs=[
            pl.BlockSpec((1, 1, MOE_BLOCK), lambda b, be, nu: (b, 0, 0), memory_space=pltpu.SMEM),
            pl.BlockSpec((1, 1, MOE_BLOCK), lambda b, be, nu: (jnp.minimum(b + 1, n_blocks - 1), 0, 0),
                         memory_space=pltpu.SMEM),
            pl.BlockSpec((MOE_BLOCK, 1), lambda b, be, nu: (b, 0)),
            pl.BlockSpec(memory_space=pl.ANY),
            pl.BlockSpec((1, d, de), lambda b, be, nu: (be[b], 0, 0)),
            pl.BlockSpec((1, d, de), lambda b, be, nu: (be[b], 0, 0)),
            pl.BlockSpec((1, de, d), lambda b, be, nu: (be[b], 0, 0)),
        ],
        out_specs=pl.BlockSpec((MOE_BLOCK, d), lambda b, be, nu: (b, 0)),
        scratch_shapes=[pltpu.VMEM((2, MOE_BLOCK, d), f32), pltpu.SemaphoreType.DMA((2,))],
    )
    return pl.pallas_call(
        _ffn_kernel,
        out_shape=jax.ShapeDtypeStruct((slots, d), f32),
        grid_spec=grid_spec,
        compiler_params=_cparams("arbitrary"),
        name="expert_ffn",
    )(blk_expert, n_used, tok3, tok3, buf_gate.reshape(slots, 1), h, wg, wu, wd)


def _fin_kernel(pos_cur, pos_nxt, x1_ref, mod_ref, lng_ref, lnb_ref, y_hbm, o_ref, ybuf, sem):
    i = pl.program_id(0)
    n = pl.num_programs(0)
    tm = x1_ref.shape[0]
    slot = i % 2

    def start_gather(pos_ref, dst_slot):
        for kk in range(2):
            for r in range(tm):
                pltpu.make_async_copy(
                    y_hbm.at[pl.ds(pos_ref[0, kk, r], 1), :],
                    ybuf.at[dst_slot, kk, pl.ds(r, 1), :],
                    sem.at[dst_slot]).start()

    def wait_gather(dst_slot):
        for kk in range(2):
            pltpu.make_async_copy(y_hbm.at[pl.ds(0, tm), :], ybuf.at[dst_slot, kk], sem.at[dst_slot]).wait()

    @pl.when(i == 0)
    def _():
        start_gather(pos_cur, 0)

    @pl.when(i + 1 < n)
    def _():
        start_gather(pos_nxt, 1 - slot)

    wait_gather(slot)
    g2 = mod_ref[0, 5:6, :]
    f = ybuf[slot, 0] + ybuf[slot, 1]
    z = DEEPNORM_ALPHA * x1_ref[...] + g2 * f
    mu = jnp.mean(z, axis=-1, keepdims=True)
    zc = z - mu
    var = jnp.mean(zc * zc, axis=-1, keepdims=True)
    o_ref[...] = zc * lax.rsqrt(var + LN_EPS) * lng_ref[...] + lnb_ref[...]


def _combine_norm(x1, pos, y_sorted, mod_l, ln_g, ln_b, seg_of_tile):
    t, d = x1.shape
    n_tiles = t // ROW_TILE
    pos3 = pos.reshape(2, n_tiles, ROW_TILE).transpose(1, 0, 2)
    return pl.pallas_call(
        _fin_kernel,
        out_shape=jax.ShapeDtypeStruct((t, d), f32),
        grid=(n_tiles,),
        in_specs=[
            pl.BlockSpec((1, 2, ROW_TILE), lambda i: (i, 0, 0), memory_space=pltpu.SMEM),
            pl.BlockSpec((1, 2, ROW_TILE), lambda i: (jnp.minimum(i + 1, n_tiles - 1), 0, 0),
                         memory_space=pltpu.SMEM),
            pl.BlockSpec((ROW_TILE, d), lambda i: (i, 0)),
            pl.BlockSpec((1, 6, d), lambda i: (seg_of_tile(i), 0, 0)),
            pl.BlockSpec((1, d), lambda i: (0, 0)),
            pl.BlockSpec((1, d), lambda i: (0, 0)),
            pl.BlockSpec(memory_space=pl.ANY),
        ],
        out_specs=pl.BlockSpec((ROW_TILE, d), lambda i: (i, 0)),
        scratch_shapes=[pltpu.VMEM((2, 2, ROW_TILE, d), f32), pltpu.SemaphoreType.DMA((2,))],
        compiler_params=_cparams("arbitrary"),
        name="combine_norm",
    )(pos3, pos3, x1, mod_l, ln_g, ln_b, y_sorted)


def _dispatch_plan(e_idx, gate):
    t = e_idx.shape[1]
    n_assign = 2 * t
    e_flat = e_idx.reshape(-1)
    tok_flat = jnp.tile(jnp.arange(t, dtype=jnp.int32), 2)
    onehot = (e_flat[:, None] == jnp.arange(N_EXPERTS, dtype=jnp.int32)[None, :]).astype(jnp.int32)
    csum = jnp.cumsum(onehot, axis=0)
    rank = jnp.sum((csum - onehot) * onehot, axis=1)
    counts = csum[-1]
    padded = (counts + MOE_BLOCK - 1) // MOE_BLOCK * MOE_BLOCK
    p_ends = jnp.cumsum(padded)
    p_starts = p_ends - padded
    dest = (p_starts[e_flat] + rank).astype(jnp.int32)
    n_blocks = (n_assign + N_EXPERTS * (MOE_BLOCK - 1) + MOE_BLOCK - 1) // MOE_BLOCK
    slots = n_blocks * MOE_BLOCK
    buf_tok = jnp.zeros((slots,), jnp.int32).at[dest].set(tok_flat)
    buf_gate = jnp.zeros((slots,), f32).at[dest].set(gate.reshape(-1))
    blk_start = jnp.arange(n_blocks, dtype=jnp.int32) * MOE_BLOCK
    blk_expert = jnp.minimum(jnp.searchsorted(p_ends, blk_start, side='right'), N_EXPERTS - 1).astype(jnp.int32)
    n_used = (p_ends[-1] // MOE_BLOCK).astype(jnp.int32).reshape(1)
    return blk_expert, n_used, buf_tok, buf_gate, dest.reshape(2, t)


def _moe(h, e_idx, gate, wg, wu, wd):
    blk_expert, n_used, buf_tok, buf_gate, pos = _dispatch_plan(e_idx, gate)
    y_sorted = _expert_ffn(h, blk_expert, n_used, buf_tok, buf_gate, wg, wu, wd)
    return y_sorted, pos


def _rope_tables(ctx_len, seq_len):
    t = jnp.arange(seq_len)
    row = (t // GRID_W).astype(f32)
    col = (t % GRID_W).astype(f32)
    inv_freq = ROPE_THETA ** (-jnp.arange(ROPE_PAIRS, dtype=f32) / ROPE_PAIRS)
    ang_r = row[:, None] * inv_freq
    ang_c = col[:, None] * inv_freq
    ang = jnp.concatenate([ang_r, ang_r, ang_c, ang_c], axis=-1)
    cos = jnp.cos(ang)
    sin = jnp.sin(ang)
    lane = jnp.arange(HEAD_DIM)
    sign = jnp.where((lane % (2 * ROPE_PAIRS)) < ROPE_PAIRS, -1.0, 1.0).astype(f32)
    cos_t = jnp.concatenate([jnp.ones((ctx_len, HEAD_DIM), f32), cos], axis=0)
    sin_t = jnp.concatenate([jnp.zeros((ctx_len, HEAD_DIM), f32), sin * sign], axis=0)
    return cos_t, sin_t


def kernel(x, c, ctx, c_ctx, w_mod, b_mod, ln_g, ln_b, w_qkv, q_gain, k_gain, w_o, w_pool, pool_scale,
           w_router, router_bias, w_gate, w_up, w_down):
    n_batch, seq_len, d = x.shape
    ctx_len = ctx.shape[1]
    assert w_mod.shape[0] == DEPTH and d == N_HEADS * HEAD_DIM
    assert ctx_len % ROW_TILE == 0 and seq_len % ROW_TILE == 0 and 1 + n_batch <= MOD_ROWS
    rows_per_batch = ctx_len + seq_len
    tiles_per_batch = rows_per_batch // ROW_TILE
    n_ctx_tiles = ctx_len // ROW_TILE

    def seg_of_tile(i):
        return jnp.where(i % tiles_per_batch < n_ctx_tiles, 0, 1 + i // tiles_per_batch)

    cond = jnp.concatenate([c_ctx[None, :], c], axis=0)
    mod = _modulation(cond, w_mod, b_mod).reshape(DEPTH, MOD_ROWS, 6, d)

    wr_p = w_router.reshape(d, N_EXPERT_GROUPS, EXPERTS_PER_GROUP).transpose(0, 2, 1).reshape(d, N_EXPERTS)
    rb_p = router_bias.reshape(N_EXPERT_GROUPS, EXPERTS_PER_GROUP).T.reshape(N_EXPERTS, 1)
    cos_t, sin_t = _rope_tables(ctx_len, seq_len)

    x_all = jnp.concatenate([ctx, x], axis=1).reshape(n_batch * rows_per_batch, d)
    q, k, v = _qkv_proj(x_all, mod[0], w_qkv[0].astype(bf16), q_gain[0:1], k_gain[0:1], cos_t, sin_t,
                        tiles_per_batch, seg_of_tile)
    o = _attention(q, k, v, n_batch, tiles_per_batch, n_ctx_tiles, ctx_len)
    x1, h, e_idx, gate = _attn_post(x_all, o, w_o[0].astype(bf16), mod[0], ln_g[0, 0:1], ln_b[0, 0:1],
                                    wr_p, rb_p, seg_of_tile)
    y_sorted, pos = _moe(h, e_idx, gate, w_gate[0].astype(bf16), w_up[0].astype(bf16), w_down[0].astype(bf16))
    x_all = _combine_norm(x1, pos, y_sorted, mod[0], ln_g[0, 1:2], ln_b[0, 1:2], seg_of_tile)

    tiles_per_seq = seq_len // ROW_TILE
    x1, h, e_idx, gate = _pool_post(x_all, w_pool[0].astype(bf16), pool_scale[0:1], mod[1],
                                    ln_g[1, 0:1], ln_b[1, 0:1], wr_p, rb_p,
                                    n_batch, seq_len, tiles_per_batch, n_ctx_tiles)
    y_sorted, pos = _moe(h, e_idx, gate, w_gate[1].astype(bf16), w_up[1].astype(bf16), w_down[1].astype(bf16))
    out = _combine_norm(x1, pos, y_sorted, mod[1], ln_g[1, 1:2], ln_b[1, 1:2],
                        lambda i: 1 + i // tiles_per_seq)
    return out.reshape(n_batch, seq_len, d)
```

```python
import functools

import jax
import jax.numpy as jnp
from jax import lax
from jax.experimental import pallas as pl
from jax.experimental.pallas import tpu as pltpu

N_HEADS = 16
N_KV_HEADS = 4
HEAD_DIM = 128
Q_PER_KV = N_HEADS // N_KV_HEADS
GRID_W = 64
ROPE_THETA = 10000.0
ROPE_PAIRS = HEAD_DIM // 4
ATTN_SCALE = HEAD_DIM ** -0.5
Q_SCALE = ATTN_SCALE * 1.4426950408889634
POOL_WINDOWS = (2, 4, 8, 16)
POOL_HALO = 8
N_EXPERTS = 32
N_EXPERT_GROUPS = 8
EXPERTS_PER_GROUP = N_EXPERTS // N_EXPERT_GROUPS
DEPTH = 2
DEEPNORM_ALPHA = (2 * DEPTH) ** 0.25
LN_EPS = 1e-6
RMS_EPS = 1e-6

LANES = 128
SUBLANES = 8
ROW_TILE = 256
ATTN_TK = 512
ATTN_ROWS = 128
MOE_BLOCK = 256
MOD_ROWS = 8
MOD_TN = 512
VMEM_LIMIT = 56 * 1024 * 1024

f32 = jnp.float32
bf16 = jnp.bfloat16


def _cparams(*sem):
    return pltpu.CompilerParams(dimension_semantics=sem, vmem_limit_bytes=VMEM_LIMIT)


def _mod_kernel(c_ref, w_ref, b_ref, o_ref, *, n_rows):
    d = w_ref.shape[1]
    tn = w_ref.shape[2]
    nj = tn // LANES

    def body(k, accs):
        r0 = pl.multiple_of(k * SUBLANES, SUBLANES)
        w8 = w_ref[0, pl.ds(r0, SUBLANES), :]
        out = []
        for r in range(n_rows):
            c8 = c_ref[r, pl.ds(r0, SUBLANES), :]
            s8 = c8 * jax.nn.sigmoid(c8)
            for j in range(nj):
                out.append(accs[r * nj + j] + s8 * w8[:, j * LANES:(j + 1) * LANES])
        return tuple(out)

    init = tuple(jnp.zeros((SUBLANES, LANES), f32) for _ in range(n_rows * nj))
    accs = lax.fori_loop(0, d // SUBLANES, body, init, unroll=2)
    rows = []
    for r in range(n_rows):
        row = jnp.concatenate(
            [jnp.sum(accs[r * nj + j], axis=0, keepdims=True) for j in range(nj)], axis=1)
        rows.append(row + b_ref[0])
    rows.append(jnp.zeros((MOD_ROWS - n_rows, tn), f32))
    o_ref[0] = jnp.concatenate(rows, axis=0)


def _modulation(cond, w_mod, b_mod):
    n_rows, d = cond.shape
    depth, _, n6 = w_mod.shape
    c_b = jnp.broadcast_to(cond[:, :, None], (n_rows, d, LANES))
    return pl.pallas_call(
        functools.partial(_mod_kernel, n_rows=n_rows),
        out_shape=jax.ShapeDtypeStruct((depth, MOD_ROWS, n6), f32),
        grid=(depth, n6 // MOD_TN),
        in_specs=[
            pl.BlockSpec((n_rows, d, LANES), lambda l, j: (0, 0, 0)),
            pl.BlockSpec((1, d, MOD_TN), lambda l, j: (l, 0, j)),
            pl.BlockSpec((1, 1, MOD_TN), lambda l, j: (l, 0, j)),
        ],
        out_specs=pl.BlockSpec((1, MOD_ROWS, MOD_TN), lambda l, j: (l, 0, j)),
        compiler_params=_cparams("arbitrary", "arbitrary"),
        name="adaln_mod",
    )(c_b, w_mod, b_mod.reshape(depth, 1, n6))


def _qkv_kernel(x_ref, mod_ref, w_ref, qg_ref, kg_ref, cos_ref, sin_ref, q_ref, k_ref, v_ref):
    tm = x_ref.shape[0]
    d = x_ref.shape[1]
    kv_dim = k_ref.shape[1]
    sh1 = mod_ref[0, 0:1, :]
    sc1 = mod_ref[0, 1:2, :]
    u = (x_ref[...] * (1.0 + sc1) + sh1).astype(bf16)
    qkv = jnp.dot(u, w_ref[...], preferred_element_type=f32)
    cos = cos_ref[...]
    sin = sin_ref[...]
    lane = lax.broadcasted_iota(jnp.int32, (tm, HEAD_DIM), 1)
    first = (lane % (2 * ROPE_PAIRS)) < ROPE_PAIRS

    def norm_rope(xh, gain, scale):
        ms = jnp.mean(xh * xh, axis=-1, keepdims=True)
        xn = xh * lax.rsqrt(ms + RMS_EPS) * gain
        rot = jnp.where(first, pltpu.roll(xn, HEAD_DIM - ROPE_PAIRS, 1), pltpu.roll(xn, ROPE_PAIRS, 1))
        y = xn * cos + rot * sin
        return y * scale if scale != 1.0 else y

    qg = qg_ref[...]
    kg = kg_ref[...]
    for h in range(N_HEADS):
        sl = slice(h * HEAD_DIM, (h + 1) * HEAD_DIM)
        q_ref[:, sl] = norm_rope(qkv[:, sl], qg, Q_SCALE).astype(bf16)
    ones = jnp.ones((tm, HEAD_DIM), bf16)
    for h in range(N_KV_HEADS):
        sl = slice(h * HEAD_DIM, (h + 1) * HEAD_DIM)
        k_ref[:, sl] = norm_rope(qkv[:, d + h * HEAD_DIM:d + (h + 1) * HEAD_DIM], kg, 1.0).astype(bf16)
        v0 = d + kv_dim + h * HEAD_DIM
        v_ref[:, 2 * h * HEAD_DIM:(2 * h + 1) * HEAD_DIM] = qkv[:, v0:v0 + HEAD_DIM].astype(bf16)
        v_ref[:, (2 * h + 1) * HEAD_DIM:(2 * h + 2) * HEAD_DIM] = ones


def _qkv_proj(x_all, mod_l, w_qkv_bf, q_gain, k_gain, cos_t, sin_t, tiles_per_batch, seg_of_tile):
    t_all, d = x_all.shape
    kv_dim = N_KV_HEADS * HEAD_DIM
    n_tiles = t_all // ROW_TILE
    return pl.pallas_call(
        _qkv_kernel,
        out_shape=(jax.ShapeDtypeStruct((t_all, d), bf16),
                   jax.ShapeDtypeStruct((t_all, kv_dim), bf16),
                   jax.ShapeDtypeStruct((t_all, 2 * kv_dim), bf16)),
        grid=(n_tiles,),
        in_specs=[
            pl.BlockSpec((ROW_TILE, d), lambda i: (i, 0)),
            pl.BlockSpec((1, 6, d), lambda i: (seg_of_tile(i), 0, 0)),
            pl.BlockSpec((d, d + 2 * kv_dim), lambda i: (0, 0)),
            pl.BlockSpec((1, HEAD_DIM), lambda i: (0, 0)),
            pl.BlockSpec((1, HEAD_DIM), lambda i: (0, 0)),
            pl.BlockSpec((ROW_TILE, HEAD_DIM), lambda i: (i % tiles_per_batch, 0)),
            pl.BlockSpec((ROW_TILE, HEAD_DIM), lambda i: (i % tiles_per_batch, 0)),
        ],
        out_specs=(pl.BlockSpec((ROW_TILE, d), lambda i: (i, 0)),
                   pl.BlockSpec((ROW_TILE, kv_dim), lambda i: (i, 0)),
                   pl.BlockSpec((ROW_TILE, 2 * kv_dim), lambda i: (i, 0))),
        compiler_params=_cparams("arbitrary"),
        name="qkv_rope",
    )(x_all, mod_l, w_qkv_bf, q_gain, k_gain, cos_t, sin_t)


def _attn_kernel(q_ref, k_ref, v_ref, o_ref, m_sc, acc_sc, *, ctx_len, tk, rs, n_ctx_tiles):
    tq = q_ref.shape[0]
    n_lat_chunks = (k_ref.shape[0] - ctx_len) // tk
    qi = pl.program_id(2)
    subs = [(h, r0) for h in range(Q_PER_KV) for r0 in range(0, tq, rs)]

    def step(off, size, first):
        kc = k_ref[pl.ds(off, size), :]
        vc = v_ref[pl.ds(off, size), :]
        for i, (h, r0) in enumerate(subs):
            qh = q_ref[r0:r0 + rs, h * HEAD_DIM:(h + 1) * HEAD_DIM]
            s = lax.dot_general(qh, kc, (((1,), (1,)), ((), ())), preferred_element_type=f32)
            mx = s[:, 0:LANES]
            for j in range(1, size // LANES):
                mx = jnp.maximum(mx, s[:, j * LANES:(j + 1) * LANES])
            m_cur = jnp.max(mx, axis=-1, keepdims=True)
            if first:
                m_new = jnp.broadcast_to(m_cur, (rs, LANES))
            else:
                m_prev = m_sc[i]
                m_new = jnp.maximum(m_prev, m_cur)
            p = jnp.concatenate(
                [jnp.exp2(s[:, j * LANES:(j + 1) * LANES] - m_new) for j in range(size // LANES)],
                axis=1).astype(bf16)
            pv = jnp.dot(p, vc, preferred_element_type=f32)
            if first:
                acc_sc[i] = pv
            else:
                alpha = jnp.exp2(m_prev - m_new)
                acc_sc[i] = acc_sc[i] * jnp.concatenate([alpha, alpha], axis=1) + pv
            m_sc[i] = m_new

    step(0, ctx_len, True)

    @pl.when(qi >= n_ctx_tiles)
    def _():
        def body(c, carry):
            step(pl.multiple_of(ctx_len + c * tk, LANES), tk, False)
            return carry
        lax.fori_loop(0, n_lat_chunks, body, 0)

    for i, (h, r0) in enumerate(subs):
        acc = acc_sc[i]
        o_ref[r0:r0 + rs, h * HEAD_DIM:(h + 1) * HEAD_DIM] = (
            acc[:, :HEAD_DIM] / acc[:, HEAD_DIM:]).astype(bf16)


def _attention(q, k, v, n_batch, tiles_per_batch, n_ctx_tiles, ctx_len):
    t_all, d = q.shape
    rows_per_batch = t_all // n_batch
    gw = Q_PER_KV * HEAD_DIM
    n_sub = Q_PER_KV * (ROW_TILE // ATTN_ROWS)
    kern = functools.partial(_attn_kernel, ctx_len=ctx_len, tk=ATTN_TK, rs=ATTN_ROWS, n_ctx_tiles=n_ctx_tiles)
    return pl.pallas_call(
        kern,
        out_shape=jax.ShapeDtypeStruct((t_all, d), bf16),
        grid=(n_batch, N_KV_HEADS, tiles_per_batch),
        in_specs=[
            pl.BlockSpec((ROW_TILE, gw), lambda b, g, i: (b * tiles_per_batch + i, g)),
            pl.BlockSpec((rows_per_batch, HEAD_DIM), lambda b, g, i: (b, g)),
            pl.BlockSpec((rows_per_batch, 2 * HEAD_DIM), lambda b, g, i: (b, g)),
        ],
        out_specs=pl.BlockSpec((ROW_TILE, gw), lambda b, g, i: (b * tiles_per_batch + i, g)),
        scratch_shapes=[
            pltpu.VMEM((n_sub, ATTN_ROWS, LANES), f32),
            pltpu.VMEM((n_sub, ATTN_ROWS, 2 * HEAD_DIM), f32),
        ],
        compiler_params=_cparams("arbitrary", "arbitrary", "arbitrary"),
        name="gqa_attention",
    )(q, k, v)


def _post_epilogue(x, y, mod_ref, lng_ref, lnb_ref, wr_ref, rb_ref, x1_ref, h_ref, e_ref, g_ref):
    g1 = mod_ref[0, 2:3, :]
    sh2 = mod_ref[0, 3:4, :]
    sc2 = mod_ref[0, 4:5, :]
    z = DEEPNORM_ALPHA * x + g1 * y
    mu = jnp.mean(z, axis=-1, keepdims=True)
    zc = z - mu
    var = jnp.mean(zc * zc, axis=-1, keepdims=True)
    x1 = zc * lax.rsqrt(var + LN_EPS) * lng_ref[...] + lnb_ref[...]
    h = x1 * (1.0 + sc2) + sh2
    x1_ref[...] = x1
    h_ref[...] = h

    logits = jnp.dot(h, wr_ref[...], preferred_element_type=f32, precision=lax.Precision.HIGHEST)
    lt = logits.T
    aff = jax.nn.sigmoid(lt)
    sel = aff + rb_ref[...]
    ng = N_EXPERT_GROUPS
    a = [aff[j * ng:(j + 1) * ng] for j in range(EXPERTS_PER_GROUP)]
    s = [sel[j * ng:(j + 1) * ng] for j in range(EXPERTS_PER_GROUP)]
    gs = None
    for i in range(EXPERTS_PER_GROUP):
        for j in range(i + 1, EXPERTS_PER_GROUP):
            pij = s[i] + s[j]
            gs = pij if gs is None else jnp.maximum(gs, pij)
    gio = lax.broadcasted_iota(jnp.int32, gs.shape, 0)
    gmax = jnp.max(gs, axis=0, keepdims=True)
    g_idx = jnp.min(jnp.where(gs == gmax, gio, ng), axis=0, keepdims=True)
    onehot = gio == g_idx
    sin_g = [jnp.sum(jnp.where(onehot, s[j], 0.0), axis=0, keepdims=True) for j in range(EXPERTS_PER_GROUP)]
    aff_g = [jnp.sum(jnp.where(onehot, a[j], 0.0), axis=0, keepdims=True) for j in range(EXPERTS_PER_GROUP)]
    loc = [jnp.zeros_like(g_idx), jnp.zeros_like(g_idx)]
    gat = [jnp.zeros_like(gmax), jnp.zeros_like(gmax)]
    for j in range(EXPERTS_PER_GROUP):
        rank = jnp.zeros_like(g_idx)
        for i in range(EXPERTS_PER_GROUP):
            if i == j:
                continue
            ahead = (sin_g[i] >= sin_g[j]) if i < j else (sin_g[i] > sin_g[j])
            rank = rank + ahead.astype(jnp.int32)
        for kk in range(2):
            hit = rank == kk
            loc[kk] = jnp.where(hit, j, loc[kk])
            gat[kk] = jnp.where(hit, aff_g[j], gat[kk])
    den = gat[0] + gat[1]
    e_ref[...] = jnp.concatenate(
        [g_idx * EXPERTS_PER_GROUP + loc[0], g_idx * EXPERTS_PER_GROUP + loc[1]], axis=0)
    g_ref[...] = jnp.concatenate([gat[0] / den, gat[1] / den], axis=0)


def _attn_post_kernel(x_ref, o_ref, wo_ref, mod_ref, lng_ref, lnb_ref, wr_ref, rb_ref,
                      x1_ref, h_ref, e_ref, g_ref):
    y = jnp.dot(o_ref[...], wo_ref[...], preferred_element_type=f32)
    _post_epilogue(x_ref[...], y, mod_ref, lng_ref, lnb_ref, wr_ref, rb_ref, x1_ref, h_ref, e_ref, g_ref)


def _post_out(n_rows, d):
    shapes = (jax.ShapeDtypeStruct((n_rows, d), f32),
              jax.ShapeDtypeStruct((n_rows, d), f32),
              jax.ShapeDtypeStruct((2, n_rows), jnp.int32),
              jax.ShapeDtypeStruct((2, n_rows), f32))
    specs = (pl.BlockSpec((ROW_TILE, d), lambda i: (i, 0)),
             pl.BlockSpec((ROW_TILE, d), lambda i: (i, 0)),
             pl.BlockSpec((2, ROW_TILE), lambda i: (0, i)),
             pl.BlockSpec((2, ROW_TILE), lambda i: (0, i)))
    return shapes, specs


def _attn_post(x_all, o, w_o_bf, mod_l, ln_g, ln_b, wr_p, rb_p, seg_of_tile):
    t_all, d = x_all.shape
    shapes, specs = _post_out(t_all, d)
    return pl.pallas_call(
        _attn_post_kernel,
        out_shape=shapes,
        grid=(t_all // ROW_TILE,),
        in_specs=[
            pl.BlockSpec((ROW_TILE, d), lambda i: (i, 0)),
            pl.BlockSpec((ROW_TILE, d), lambda i: (i, 0)),
            pl.BlockSpec((d, d), lambda i: (0, 0)),
            pl.BlockSpec((1, 6, d), lambda i: (seg_of_tile(i), 0, 0)),
            pl.BlockSpec((1, d), lambda i: (0, 0)),
            pl.BlockSpec((1, d), lambda i: (0, 0)),
            pl.BlockSpec((d, N_EXPERTS), lambda i: (0, 0)),
            pl.BlockSpec((N_EXPERTS, 1), lambda i: (0, 0)),
        ],
        out_specs=specs,
        compiler_params=_cparams("arbitrary"),
        name="attn_out_norm_route",
    )(x_all, o, w_o_bf, mod_l, ln_g, ln_b, wr_p, rb_p)


def _pool_post_kernel(x_ref, xp_ref, xn_ref, wp_ref, ps_ref, mod_ref, lng_ref, lnb_ref, wr_ref, rb_ref,
                      x1_ref, h_ref, e_ref, g_ref, *, seq_len, tiles_per_seq):
    tm, d = x_ref.shape
    gd = d // len(POOL_WINDOWS)
    r = pl.program_id(0) % tiles_per_seq
    sh1 = mod_ref[0, 0:1, :]
    sc1 = mod_ref[0, 1:2, :]
    x = x_ref[...]
    ext = jnp.concatenate([xp_ref[...], x, xn_ref[...]], axis=0)
    n_ext = tm + 2 * POOL_HALO
    pos = r * tm - POOL_HALO + lax.broadcasted_iota(jnp.int32, (n_ext, 1), 0)
    inside = (pos >= 0) & (pos < seq_len)
    u_ext = jnp.where(inside, ext * (1.0 + sc1) + sh1, 0.0)
    t = pos[POOL_HALO:POOL_HALO + tm]
    ys = []
    for g, w in enumerate(POOL_WINDOWS):
        e = u_ext[:, g * gd:(g + 1) * gd]
        p = e + pltpu.roll(e, 1, 0)
        step = 1
        while 2 * step < w:
            p = pltpu.roll(p, step, 0) + pltpu.roll(p, n_ext - step, 0)
            step *= 2
        cnt = (jnp.clip(t + w // 2, 0, seq_len) - jnp.clip(t - w // 2, 0, seq_len)).astype(f32)
        mean = p[POOL_HALO:POOL_HALO + tm] / cnt
        dlt = (mean - e[POOL_HALO:POOL_HALO + tm]).astype(bf16)
        ys.append(jnp.dot(dlt, wp_ref[g], preferred_element_type=f32))
    y = jnp.concatenate(ys, axis=-1) * ps_ref[...]
    _post_epilogue(x, y, mod_ref, lng_ref, lnb_ref, wr_ref, rb_ref, x1_ref, h_ref, e_ref, g_ref)


def _pool_post(x_all, w_pool_bf, pool_scale, mod_l, ln_g, ln_b, wr_p, rb_p,
               n_batch, seq_len, tiles_per_batch, n_ctx_tiles):
    t_all, d = x_all.shape
    tiles_per_seq = seq_len // ROW_TILE
    n_rows = n_batch * seq_len
    halo_per_tile = ROW_TILE // POOL_HALO
    last_halo = t_all // POOL_HALO - 1
    ng = len(POOL_WINDOWS)

    def src_tile(i):
        return (i // tiles_per_seq) * tiles_per_batch + n_ctx_tiles + i % tiles_per_seq

    shapes, specs = _post_out(n_rows, d)
    kern = functools.partial(_pool_post_kernel, seq_len=seq_len, tiles_per_seq=tiles_per_seq)
    return pl.pallas_call(
        kern,
        out_shape=shapes,
        grid=(n_rows // ROW_TILE,),
        in_specs=[
            pl.BlockSpec((ROW_TILE, d), lambda i: (src_tile(i), 0)),
            pl.BlockSpec((POOL_HALO, d), lambda i: (jnp.maximum(src_tile(i) * halo_per_tile - 1, 0), 0)),
            pl.BlockSpec((POOL_HALO, d), lambda i: (jnp.minimum((src_tile(i) + 1) * halo_per_tile, last_halo), 0)),
            pl.BlockSpec((ng, d // ng, d // ng), lambda i: (0, 0, 0)),
            pl.BlockSpec((1, d), lambda i: (0, 0)),
            pl.BlockSpec((1, 6, d), lambda i: (1 + i // tiles_per_seq, 0, 0)),
            pl.BlockSpec((1, d), lambda i: (0, 0)),
            pl.BlockSpec((1, d), lambda i: (0, 0)),
            pl.BlockSpec((d, N_EXPERTS), lambda i: (0, 0)),
            pl.BlockSpec((N_EXPERTS, 1), lambda i: (0, 0)),
        ],
        out_specs=specs,
        compiler_params=_cparams("arbitrary"),
        name="pool_norm_route",
    )(x_all, x_all, x_all, w_pool_bf, pool_scale, mod_l, ln_g, ln_b, wr_p, rb_p)


def _ffn_kernel(be_ref, nu_ref, tok_cur, tok_nxt, gate_ref, h_hbm, wg_ref, wu_ref, wd_ref, y_ref,
                xbuf, sem):
    b = pl.program_id(0)
    n_blocks = pl.num_programs(0)
    nu = nu_ref[0]
    slot = b % 2

    def start_gather(tok_ref, dst_slot):
        for r in range(MOE_BLOCK):
            pltpu.make_async_copy(
                h_hbm.at[pl.ds(tok_ref[0, 0, r], 1), :],
                xbuf.at[dst_slot, pl.ds(r, 1), :],
                sem.at[dst_slot]).start()

    def wait_gather(dst_slot):
        pltpu.make_async_copy(h_hbm.at[pl.ds(0, MOE_BLOCK), :], xbuf.at[dst_slot], sem.at[dst_slot]).wait()

    @pl.when(b == 0)
    def _():
        start_gather(tok_cur, 0)

    @pl.when(b < nu)
    def _():
        start_gather(tok_nxt, 1 - slot)
        wait_gather(slot)
        xb = xbuf[slot].astype(bf16)
        a = jnp.dot(xb, wg_ref[0], preferred_element_type=f32)
        u = jnp.dot(xb, wu_ref[0], preferred_element_type=f32)
        z = (a * jax.nn.sigmoid(a) * u).astype(bf16)
        y = jnp.dot(z, wd_ref[0], preferred_element_type=f32)
        y_ref[...] = y * gate_ref[...]

    @pl.when(b >= nu)
    def _():
        y_ref[...] = jnp.zeros(y_ref.shape, f32)

    @pl.when((b == nu) | ((b == n_blocks - 1) & (nu == n_blocks)))
    def _():
        wait_gather(jnp.where(b == nu, slot, 1 - slot))


def _expert_ffn(h, blk_expert, n_used, buf_tok, buf_gate, wg, wu, wd):
    t, d = h.shape
    de = wg.shape[2]
    n_blocks = blk_expert.shape[0]
    slots = n_blocks * MOE_BLOCK
    tok3 = buf_tok.reshape(n_blocks, 1, MOE_BLOCK)
    grid_spec = pltpu.PrefetchScalarGridSpec(
        num_scalar_prefetch=2,
        grid=(n_blocks,),
        in_specs=[
            pl.BlockSpec((1, 1, MOE_BLOCK), lambda b, be, nu: (b, 0, 0), memory_space=pltpu.SMEM),
            pl.BlockSpec((1, 1, MOE_BLOCK), lambda b, be, nu: (jnp.minimum(b + 1, n_blocks - 1), 0, 0),
                         memory_space=pltpu.SMEM),
            pl.BlockSpec((MOE_BLOCK, 1), lambda b, be, nu: (b, 0)),
            pl.BlockSpec(memory_space=pl.ANY),
            pl.BlockSpec((1, d, de), lambda b, be, nu: (be[b], 0, 0)),
            pl.BlockSpec((1, d, de), lambda b, be, nu: (be[b], 0, 0)),
            pl.BlockSpec((1, de, d), lambda b, be, nu: (be[b], 0, 0)),
        ],
        out_specs=pl.BlockSpec((MOE_BLOCK, d), lambda b, be, nu: (b, 0)),
        scratch_shapes=[pltpu.VMEM((2, MOE_BLOCK, d), f32), pltpu.SemaphoreType.DMA((2,))],
    )
    return pl.pallas_call(
        _ffn_kernel,
        out_shape=jax.ShapeDtypeStruct((slots, d), f32),
        grid_spec=grid_spec,
        compiler_params=_cparams("arbitrary"),
        name="expert_ffn",
    )(blk_expert, n_used, tok3, tok3, buf_gate.reshape(slots, 1), h, wg, wu, wd)


def _fin_kernel(pos_cur, pos_nxt, x1_ref, mod_ref, lng_ref, lnb_ref, y_hbm, o_ref, ybuf, sem):
    i = pl.program_id(0)
    n = pl.num_programs(0)
    tm = x1_ref.shape[0]
    slot = i % 2

    def start_gather(pos_ref, dst_slot):
        for kk in range(2):
            for r in range(tm):
                pltpu.make_async_copy(
                    y_hbm.at[pl.ds(pos_ref[0, kk, r], 1), :],
                    ybuf.at[dst_slot, kk, pl.ds(r, 1), :],
                    sem.at[dst_slot]).start()

    def wait_gather(dst_slot):
        for kk in range(2):
            pltpu.make_async_copy(y_hbm.at[pl.ds(0, tm), :], ybuf.at[dst_slot, kk], sem.at[dst_slot]).wait()

    @pl.when(i == 0)
    def _():
        start_gather(pos_cur, 0)

    @pl.when(i + 1 < n)
    def _():
        start_gather(pos_nxt, 1 - slot)

    wait_gather(slot)
    g2 = mod_ref[0, 5:6, :]
    f = ybuf[slot, 0] + ybuf[slot, 1]
    z = DEEPNORM_ALPHA * x1_ref[...] + g2 * f
    mu = jnp.mean(z, axis=-1, keepdims=True)
    zc = z - mu
    var = jnp.mean(zc * zc, axis=-1, keepdims=True)
    o_ref[...] = zc * lax.rsqrt(var + LN_EPS) * lng_ref[...] + lnb_ref[...]


def _combine_norm(x1, pos, y_sorted, mod_l, ln_g, ln_b, seg_of_tile):
    t, d = x1.shape
    n_tiles = t // ROW_TILE
    pos3 = pos.reshape(2, n_tiles, ROW_TILE).transpose(1, 0, 2)
    return pl.pallas_call(
        _fin_kernel,
        out_shape=jax.ShapeDtypeStruct((t, d), f32),
        grid=(n_tiles,),
        in_specs=[
            pl.BlockSpec((1, 2, ROW_TILE), lambda i: (i, 0, 0), memory_space=pltpu.SMEM),
            pl.BlockSpec((1, 2, ROW_TILE), lambda i: (jnp.minimum(i + 1, n_tiles - 1), 0, 0),
                         memory_space=pltpu.SMEM),
            pl.BlockSpec((ROW_TILE, d), lambda i: (i, 0)),
            pl.BlockSpec((1, 6, d), lambda i: (seg_of_tile(i), 0, 0)),
            pl.BlockSpec((1, d), lambda i: (0, 0)),
            pl.BlockSpec((1, d), lambda i: (0, 0)),
            pl.BlockSpec(memory_space=pl.ANY),
        ],
        out_specs=pl.BlockSpec((ROW_TILE, d), lambda i: (i, 0)),
        scratch_shapes=[pltpu.VMEM((2, 2, ROW_TILE, d), f32), pltpu.SemaphoreType.DMA((2,))],
        compiler_params=_cparams("arbitrary"),
        name="combine_norm",
    )(pos3, pos3, x1, mod_l, ln_g, ln_b, y_sorted)


def _dispatch_plan(e_idx, gate):
    t = e_idx.shape[1]
    n_assign = 2 * t
    e_flat = e_idx.reshape(-1)
    tok_flat = jnp.tile(jnp.arange(t, dtype=jnp.int32), 2)
    onehot = (e_flat[:, None] == jnp.arange(N_EXPERTS, dtype=jnp.int32)[None, :]).astype(jnp.int32)
    csum = jnp.cumsum(onehot, axis=0)
    rank = jnp.sum((csum - onehot) * onehot, axis=1)
    counts = csum[-1]
    padded = (counts + MOE_BLOCK - 1) // MOE_BLOCK * MOE_BLOCK
    p_ends = jnp.cumsum(padded)
    p_starts = p_ends - padded
    dest = (p_starts[e_flat] + rank).astype(jnp.int32)
    n_blocks = (n_assign + N_EXPERTS * (MOE_BLOCK - 1) + MOE_BLOCK - 1) // MOE_BLOCK
    slots = n_blocks * MOE_BLOCK
    buf_tok = jnp.zeros((slots,), jnp.int32).at[dest].set(tok_flat)
    buf_gate = jnp.zeros((slots,), f32).at[dest].set(gate.reshape(-1))
    blk_start = jnp.arange(n_blocks, dtype=jnp.int32) * MOE_BLOCK
    blk_expert = jnp.minimum(jnp.searchsorted(p_ends, blk_start, side='right'), N_EXPERTS - 1).astype(jnp.int32)
    n_used = (p_ends[-1] // MOE_BLOCK).astype(jnp.int32).reshape(1)
    return blk_expert, n_used, buf_tok, buf_gate, dest.reshape(2, t)


def _moe(h, e_idx, gate, wg, wu, wd):
    blk_expert, n_used, buf_tok, buf_gate, pos = _dispatch_plan(e_idx, gate)
    y_sorted = _expert_ffn(h, blk_expert, n_used, buf_tok, buf_gate, wg, wu, wd)
    return y_sorted, pos


def _rope_tables(ctx_len, seq_len):
    t = jnp.arange(seq_len)
    row = (t // GRID_W).astype(f32)
    col = (t % GRID_W).astype(f32)
    inv_freq = ROPE_THETA ** (-jnp.arange(ROPE_PAIRS, dtype=f32) / ROPE_PAIRS)
    ang_r = row[:, None] * inv_freq
    ang_c = col[:, None] * inv_freq
    ang = jnp.concatenate([ang_r, ang_r, ang_c, ang_c], axis=-1)
    cos = jnp.cos(ang)
    sin = jnp.sin(ang)
    lane = jnp.arange(HEAD_DIM)
    sign = jnp.where((lane % (2 * ROPE_PAIRS)) < ROPE_PAIRS, -1.0, 1.0).astype(f32)
    cos_t = jnp.concatenate([jnp.ones((ctx_len, HEAD_DIM), f32), cos], axis=0)
    sin_t = jnp.concatenate([jnp.zeros((ctx_len, HEAD_DIM), f32), sin * sign], axis=0)
    return cos_t, sin_t


def kernel(x, c, ctx, c_ctx, w_mod, b_mod, ln_g, ln_b, w_qkv, q_gain, k_gain, w_o, w_pool, pool_scale,
           w_router, router_bias, w_gate, w_up, w_down):
    n_batch, seq_len, d = x.shape
    ctx_len = ctx.shape[1]
    assert w_mod.shape[0] == DEPTH and d == N_HEADS * HEAD_DIM
    assert ctx_len % ROW_TILE == 0 and seq_len % ROW_TILE == 0 and 1 + n_batch <= MOD_ROWS
    rows_per_batch = ctx_len + seq_len
    tiles_per_batch = rows_per_batch // ROW_TILE
    n_ctx_tiles = ctx_len // ROW_TILE

    def seg_of_tile(i):
        return jnp.where(i % tiles_per_batch < n_ctx_tiles, 0, 1 + i // tiles_per_batch)

    cond = jnp.concatenate([c_ctx[None, :], c], axis=0)
    mod = _modulation(cond, w_mod, b_mod).reshape(DEPTH, MOD_ROWS, 6, d)

    wr_p = w_router.reshape(d, N_EXPERT_GROUPS, EXPERTS_PER_GROUP).transpose(0, 2, 1).reshape(d, N_EXPERTS)
    rb_p = router_bias.reshape(N_EXPERT_GROUPS, EXPERTS_PER_GROUP).T.reshape(N_EXPERTS, 1)
    cos_t, sin_t = _rope_tables(ctx_len, seq_len)

    x_all = jnp.concatenate([ctx, x], axis=1).reshape(n_batch * rows_per_batch, d)
    q, k, v = _qkv_proj(x_all, mod[0], w_qkv[0].astype(bf16), q_gain[0:1], k_gain[0:1], cos_t, sin_t,
                        tiles_per_batch, seg_of_tile)
    o = _attention(q, k, v, n_batch, tiles_per_batch, n_ctx_tiles, ctx_len)
    x1, h, e_idx, gate = _attn_post(x_all, o, w_o[0].astype(bf16), mod[0], ln_g[0, 0:1], ln_b[0, 0:1],
                                    wr_p, rb_p, seg_of_tile)
    y_sorted, pos = _moe(h, e_idx, gate, w_gate[0].astype(bf16), w_up[0].astype(bf16), w_down[0].astype(bf16))
    x_all = _combine_norm(x1, pos, y_sorted, mod[0], ln_g[0, 1:2], ln_b[0, 1:2], seg_of_tile)

    tiles_per_seq = seq_len // ROW_TILE
    x1, h, e_idx, gate = _pool_post(x_all, w_pool[0].astype(bf16), pool_scale[0:1], mod[1],
                                    ln_g[1, 0:1], ln_b[1, 0:1], wr_p, rb_p,
                                    n_batch, seq_len, tiles_per_batch, n_ctx_tiles)
    y_sorted, pos = _moe(h, e_idx, gate, w_gate[1].astype(bf16), w_up[1].astype(bf16), w_down[1].astype(bf16))
    out = _combine_norm(x1, pos, y_sorted, mod[1], ln_g[1, 1:2], ln_b[1, 1:2],
                        lambda i: 1 + i // tiles_per_seq)
    return out.reshape(n_batch, seq_len, d)
```

```python
import functools

import jax
import jax.numpy as jnp
from jax import lax
from jax.experimental import pallas as pl
from jax.experimental.pallas import tpu as pltpu

N_HEADS = 16
N_KV_HEADS = 4
HEAD_DIM = 128
Q_PER_KV = N_HEADS // N_KV_HEADS
GRID_W = 64
ROPE_THETA = 10000.0
ROPE_PAIRS = HEAD_DIM // 4
ATTN_SCALE = HEAD_DIM ** -0.5
Q_SCALE = ATTN_SCALE * 1.4426950408889634
POOL_WINDOWS = (2, 4, 8, 16)
POOL_HALO = 8
N_EXPERTS = 32
N_EXPERT_GROUPS = 8
EXPERTS_PER_GROUP = N_EXPERTS // N_EXPERT_GROUPS
DEPTH = 2
DEEPNORM_ALPHA = (2 * DEPTH) ** 0.25
LN_EPS = 1e-6
RMS_EPS = 1e-6

LANES = 128
SUBLANES = 8
ROW_TILE = 256
ATTN_TK = 2048
ATTN_ROWS = 128
ATTN_UNROLL = 4
MOE_BLOCK = 256
GATE_COLS = 8
MOD_ROWS = 8
MOD_TN = 512
VMEM_LIMIT = 56 * 1024 * 1024

f32 = jnp.float32
bf16 = jnp.bfloat16
u32 = jnp.uint32
i32 = jnp.int32


def _cparams(*sem):
    return pltpu.CompilerParams(dimension_semantics=sem, vmem_limit_bytes=VMEM_LIMIT)


def _nt_dot(a, b):
    return lax.dot_general(a, b, (((1,), (1,)), ((), ())), preferred_element_type=f32)


def _mod_kernel(c_ref, w_ref, b_ref, o_ref, *, n_rows):
    d = w_ref.shape[1]
    tn = w_ref.shape[2]
    nj = tn // LANES

    def body(k, accs):
        r0 = pl.multiple_of(k * SUBLANES, SUBLANES)
        w8 = w_ref[0, pl.ds(r0, SUBLANES), :]
        out = []
        for r in range(n_rows):
            c8 = c_ref[r, pl.ds(r0, SUBLANES), :]
            s8 = c8 * jax.nn.sigmoid(c8)
            for j in range(nj):
                out.append(accs[r * nj + j] + s8 * w8[:, j * LANES:(j + 1) * LANES])
        return tuple(out)

    init = tuple(jnp.zeros((SUBLANES, LANES), f32) for _ in range(n_rows * nj))
    accs = lax.fori_loop(0, d // SUBLANES, body, init, unroll=2)
    rows = []
    for r in range(n_rows):
        row = jnp.concatenate(
            [jnp.sum(accs[r * nj + j], axis=0, keepdims=True) for j in range(nj)], axis=1)
        rows.append(row + b_ref[0])
    rows.append(jnp.zeros((MOD_ROWS - n_rows, tn), f32))
    o_ref[0] = jnp.concatenate(rows, axis=0)


def _modulation(cond, w_mod, b_mod):
    n_rows, d = cond.shape
    depth, _, n6 = w_mod.shape
    c_b = jnp.broadcast_to(cond[:, :, None], (n_rows, d, LANES))
    return pl.pallas_call(
        functools.partial(_mod_kernel, n_rows=n_rows),
        out_shape=jax.ShapeDtypeStruct((depth, MOD_ROWS, n6), f32),
        grid=(depth, n6 // MOD_TN),
        in_specs=[
            pl.BlockSpec((n_rows, d, LANES), lambda l, j: (0, 0, 0)),
            pl.BlockSpec((1, d, MOD_TN), lambda l, j: (l, 0, j)),
            pl.BlockSpec((1, 1, MOD_TN), lambda l, j: (l, 0, j)),
        ],
        out_specs=pl.BlockSpec((1, MOD_ROWS, MOD_TN), lambda l, j: (l, 0, j)),
        compiler_params=_cparams("arbitrary", "arbitrary"),
        name="adaln_mod",
    )(c_b, w_mod, b_mod.reshape(depth, 1, n6))


class _Rows:
    def __init__(self, n_batch, ctx_len, seq_len):
        self.n_batch = n_batch
        self.ctx_len = ctx_len
        self.seq_len = seq_len
        self.n_ctx_tiles = ctx_len // ROW_TILE
        self.n_seq_tiles = seq_len // ROW_TILE
        self.tiles_per_batch = self.n_ctx_tiles + self.n_seq_tiles
        self.n_tiles = n_batch * self.tiles_per_batch
        self.t_all = self.n_tiles * ROW_TILE

    def is_ctx(self, i):
        return i % self.tiles_per_batch < self.n_ctx_tiles

    def seg(self, i):
        return jnp.where(self.is_ctx(i), 0, 1 + i // self.tiles_per_batch)

    def ctx_tile(self, i):
        r = i % self.tiles_per_batch
        return (i // self.tiles_per_batch) * self.n_ctx_tiles + jnp.minimum(r, self.n_ctx_tiles - 1)

    def lat_tile(self, i):
        r = i % self.tiles_per_batch
        return (i // self.tiles_per_batch) * self.n_seq_tiles + jnp.maximum(r - self.n_ctx_tiles, 0)


def _qkv_kernel(xc_ref, xl_ref, mod_ref, w_ref, qg_ref, kg_ref, cos_ref, sin_ref, q_ref, k_ref, v_ref, *, rows):
    tm, d = xl_ref.shape
    kv_dim = k_ref.shape[1]
    x = jnp.where(rows.is_ctx(pl.program_id(0)), xc_ref[...], xl_ref[...])
    sh1 = mod_ref[0, 0:1, :]
    sc1 = mod_ref[0, 1:2, :]
    u = (x * (1.0 + sc1) + sh1).astype(bf16)
    qkv = jnp.dot(u, w_ref[...], preferred_element_type=f32)
    cos = cos_ref[...]
    sin = sin_ref[...]
    lane = lax.broadcasted_iota(i32, (tm, HEAD_DIM), 1)
    first = (lane % (2 * ROPE_PAIRS)) < ROPE_PAIRS

    def norm_rope(xh, gain, scale):
        ms = jnp.mean(xh * xh, axis=-1, keepdims=True)
        xn = xh * lax.rsqrt(ms + RMS_EPS) * gain
        rot = jnp.where(first, pltpu.roll(xn, HEAD_DIM - ROPE_PAIRS, 1), pltpu.roll(xn, ROPE_PAIRS, 1))
        y = xn * cos + rot * sin
        return y * scale if scale != 1.0 else y

    qg = qg_ref[...]
    kg = kg_ref[...]
    for h in range(N_HEADS):
        sl = slice(h * HEAD_DIM, (h + 1) * HEAD_DIM)
        q_ref[:, sl] = norm_rope(qkv[:, sl], qg, Q_SCALE).astype(bf16)
    ones = jnp.ones((tm, HEAD_DIM), bf16)
    for h in range(N_KV_HEADS):
        sl = slice(h * HEAD_DIM, (h + 1) * HEAD_DIM)
        k_ref[:, sl] = norm_rope(qkv[:, d + h * HEAD_DIM:d + (h + 1) * HEAD_DIM], kg, 1.0).astype(bf16)
        v0 = d + kv_dim + h * HEAD_DIM
        v_ref[:, 2 * h * HEAD_DIM:(2 * h + 1) * HEAD_DIM] = qkv[:, v0:v0 + HEAD_DIM].astype(bf16)
        v_ref[:, (2 * h + 1) * HEAD_DIM:(2 * h + 2) * HEAD_DIM] = ones


def _qkv_proj(ctx2, x2, mod_l, w_qkv_bf, q_gain, k_gain, cos_t, sin_t, rows):
    d = x2.shape[1]
    kv_dim = N_KV_HEADS * HEAD_DIM
    tpb = rows.tiles_per_batch
    return pl.pallas_call(
        functools.partial(_qkv_kernel, rows=rows),
        out_shape=(jax.ShapeDtypeStruct((rows.t_all, d), bf16),
                   jax.ShapeDtypeStruct((rows.t_all, kv_dim), bf16),
                   jax.ShapeDtypeStruct((rows.t_all, 2 * kv_dim), bf16)),
        grid=(rows.n_tiles,),
        in_specs=[
            pl.BlockSpec((ROW_TILE, d), lambda i: (rows.ctx_tile(i), 0)),
            pl.BlockSpec((ROW_TILE, d), lambda i: (rows.lat_tile(i), 0)),
            pl.BlockSpec((1, 6, d), lambda i: (rows.seg(i), 0, 0)),
            pl.BlockSpec((d, d + 2 * kv_dim), lambda i: (0, 0)),
            pl.BlockSpec((1, HEAD_DIM), lambda i: (0, 0)),
            pl.BlockSpec((1, HEAD_DIM), lambda i: (0, 0)),
            pl.BlockSpec((ROW_TILE, HEAD_DIM), lambda i: (i % tpb, 0)),
            pl.BlockSpec((ROW_TILE, HEAD_DIM), lambda i: (i % tpb, 0)),
        ],
        out_specs=(pl.BlockSpec((ROW_TILE, d), lambda i: (i, 0)),
                   pl.BlockSpec((ROW_TILE, kv_dim), lambda i: (i, 0)),
                   pl.BlockSpec((ROW_TILE, 2 * kv_dim), lambda i: (i, 0))),
        compiler_params=_cparams("arbitrary"),
        name="qkv_rope",
    )(ctx2, x2, mod_l, w_qkv_bf, q_gain, k_gain, cos_t, sin_t)


def _attn_kernel(q_ref, k_ref, v_ref, wg_ref, wu_ref, wd_ref, o_ref, wgo_ref, wuo_ref, wdo_ref, m_sc, acc_sc,
                 *, ctx_len, tk, rs, n_ctx_tiles):
    tq = q_ref.shape[0]
    n_lat_chunks = (k_ref.shape[0] - ctx_len) // tk
    qi = pl.program_id(2)
    subs = [(h, r0) for h in range(Q_PER_KV) for r0 in range(0, tq, rs)]

    wgo_ref[...] = wg_ref[...].astype(bf16)
    wuo_ref[...] = wu_ref[...].astype(bf16)
    wdo_ref[...] = wd_ref[...].astype(bf16)

    def step(off, size, first):
        kc = k_ref[pl.ds(off, size), :]
        vc = v_ref[pl.ds(off, size), :]
        for i, (h, r0) in enumerate(subs):
            qh = q_ref[r0:r0 + rs, h * HEAD_DIM:(h + 1) * HEAD_DIM]
            s = _nt_dot(qh, kc)
            mx = s[:, 0:LANES]
            for j in range(1, size // LANES):
                mx = jnp.maximum(mx, s[:, j * LANES:(j + 1) * LANES])
            m_cur = jnp.max(mx, axis=-1, keepdims=True)
            if first:
                m_new = jnp.broadcast_to(m_cur, (rs, LANES))
            else:
                m_prev = m_sc[i]
                m_new = jnp.maximum(m_prev, m_cur)
            p = jnp.concatenate(
                [jnp.exp2(s[:, j * LANES:(j + 1) * LANES] - m_new) for j in range(size // LANES)],
                axis=1).astype(bf16)
            pv = jnp.dot(p, vc, preferred_element_type=f32)
            if first:
                acc_sc[i] = pv
            else:
                alpha = jnp.exp2(m_prev - m_new)
                acc_sc[i] = acc_sc[i] * jnp.concatenate([alpha, alpha], axis=1) + pv
            m_sc[i] = m_new

    step(0, ctx_len, True)

    @pl.when(qi >= n_ctx_tiles)
    def _():
        def body(c, carry):
            step(pl.multiple_of(ctx_len + c * tk, LANES), tk, False)
            return carry
        lax.fori_loop(0, n_lat_chunks, body, 0, unroll=min(ATTN_UNROLL, n_lat_chunks))

    for i, (h, r0) in enumerate(subs):
        acc = acc_sc[i]
        o_ref[r0:r0 + rs, h * HEAD_DIM:(h + 1) * HEAD_DIM] = (
            acc[:, :HEAD_DIM] / acc[:, HEAD_DIM:]).astype(bf16)


def _attention(q, k, v, w_gate, w_up, w_down, rows):
    t_all, d = q.shape
    tpb = rows.tiles_per_batch
    rows_per_batch = t_all // rows.n_batch
    gw = Q_PER_KV * HEAD_DIM
    n_sub = Q_PER_KV * (ROW_TILE // ATTN_ROWS)
    tk = min(ATTN_TK, rows.seq_len)
    assert rows.seq_len % tk == 0

    n_steps = rows.n_batch * N_KV_HEADS * tpb
    wg2 = w_gate.reshape(-1, w_gate.shape[-1])
    wu2 = w_up.reshape(-1, w_up.shape[-1])
    wd2 = w_down.reshape(-1, w_down.shape[-1])
    n_slabs = 1 << (n_steps.bit_length() - 1)
    slab_g = wg2.shape[0] // n_slabs
    slab_d = wd2.shape[0] // n_slabs
    assert wg2.shape[0] == n_slabs * slab_g and wd2.shape[0] == n_slabs * slab_d and wu2.shape == wg2.shape
    assert slab_g % SUBLANES == 0 and slab_d % SUBLANES == 0

    def slab(b, g, i):
        return (jnp.minimum((b * N_KV_HEADS + g) * tpb + i, n_slabs - 1), 0)

    kern = functools.partial(_attn_kernel, ctx_len=rows.ctx_len, tk=tk, rs=ATTN_ROWS,
                             n_ctx_tiles=rows.n_ctx_tiles)
    o, wg_bf, wu_bf, wd_bf = pl.pallas_call(
        kern,
        out_shape=(jax.ShapeDtypeStruct((t_all, d), bf16),
                   jax.ShapeDtypeStruct(wg2.shape, bf16),
                   jax.ShapeDtypeStruct(wu2.shape, bf16),
                   jax.ShapeDtypeStruct(wd2.shape, bf16)),
        grid=(rows.n_batch, N_KV_HEADS, tpb),
        in_specs=[
            pl.BlockSpec((ROW_TILE, gw), lambda b, g, i: (b * tpb + i, g)),
            pl.BlockSpec((rows_per_batch, HEAD_DIM), lambda b, g, i: (b, g)),
            pl.BlockSpec((rows_per_batch, 2 * HEAD_DIM), lambda b, g, i: (b, g)),
            pl.BlockSpec((slab_g, wg2.shape[1]), slab),
            pl.BlockSpec((slab_g, wu2.shape[1]), slab),
            pl.BlockSpec((slab_d, wd2.shape[1]), slab),
        ],
        out_specs=(pl.BlockSpec((ROW_TILE, gw), lambda b, g, i: (b * tpb + i, g)),
                   pl.BlockSpec((slab_g, wg2.shape[1]), slab),
                   pl.BlockSpec((slab_g, wu2.shape[1]), slab),
                   pl.BlockSpec((slab_d, wd2.shape[1]), slab)),
        scratch_shapes=[
            pltpu.VMEM((n_sub, ATTN_ROWS, LANES), f32),
            pltpu.VMEM((n_sub, ATTN_ROWS, 2 * HEAD_DIM), f32),
        ],
        compiler_params=_cparams("arbitrary", "arbitrary", "arbitrary"),
        name="gqa_attention",
    )(q, k, v, wg2, wu2, wd2)
    return o, wg_bf.reshape(w_gate.shape), wu_bf.reshape(w_up.shape), wd_bf.reshape(w_down.shape)


def _post_epilogue(x, y, mod_ref, lng_ref, lnb_ref, wr_ref, rb_ref, x1_ref, hp_ref, e_ref, g_ref):
    tm, d = x.shape
    g1 = mod_ref[0, 2:3, :]
    sh2 = mod_ref[0, 3:4, :]
    sc2 = mod_ref[0, 4:5, :]
    z = DEEPNORM_ALPHA * x + g1 * y
    mu = jnp.mean(z, axis=-1, keepdims=True)
    zc = z - mu
    var = jnp.mean(zc * zc, axis=-1, keepdims=True)
    x1 = zc * lax.rsqrt(var + LN_EPS) * lng_ref[...] + lnb_ref[...]
    h = x1 * (1.0 + sc2) + sh2
    x1_ref[...] = x1

    h_hi = h.astype(bf16)
    h_hi32 = h_hi.astype(f32)
    bits = lax.bitcast_convert_type(h_hi32, u32)
    hp_ref[...] = (bits[:, :d // 2] >> 16) | bits[:, d // 2:]

    h_lo = (h - h_hi32).astype(bf16)
    wr = wr_ref[...]
    a2 = _nt_dot(wr, h_hi)
    lt = a2[:N_EXPERTS] + a2[N_EXPERTS:] + _nt_dot(wr[:N_EXPERTS], h_lo)
    aff = jax.nn.sigmoid(lt)
    sel = aff + rb_ref[...]
    ng = N_EXPERT_GROUPS
    a = [aff[j * ng:(j + 1) * ng] for j in range(EXPERTS_PER_GROUP)]
    s = [sel[j * ng:(j + 1) * ng] for j in range(EXPERTS_PER_GROUP)]
    gs = None
    for i in range(EXPERTS_PER_GROUP):
        for j in range(i + 1, EXPERTS_PER_GROUP):
            pij = s[i] + s[j]
            gs = pij if gs is None else jnp.maximum(gs, pij)
    gio = lax.broadcasted_iota(i32, gs.shape, 0)
    gmax = jnp.max(gs, axis=0, keepdims=True)
    g_idx = jnp.min(jnp.where(gs == gmax, gio, ng), axis=0, keepdims=True)
    onehot = gio == g_idx
    sin_g = [jnp.sum(jnp.where(onehot, s[j], 0.0), axis=0, keepdims=True) for j in range(EXPERTS_PER_GROUP)]
    aff_g = [jnp.sum(jnp.where(onehot, a[j], 0.0), axis=0, keepdims=True) for j in range(EXPERTS_PER_GROUP)]
    loc = [jnp.zeros_like(g_idx), jnp.zeros_like(g_idx)]
    gat = [jnp.zeros_like(gmax), jnp.zeros_like(gmax)]
    for j in range(EXPERTS_PER_GROUP):
        rank = jnp.zeros_like(g_idx)
        for i in range(EXPERTS_PER_GROUP):
            if i == j:
                continue
            ahead = (sin_g[i] >= sin_g[j]) if i < j else (sin_g[i] > sin_g[j])
            rank = rank + ahead.astype(i32)
        for kk in range(2):
            hit = rank == kk
            loc[kk] = jnp.where(hit, j, loc[kk])
            gat[kk] = jnp.where(hit, aff_g[j], gat[kk])
    den = gat[0] + gat[1]
    e_ref[...] = jnp.concatenate(
        [g_idx * EXPERTS_PER_GROUP + loc[0], g_idx * EXPERTS_PER_GROUP + loc[1]], axis=0)
    g8 = jnp.concatenate([gat[0] / den, gat[1] / den, jnp.zeros((GATE_COLS - 2, tm), f32)], axis=0)
    g_ref[...] = g8.T


def _attn_post_kernel(xc_ref, xl_ref, o_ref, wo_ref, mod_ref, lng_ref, lnb_ref, wr_ref, rb_ref,
                      x1_ref, hp_ref, e_ref, g_ref, *, rows):
    x = jnp.where(rows.is_ctx(pl.program_id(0)), xc_ref[...], xl_ref[...])
    y = jnp.dot(o_ref[...], wo_ref[...], preferred_element_type=f32)
    _post_epilogue(x, y, mod_ref, lng_ref, lnb_ref, wr_ref, rb_ref, x1_ref, hp_ref, e_ref, g_ref)


def _post_out(n_rows, d):
    shapes = (jax.ShapeDtypeStruct((n_rows, d), f32),
              jax.ShapeDtypeStruct((n_rows, d // 2), u32),
              jax.ShapeDtypeStruct((2, n_rows), i32),
              jax.ShapeDtypeStruct((n_rows, GATE_COLS), f32))
    specs = (pl.BlockSpec((ROW_TILE, d), lambda i: (i, 0)),
             pl.BlockSpec((ROW_TILE, d // 2), lambda i: (i, 0)),
             pl.BlockSpec((2, ROW_TILE), lambda i: (0, i)),
             pl.BlockSpec((ROW_TILE, GATE_COLS), lambda i: (i, 0)))
    return shapes, specs


def _router_specs(d):
    return [pl.BlockSpec((2 * N_EXPERTS, d), lambda i: (0, 0)),
            pl.BlockSpec((N_EXPERTS, 1), lambda i: (0, 0))]


def _attn_post(ctx2, x2, o, w_o_bf, mod_l, ln_g, ln_b, wr_hl, rb_p, rows):
    d = x2.shape[1]
    shapes, specs = _post_out(rows.t_all, d)
    return pl.pallas_call(
        functools.partial(_attn_post_kernel, rows=rows),
        out_shape=shapes,
        grid=(rows.n_tiles,),
        in_specs=[
            pl.BlockSpec((ROW_TILE, d), lambda i: (rows.ctx_tile(i), 0)),
            pl.BlockSpec((ROW_TILE, d), lambda i: (rows.lat_tile(i), 0)),
            pl.BlockSpec((ROW_TILE, d), lambda i: (i, 0)),
            pl.BlockSpec((d, d), lambda i: (0, 0)),
            pl.BlockSpec((1, 6, d), lambda i: (rows.seg(i), 0, 0)),
            pl.BlockSpec((1, d), lambda i: (0, 0)),
            pl.BlockSpec((1, d), lambda i: (0, 0)),
        ] + _router_specs(d),
        out_specs=specs,
        compiler_params=_cparams("arbitrary"),
        name="attn_out_norm_route",
    )(ctx2, x2, o, w_o_bf, mod_l, ln_g, ln_b, wr_hl, rb_p)


def _pool_post_kernel(x_ref, xp_ref, xn_ref, wp_ref, ps_ref, mod_ref, lng_ref, lnb_ref, wr_ref, rb_ref,
                      x1_ref, hp_ref, e_ref, g_ref, *, seq_len, tiles_per_seq):
    tm, d = x_ref.shape
    gd = d // len(POOL_WINDOWS)
    r = pl.program_id(0) % tiles_per_seq
    sh1 = mod_ref[0, 0:1, :]
    sc1 = mod_ref[0, 1:2, :]
    x = x_ref[...]
    ext = jnp.concatenate([xp_ref[...], x, xn_ref[...]], axis=0)
    n_ext = tm + 2 * POOL_HALO
    pos = r * tm - POOL_HALO + lax.broadcasted_iota(i32, (n_ext, 1), 0)
    inside = (pos >= 0) & (pos < seq_len)
    u_ext = jnp.where(inside, ext * (1.0 + sc1) + sh1, 0.0)
    t = pos[POOL_HALO:POOL_HALO + tm]
    ys = []
    for g, w in enumerate(POOL_WINDOWS):
        e = u_ext[:, g * gd:(g + 1) * gd]
        p = e + pltpu.roll(e, 1, 0)
        step = 1
        while 2 * step < w:
            p = pltpu.roll(p, step, 0) + pltpu.roll(p, n_ext - step, 0)
            step *= 2
        cnt = (jnp.clip(t + w // 2, 0, seq_len) - jnp.clip(t - w // 2, 0, seq_len)).astype(f32)
        mean = p[POOL_HALO:POOL_HALO + tm] / cnt
        dlt = (mean - e[POOL_HALO:POOL_HALO + tm]).astype(bf16)
        ys.append(jnp.dot(dlt, wp_ref[g], preferred_element_type=f32))
    y = jnp.concatenate(ys, axis=-1) * ps_ref[...]
    _post_epilogue(x, y, mod_ref, lng_ref, lnb_ref, wr_ref, rb_ref, x1_ref, hp_ref, e_ref, g_ref)


def _pool_post(x_all, w_pool_bf, pool_scale, mod_l, ln_g, ln_b, wr_hl, rb_p, rows):
    t_all, d = x_all.shape
    tiles_per_seq = rows.n_seq_tiles
    n_rows = rows.n_batch * rows.seq_len
    halo_per_tile = ROW_TILE // POOL_HALO
    last_halo = t_all // POOL_HALO - 1
    ng = len(POOL_WINDOWS)

    def src_tile(i):
        return (i // tiles_per_seq) * rows.tiles_per_batch + rows.n_ctx_tiles + i % tiles_per_seq

    shapes, specs = _post_out(n_rows, d)
    kern = functools.partial(_pool_post_kernel, seq_len=rows.seq_len, tiles_per_seq=tiles_per_seq)
    return pl.pallas_call(
        kern,
        out_shape=shapes,
        grid=(n_rows // ROW_TILE,),
        in_specs=[
            pl.BlockSpec((ROW_TILE, d), lambda i: (src_tile(i), 0)),
            pl.BlockSpec((POOL_HALO, d), lambda i: (jnp.maximum(src_tile(i) * halo_per_tile - 1, 0), 0)),
            pl.BlockSpec((POOL_HALO, d), lambda i: (jnp.minimum((src_tile(i) + 1) * halo_per_tile, last_halo), 0)),
            pl.BlockSpec((ng, d // ng, d // ng), lambda i: (0, 0, 0)),
            pl.BlockSpec((1, d), lambda i: (0, 0)),
            pl.BlockSpec((1, 6, d), lambda i: (1 + i // tiles_per_seq, 0, 0)),
            pl.BlockSpec((1, d), lambda i: (0, 0)),
            pl.BlockSpec((1, d), lambda i: (0, 0)),
        ] + _router_specs(d),
        out_specs=specs,
        compiler_params=_cparams("arbitrary"),
        name="pool_norm_route",
    )(x_all, x_all, x_all, w_pool_bf, pool_scale, mod_l, ln_g, ln_b, wr_hl, rb_p)


def _rank_kernel(e_ref, rank_ref, cnt_ref, carry_sc):
    tt = e_ref.shape[2]
    first_step = (pl.program_id(0) == 0) & (pl.program_id(1) == 0)

    @pl.when(first_step)
    def _():
        carry_sc[...] = jnp.zeros(carry_sc.shape, f32)

    e = e_ref[0]
    hit = lax.broadcasted_iota(i32, (N_EXPERTS, tt), 0) == e
    onehot = jnp.where(hit, 1.0, 0.0)
    before = lax.broadcasted_iota(i32, (tt, tt), 0) < lax.broadcasted_iota(i32, (tt, tt), 1)
    tri = jnp.where(before, 1.0, 0.0).astype(bf16)
    cum = jnp.dot(onehot.astype(bf16), tri, preferred_element_type=f32)
    carry = carry_sc[...]
    rank = jnp.sum(jnp.where(hit, cum + carry[:, 0:1], 0.0), axis=0, keepdims=True)
    rank_ref[0] = rank.astype(i32)
    carry = carry + jnp.sum(onehot, axis=1, keepdims=True)
    carry_sc[...] = carry
    cnt_ref[...] = carry.astype(i32)


def _rank_assignments(e_idx):
    t = e_idx.shape[1]
    tt = 2 * ROW_TILE if t % (2 * ROW_TILE) == 0 else ROW_TILE
    n = t // tt
    rank, cnt = pl.pallas_call(
        _rank_kernel,
        out_shape=(jax.ShapeDtypeStruct((2 * n, 1, tt), i32),
                   jax.ShapeDtypeStruct((N_EXPERTS, LANES), i32)),
        grid=(2, n),
        in_specs=[pl.BlockSpec((1, 1, tt), lambda k, j: (k * n + j, 0, 0))],
        out_specs=(pl.BlockSpec((1, 1, tt), lambda k, j: (k * n + j, 0, 0)),
                   pl.BlockSpec((N_EXPERTS, LANES), lambda k, j: (0, 0))),
        scratch_shapes=[pltpu.VMEM((N_EXPERTS, LANES), f32)],
        compiler_params=_cparams("arbitrary", "arbitrary"),
        name="route_rank",
    )(e_idx.reshape(2 * n, 1, tt))
    return rank.reshape(2, t), cnt[:, 0]


def _dispatch_plan(e_idx, rank, counts):
    t = e_idx.shape[1]
    padded = (counts + MOE_BLOCK - 1) // MOE_BLOCK * MOE_BLOCK
    p_ends = jnp.cumsum(padded)
    p_starts = p_ends - padded
    dest = (p_starts[e_idx] + rank).astype(i32)
    n_blocks = (2 * t + N_EXPERTS * (MOE_BLOCK - 1) + MOE_BLOCK - 1) // MOE_BLOCK
    blk_start = jnp.arange(n_blocks, dtype=i32) * MOE_BLOCK
    blk_expert = jnp.minimum(jnp.searchsorted(p_ends, blk_start, side='right'), N_EXPERTS - 1).astype(i32)
    n_used = (p_ends[-1] // MOE_BLOCK).astype(i32).reshape(1)
    n_tiles = t // ROW_TILE
    dest3 = dest.reshape(2, n_tiles, ROW_TILE).transpose(1, 0, 2)
    return blk_expert, n_used, dest3, n_blocks


def _dispatch_kernel(dest_ref, hp_hbm, xs_in, xs_hbm, sem):
    del xs_in
    i = pl.program_id(0)
    n = pl.num_programs(0)
    tm = dest_ref.shape[2]
    slot = i % 2

    for kk in range(2):
        for r in range(tm):
            pltpu.make_async_copy(
                hp_hbm.at[pl.ds(i * tm + r, 1), :], xs_hbm.at[pl.ds(dest_ref[0, kk, r], 1), :],
                sem.at[slot]).start()

    def drain(s):
        for _ in range(2 * tm):
            pltpu.make_async_copy(hp_hbm.at[pl.ds(0, 1), :], xs_hbm.at[pl.ds(0, 1), :], sem.at[s]).wait()

    @pl.when(i > 0)
    def _():
        drain(1 - slot)

    @pl.when(i == n - 1)
    def _():
        drain(slot)


def _dispatch(hp, dest3, n_blocks):
    t, dw = hp.shape
    n_tiles = t // ROW_TILE
    xs0 = jnp.zeros((n_blocks * MOE_BLOCK, dw), u32)
    return pl.pallas_call(
        _dispatch_kernel,
        out_shape=jax.ShapeDtypeStruct(xs0.shape, u32),
        grid=(n_tiles,),
        in_specs=[
            pl.BlockSpec((1, 2, ROW_TILE), lambda i: (i, 0, 0), memory_space=pltpu.SMEM),
            pl.BlockSpec(memory_space=pl.ANY),
            pl.BlockSpec(memory_space=pl.ANY),
        ],
        out_specs=pl.BlockSpec(memory_space=pl.ANY),
        scratch_shapes=[pltpu.SemaphoreType.DMA((2,))],
        input_output_aliases={2: 0},
        compiler_params=_cparams("arbitrary"),
        name="moe_dispatch",
    )(dest3, hp, xs0)


def _ffn_kernel(be_ref, nu_ref, xs_ref, wg_ref, wu_ref, wd_ref, y_ref):
    b = pl.program_id(0)

    @pl.when(b < nu_ref[0])
    def _():
        w = xs_ref[...]
        lo = lax.bitcast_convert_type(w << 16, f32)
        hi = lax.bitcast_convert_type(w & jnp.uint32(0xFFFF0000), f32)
        xb = jnp.concatenate([lo, hi], axis=1).astype(bf16)
        a = jnp.dot(xb, wg_ref[0, 0], preferred_element_type=f32)
        u = jnp.dot(xb, wu_ref[0, 0], preferred_element_type=f32)
        z = (a * jax.nn.sigmoid(a) * u).astype(bf16)
        y_ref[...] = jnp.dot(z, wd_ref[0, 0], preferred_element_type=f32)

    @pl.when(b >= nu_ref[0])
    def _():
        y_ref[...] = jnp.zeros(y_ref.shape, f32)


def _expert_ffn(xs, blk_expert, n_used, layer, wg, wu, wd):
    slots, dw = xs.shape
    d = 2 * dw
    de = wg.shape[3]
    n_blocks = slots // MOE_BLOCK
    grid_spec = pltpu.PrefetchScalarGridSpec(
        num_scalar_prefetch=2,
        grid=(n_blocks,),
        in_specs=[
            pl.BlockSpec((MOE_BLOCK, dw), lambda b, be, nu: (b, 0)),
            pl.BlockSpec((1, 1, d, de), lambda b, be, nu: (layer, be[b], 0, 0)),
            pl.BlockSpec((1, 1, d, de), lambda b, be, nu: (layer, be[b], 0, 0)),
            pl.BlockSpec((1, 1, de, d), lambda b, be, nu: (layer, be[b], 0, 0)),
        ],
        out_specs=pl.BlockSpec((MOE_BLOCK, d), lambda b, be, nu: (b, 0)),
    )
    return pl.pallas_call(
        _ffn_kernel,
        out_shape=jax.ShapeDtypeStruct((slots, d), f32),
        grid_spec=grid_spec,
        compiler_params=_cparams("arbitrary"),
        name="expert_ffn",
    )(blk_expert, n_used, xs, wg, wu, wd)


def _fin_kernel(pos_cur, pos_nxt, x1_ref, gate_ref, mod_ref, lng_ref, lnb_ref, y_hbm, o_ref, ybuf, sem):
    i = pl.program_id(0)
    n = pl.num_programs(0)
    tm = x1_ref.shape[0]
    slot = i % 2

    def copy(pos_ref, kk, r, s):
        return pltpu.make_async_copy(
            y_hbm.at[pl.ds(pos_ref[0, kk, r], 1), :], ybuf.at[s, kk, pl.ds(r, 1), :], sem.at[s])

    def start_gather(pos_ref, s):
        for kk in range(2):
            for r in range(tm):
                copy(pos_ref, kk, r, s).start()

    @pl.when(i == 0)
    def _():
        start_gather(pos_cur, 0)

    @pl.when(i + 1 < n)
    def _():
        start_gather(pos_nxt, 1 - slot)

    for kk in range(2):
        for r in range(tm):
            copy(pos_cur, kk, r, slot).wait()
    g2 = mod_ref[0, 5:6, :]
    gate = gate_ref[...]
    f = gate[:, 0:1] * ybuf[slot, 0] + gate[:, 1:2] * ybuf[slot, 1]
    z = DEEPNORM_ALPHA * x1_ref[...] + g2 * f
    mu = jnp.mean(z, axis=-1, keepdims=True)
    zc = z - mu
    var = jnp.mean(zc * zc, axis=-1, keepdims=True)
    o_ref[...] = zc * lax.rsqrt(var + LN_EPS) * lng_ref[...] + lnb_ref[...]


def _combine_norm(x1, gate, pos3, y_sorted, mod_l, ln_g, ln_b, seg_of_tile):
    t, d = x1.shape
    n_tiles = t // ROW_TILE
    return pl.pallas_call(
        _fin_kernel,
        out_shape=jax.ShapeDtypeStruct((t, d), f32),
        grid=(n_tiles,),
        in_specs=[
            pl.BlockSpec((1, 2, ROW_TILE), lambda i: (i, 0, 0), memory_space=pltpu.SMEM),
            pl.BlockSpec((1, 2, ROW_TILE), lambda i: (jnp.minimum(i + 1, n_tiles - 1), 0, 0),
                         memory_space=pltpu.SMEM),
            pl.BlockSpec((ROW_TILE, d), lambda i: (i, 0)),
            pl.BlockSpec((ROW_TILE, GATE_COLS), lambda i: (i, 0)),
            pl.BlockSpec((1, 6, d), lambda i: (seg_of_tile(i), 0, 0)),
            pl.BlockSpec((1, d), lambda i: (0, 0)),
            pl.BlockSpec((1, d), lambda i: (0, 0)),
            pl.BlockSpec(memory_space=pl.ANY),
        ],
        out_specs=pl.BlockSpec((ROW_TILE, d), lambda i: (i, 0)),
        scratch_shapes=[pltpu.VMEM((2, 2, ROW_TILE, d), f32), pltpu.SemaphoreType.DMA((2,))],
        compiler_params=_cparams("arbitrary"),
        name="combine_norm",
    )(pos3, pos3, x1, gate, mod_l, ln_g, ln_b, y_sorted)


def _moe_combine(x1, hp, e_idx, gate, layer, wg_bf, wu_bf, wd_bf, mod_l, ln_g, ln_b, seg_of_tile):
    rank, counts = _rank_assignments(e_idx)
    blk_expert, n_used, dest3, n_blocks = _dispatch_plan(e_idx, rank, counts)
    xs = _dispatch(hp, dest3, n_blocks)
    y_sorted = _expert_ffn(xs, blk_expert, n_used, layer, wg_bf, wu_bf, wd_bf)
    return _combine_norm(x1, gate, dest3, y_sorted, mod_l, ln_g, ln_b, seg_of_tile)


def _rope_tables(ctx_len, seq_len):
    t = jnp.arange(seq_len)
    row = (t // GRID_W).astype(f32)
    col = (t % GRID_W).astype(f32)
    inv_freq = ROPE_THETA ** (-jnp.arange(ROPE_PAIRS, dtype=f32) / ROPE_PAIRS)
    ang_r = row[:, None] * inv_freq
    ang_c = col[:, None] * inv_freq
    ang = jnp.concatenate([ang_r, ang_r, ang_c, ang_c], axis=-1)
    cos = jnp.cos(ang)
    sin = jnp.sin(ang)
    lane = jnp.arange(HEAD_DIM)
    sign = jnp.where((lane % (2 * ROPE_PAIRS)) < ROPE_PAIRS, -1.0, 1.0).astype(f32)
    cos_t = jnp.concatenate([jnp.ones((ctx_len, HEAD_DIM), f32), cos], axis=0)
    sin_t = jnp.concatenate([jnp.zeros((ctx_len, HEAD_DIM), f32), sin * sign], axis=0)
    return cos_t, sin_t


def kernel(x, c, ctx, c_ctx, w_mod, b_mod, ln_g, ln_b, w_qkv, q_gain, k_gain, w_o, w_pool, pool_scale,
           w_router, router_bias, w_gate, w_up, w_down):
    n_batch, seq_len, d = x.shape
    ctx_len = ctx.shape[1]
    assert w_mod.shape[0] == DEPTH and d == N_HEADS * HEAD_DIM
    assert ctx_len % ROW_TILE == 0 and seq_len % ROW_TILE == 0 and 1 + n_batch <= MOD_ROWS
    rows = _Rows(n_batch, ctx_len, seq_len)

    cond = jnp.concatenate([c_ctx[None, :], c], axis=0)
    mod = _modulation(cond, w_mod, b_mod).reshape(DEPTH, MOD_ROWS, 6, d)

    wr_t = w_router.reshape(d, N_EXPERT_GROUPS, EXPERTS_PER_GROUP).transpose(2, 1, 0).reshape(N_EXPERTS, d)
    wr_hi = wr_t.astype(bf16)
    wr_hl = jnp.concatenate([wr_hi, (wr_t - wr_hi.astype(f32)).astype(bf16)], axis=0)
    rb_p = router_bias.reshape(N_EXPERT_GROUPS, EXPERTS_PER_GROUP).T.reshape(N_EXPERTS, 1)
    cos_t, sin_t = _rope_tables(ctx_len, seq_len)

    ctx2 = ctx.reshape(n_batch * ctx_len, d)
    x2 = x.reshape(n_batch * seq_len, d)
    q, k, v = _qkv_proj(ctx2, x2, mod[0], w_qkv[0].astype(bf16), q_gain[0:1], k_gain[0:1], cos_t, sin_t, rows)
    o, wg_bf, wu_bf, wd_bf = _attention(q, k, v, w_gate, w_up, w_down, rows)
    x1, hp, e_idx, gate = _attn_post(ctx2, x2, o, w_o[0].astype(bf16), mod[0], ln_g[0, 0:1], ln_b[0, 0:1],
                                     wr_hl, rb_p, rows)
    x_all = _moe_combine(x1, hp, e_idx, gate, 0, wg_bf, wu_bf, wd_bf, mod[0], ln_g[0, 1:2], ln_b[0, 1:2],
                         rows.seg)

    x1, hp, e_idx, gate = _pool_post(x_all, w_pool[0].astype(bf16), pool_scale[0:1], mod[1],
                                     ln_g[1, 0:1], ln_b[1, 0:1], wr_hl, rb_p, rows)
    out = _moe_combine(x1, hp, e_idx, gate, 1, wg_bf, wu_bf, wd_bf, mod[1], ln_g[1, 1:2], ln_b[1, 1:2],
                       lambda i: 1 + i // rows.n_seq_tiles)
    return out.reshape(n_batch, seq_len, d)
```

```python
import functools

import jax
import jax.numpy as jnp
from jax import lax
from jax.experimental import pallas as pl
from jax.experimental.pallas import tpu as pltpu

N_HEADS = 16
N_KV_HEADS = 4
HEAD_DIM = 128
Q_PER_KV = N_HEADS // N_KV_HEADS
GRID_W = 64
ROPE_THETA = 10000.0
ROPE_PAIRS = HEAD_DIM // 4
ATTN_SCALE = HEAD_DIM ** -0.5
Q_SCALE = ATTN_SCALE * 1.4426950408889634
POOL_WINDOWS = (2, 4, 8, 16)
POOL_HALO = 8
N_EXPERTS = 32
N_EXPERT_GROUPS = 8
EXPERTS_PER_GROUP = N_EXPERTS // N_EXPERT_GROUPS
DEPTH = 2
DEEPNORM_ALPHA = (2 * DEPTH) ** 0.25
LN_EPS = 1e-6
RMS_EPS = 1e-6

LANES = 128
SUBLANES = 8
ROW_TILE = 256
ATTN_TK = 2048
ATTN_ROWS = 128
ATTN_UNROLL = 4
MOE_BLOCK = 256
GATE_COLS = 8
MOD_ROWS = 8
MOD_TN = 512
VMEM_LIMIT = 56 * 1024 * 1024

f32 = jnp.float32
bf16 = jnp.bfloat16
u32 = jnp.uint32
i32 = jnp.int32


def _cparams(*sem):
    return pltpu.CompilerParams(dimension_semantics=sem, vmem_limit_bytes=VMEM_LIMIT)


def _nt_dot(a, b):
    return lax.dot_general(a, b, (((1,), (1,)), ((), ())), preferred_element_type=f32)


def _mod_kernel(c_ref, w_ref, b_ref, o_ref, *, n_rows):
    d = w_ref.shape[1]
    tn = w_ref.shape[2]
    nj = tn // LANES

    def body(k, accs):
        r0 = pl.multiple_of(k * SUBLANES, SUBLANES)
        w8 = w_ref[0, pl.ds(r0, SUBLANES), :]
        out = []
        for r in range(n_rows):
            c8 = c_ref[r, pl.ds(r0, SUBLANES), :]
            s8 = c8 * jax.nn.sigmoid(c8)
            for j in range(nj):
                out.append(accs[r * nj + j] + s8 * w8[:, j * LANES:(j + 1) * LANES])
        return tuple(out)

    init = tuple(jnp.zeros((SUBLANES, LANES), f32) for _ in range(n_rows * nj))
    accs = lax.fori_loop(0, d // SUBLANES, body, init, unroll=2)
    rows = []
    for r in range(n_rows):
        row = jnp.concatenate(
            [jnp.sum(accs[r * nj + j], axis=0, keepdims=True) for j in range(nj)], axis=1)
        rows.append(row + b_ref[0])
    rows.append(jnp.zeros((MOD_ROWS - n_rows, tn), f32))
    o_ref[0] = jnp.concatenate(rows, axis=0)


def _modulation(cond, w_mod, b_mod):
    n_rows, d = cond.shape
    depth, _, n6 = w_mod.shape
    c_b = jnp.broadcast_to(cond[:, :, None], (n_rows, d, LANES))
    return pl.pallas_call(
        functools.partial(_mod_kernel, n_rows=n_rows),
        out_shape=jax.ShapeDtypeStruct((depth, MOD_ROWS, n6), f32),
        grid=(depth, n6 // MOD_TN),
        in_specs=[
            pl.BlockSpec((n_rows, d, LANES), lambda l, j: (0, 0, 0)),
            pl.BlockSpec((1, d, MOD_TN), lambda l, j: (l, 0, j)),
            pl.BlockSpec((1, 1, MOD_TN), lambda l, j: (l, 0, j)),
        ],
        out_specs=pl.BlockSpec((1, MOD_ROWS, MOD_TN), lambda l, j: (l, 0, j)),
        compiler_params=_cparams("arbitrary", "arbitrary"),
        name="adaln_mod",
    )(c_b, w_mod, b_mod.reshape(depth, 1, n6))


class _Rows:
    def __init__(self, n_batch, ctx_len, seq_len):
        self.n_batch = n_batch
        self.ctx_len = ctx_len
        self.seq_len = seq_len
        self.n_ctx_tiles = ctx_len // ROW_TILE
        self.n_seq_tiles = seq_len // ROW_TILE
        self.tiles_per_batch = self.n_ctx_tiles + self.n_seq_tiles
        self.n_tiles = n_batch * self.tiles_per_batch
        self.t_all = self.n_tiles * ROW_TILE

    def is_ctx(self, i):
        return i % self.tiles_per_batch < self.n_ctx_tiles

    def seg(self, i):
        return jnp.where(self.is_ctx(i), 0, 1 + i // self.tiles_per_batch)

    def ctx_tile(self, i):
        r = i % self.tiles_per_batch
        return (i // self.tiles_per_batch) * self.n_ctx_tiles + jnp.minimum(r, self.n_ctx_tiles - 1)

    def lat_tile(self, i):
        r = i % self.tiles_per_batch
        return (i // self.tiles_per_batch) * self.n_seq_tiles + jnp.maximum(r - self.n_ctx_tiles, 0)


def _qkv_kernel(xc_ref, xl_ref, mod_ref, w_ref, qg_ref, kg_ref, cos_ref, sin_ref, q_ref, k_ref, v_ref, *, rows):
    tm, d = xl_ref.shape
    kv_dim = k_ref.shape[1]
    x = jnp.where(rows.is_ctx(pl.program_id(0)), xc_ref[...], xl_ref[...])
    sh1 = mod_ref[0, 0:1, :]
    sc1 = mod_ref[0, 1:2, :]
    u = (x * (1.0 + sc1) + sh1).astype(bf16)
    qkv = jnp.dot(u, w_ref[...], preferred_element_type=f32)
    cos = cos_ref[...]
    sin = sin_ref[...]
    lane = lax.broadcasted_iota(i32, (tm, HEAD_DIM), 1)
    first = (lane % (2 * ROPE_PAIRS)) < ROPE_PAIRS

    def norm_rope(xh, gain, scale):
        ms = jnp.mean(xh * xh, axis=-1, keepdims=True)
        xn = xh * lax.rsqrt(ms + RMS_EPS) * gain
        rot = jnp.where(first, pltpu.roll(xn, HEAD_DIM - ROPE_PAIRS, 1), pltpu.roll(xn, ROPE_PAIRS, 1))
        y = xn * cos + rot * sin
        return y * scale if scale != 1.0 else y

    qg = qg_ref[...]
    kg = kg_ref[...]
    for h in range(N_HEADS):
        sl = slice(h * HEAD_DIM, (h + 1) * HEAD_DIM)
        q_ref[:, sl] = norm_rope(qkv[:, sl], qg, Q_SCALE).astype(bf16)
    ones = jnp.ones((tm, HEAD_DIM), bf16)
    for h in range(N_KV_HEADS):
        sl = slice(h * HEAD_DIM, (h + 1) * HEAD_DIM)
        k_ref[:, sl] = norm_rope(qkv[:, d + h * HEAD_DIM:d + (h + 1) * HEAD_DIM], kg, 1.0).astype(bf16)
        v0 = d + kv_dim + h * HEAD_DIM
        v_ref[:, 2 * h * HEAD_DIM:(2 * h + 1) * HEAD_DIM] = qkv[:, v0:v0 + HEAD_DIM].astype(bf16)
        v_ref[:, (2 * h + 1) * HEAD_DIM:(2 * h + 2) * HEAD_DIM] = ones


def _qkv_proj(ctx2, x2, mod_l, w_qkv_bf, q_gain, k_gain, cos_t, sin_t, rows):
    d = x2.shape[1]
    kv_dim = N_KV_HEADS * HEAD_DIM
    tpb = rows.tiles_per_batch
    return pl.pallas_call(
        functools.partial(_qkv_kernel, rows=rows),
        out_shape=(jax.ShapeDtypeStruct((rows.t_all, d), bf16),
                   jax.ShapeDtypeStruct((rows.t_all, kv_dim), bf16),
                   jax.ShapeDtypeStruct((rows.t_all, 2 * kv_dim), bf16)),
        grid=(rows.n_tiles,),
        in_specs=[
            pl.BlockSpec((ROW_TILE, d), lambda i: (rows.ctx_tile(i), 0)),
            pl.BlockSpec((ROW_TILE, d), lambda i: (rows.lat_tile(i), 0)),
            pl.BlockSpec((1, 6, d), lambda i: (rows.seg(i), 0, 0)),
            pl.BlockSpec((d, d + 2 * kv_dim), lambda i: (0, 0)),
            pl.BlockSpec((1, HEAD_DIM), lambda i: (0, 0)),
            pl.BlockSpec((1, HEAD_DIM), lambda i: (0, 0)),
            pl.BlockSpec((ROW_TILE, HEAD_DIM), lambda i: (i % tpb, 0)),
            pl.BlockSpec((ROW_TILE, HEAD_DIM), lambda i: (i % tpb, 0)),
        ],
        out_specs=(pl.BlockSpec((ROW_TILE, d), lambda i: (i, 0)),
                   pl.BlockSpec((ROW_TILE, kv_dim), lambda i: (i, 0)),
                   pl.BlockSpec((ROW_TILE, 2 * kv_dim), lambda i: (i, 0))),
        compiler_params=_cparams("arbitrary"),
        name="qkv_rope",
    )(ctx2, x2, mod_l, w_qkv_bf, q_gain, k_gain, cos_t, sin_t)


def _attn_kernel(q_ref, k_ref, v_ref, wg_ref, wu_ref, wd_ref, o_ref, wgo_ref, wuo_ref, wdo_ref, m_sc, acc_sc,
                 *, ctx_len, tk, rs, n_ctx_tiles):
    tq = q_ref.shape[0]
    n_lat_chunks = (k_ref.shape[0] - ctx_len) // tk
    qi = pl.program_id(2)
    subs = [(h, r0) for h in range(Q_PER_KV) for r0 in range(0, tq, rs)]

    wgo_ref[...] = wg_ref[...].astype(bf16)
    wuo_ref[...] = wu_ref[...].astype(bf16)
    wdo_ref[...] = wd_ref[...].astype(bf16)

    def step(off, size, first):
        kc = k_ref[pl.ds(off, size), :]
        vc = v_ref[pl.ds(off, size), :]
        for i, (h, r0) in enumerate(subs):
            qh = q_ref[r0:r0 + rs, h * HEAD_DIM:(h + 1) * HEAD_DIM]
            s = _nt_dot(qh, kc)
            mx = s[:, 0:LANES]
            for j in range(1, size // LANES):
                mx = jnp.maximum(mx, s[:, j * LANES:(j + 1) * LANES])
            m_cur = jnp.max(mx, axis=-1, keepdims=True)
            if first:
                m_new = jnp.broadcast_to(m_cur, (rs, LANES))
            else:
                m_prev = m_sc[i]
                m_new = jnp.maximum(m_prev, m_cur)
            p = jnp.concatenate(
                [jnp.exp2(s[:, j * LANES:(j + 1) * LANES] - m_new) for j in range(size // LANES)],
                axis=1).astype(bf16)
            pv = jnp.dot(p, vc, preferred_element_type=f32)
            if first:
                acc_sc[i] = pv
            else:
                alpha = jnp.exp2(m_prev - m_new)
                acc_sc[i] = acc_sc[i] * jnp.concatenate([alpha, alpha], axis=1) + pv
            m_sc[i] = m_new

    step(0, ctx_len, True)

    @pl.when(qi >= n_ctx_tiles)
    def _():
        def body(c, carry):
            step(pl.multiple_of(ctx_len + c * tk, LANES), tk, False)
            return carry
        lax.fori_loop(0, n_lat_chunks, body, 0, unroll=min(ATTN_UNROLL, n_lat_chunks))

    for i, (h, r0) in enumerate(subs):
        acc = acc_sc[i]
        o_ref[r0:r0 + rs, h * HEAD_DIM:(h + 1) * HEAD_DIM] = (
            acc[:, :HEAD_DIM] / acc[:, HEAD_DIM:]).astype(bf16)


def _attention(q, k, v, w_gate, w_up, w_down, rows):
    t_all, d = q.shape
    tpb = rows.tiles_per_batch
    rows_per_batch = t_all // rows.n_batch
    gw = Q_PER_KV * HEAD_DIM
    n_sub = Q_PER_KV * (ROW_TILE // ATTN_ROWS)
    tk = min(ATTN_TK, rows.seq_len)
    assert rows.seq_len % tk == 0

    n_steps = rows.n_batch * N_KV_HEADS * tpb
    wg2 = w_gate.reshape(-1, w_gate.shape[-1])
    wu2 = w_up.reshape(-1, w_up.shape[-1])
    wd2 = w_down.reshape(-1, w_down.shape[-1])
    n_slabs = 1 << (n_steps.bit_length() - 1)
    slab_g = wg2.shape[0] // n_slabs
    slab_d = wd2.shape[0] // n_slabs
    assert wg2.shape[0] == n_slabs * slab_g and wd2.shape[0] == n_slabs * slab_d and wu2.shape == wg2.shape
    assert slab_g % SUBLANES == 0 and slab_d % SUBLANES == 0

    def slab(b, g, i):
        return (jnp.minimum((b * N_KV_HEADS + g) * tpb + i, n_slabs - 1), 0)

    kern = functools.partial(_attn_kernel, ctx_len=rows.ctx_len, tk=tk, rs=ATTN_ROWS,
                             n_ctx_tiles=rows.n_ctx_tiles)
    o, wg_bf, wu_bf, wd_bf = pl.pallas_call(
        kern,
        out_shape=(jax.ShapeDtypeStruct((t_all, d), bf16),
                   jax.ShapeDtypeStruct(wg2.shape, bf16),
                   jax.ShapeDtypeStruct(wu2.shape, bf16),
                   jax.ShapeDtypeStruct(wd2.shape, bf16)),
        grid=(rows.n_batch, N_KV_HEADS, tpb),
        in_specs=[
            pl.BlockSpec((ROW_TILE, gw), lambda b, g, i: (b * tpb + i, g)),
            pl.BlockSpec((rows_per_batch, HEAD_DIM), lambda b, g, i: (b, g)),
            pl.BlockSpec((rows_per_batch, 2 * HEAD_DIM), lambda b, g, i: (b, g)),
            pl.BlockSpec((slab_g, wg2.shape[1]), slab),
            pl.BlockSpec((slab_g, wu2.shape[1]), slab),
            pl.BlockSpec((slab_d, wd2.shape[1]), slab),
        ],
        out_specs=(pl.BlockSpec((ROW_TILE, gw), lambda b, g, i: (b * tpb + i, g)),
                   pl.BlockSpec((slab_g, wg2.shape[1]), slab),
                   pl.BlockSpec((slab_g, wu2.shape[1]), slab),
                   pl.BlockSpec((slab_d, wd2.shape[1]), slab)),
        scratch_shapes=[
            pltpu.VMEM((n_sub, ATTN_ROWS, LANES), f32),
            pltpu.VMEM((n_sub, ATTN_ROWS, 2 * HEAD_DIM), f32),
        ],
        compiler_params=_cparams("arbitrary", "arbitrary", "arbitrary"),
        name="gqa_attention",
    )(q, k, v, wg2, wu2, wd2)
    return o, wg_bf.reshape(w_gate.shape), wu_bf.reshape(w_up.shape), wd_bf.reshape(w_down.shape)


def _post_epilogue(x, y, mod_ref, lng_ref, lnb_ref, wr_ref, rb_ref, x1_ref, hp_ref, e_ref, g_ref):
    tm, d = x.shape
    g1 = mod_ref[0, 2:3, :]
    sh2 = mod_ref[0, 3:4, :]
    sc2 = mod_ref[0, 4:5, :]
    z = DEEPNORM_ALPHA * x + g1 * y
    mu = jnp.mean(z, axis=-1, keepdims=True)
    zc = z - mu
    var = jnp.mean(zc * zc, axis=-1, keepdims=True)
    x1 = zc * lax.rsqrt(var + LN_EPS) * lng_ref[...] + lnb_ref[...]
    h = x1 * (1.0 + sc2) + sh2
    x1_ref[...] = x1

    h_hi = h.astype(bf16)
    h_hi32 = h_hi.astype(f32)
    bits = lax.bitcast_convert_type(h_hi32, u32)
    hp_ref[...] = (bits[:, :d // 2] >> 16) | bits[:, d // 2:]

    h_lo = (h - h_hi32).astype(bf16)
    wr = wr_ref[...]
    a2 = _nt_dot(wr, h_hi)
    lt = a2[:N_EXPERTS] + a2[N_EXPERTS:] + _nt_dot(wr[:N_EXPERTS], h_lo)
    aff = jax.nn.sigmoid(lt)
    sel = aff + rb_ref[...]
    ng = N_EXPERT_GROUPS
    a = [aff[j * ng:(j + 1) * ng] for j in range(EXPERTS_PER_GROUP)]
    s = [sel[j * ng:(j + 1) * ng] for j in range(EXPERTS_PER_GROUP)]
    gs = None
    for i in range(EXPERTS_PER_GROUP):
        for j in range(i + 1, EXPERTS_PER_GROUP):
            pij = s[i] + s[j]
            gs = pij if gs is None else jnp.maximum(gs, pij)
    gio = lax.broadcasted_iota(i32, gs.shape, 0)
    gmax = jnp.max(gs, axis=0, keepdims=True)
    g_idx = jnp.min(jnp.where(gs == gmax, gio, ng), axis=0, keepdims=True)
    onehot = gio == g_idx
    sin_g = [jnp.sum(jnp.where(onehot, s[j], 0.0), axis=0, keepdims=True) for j in range(EXPERTS_PER_GROUP)]
    aff_g = [jnp.sum(jnp.where(onehot, a[j], 0.0), axis=0, keepdims=True) for j in range(EXPERTS_PER_GROUP)]
    loc = [jnp.zeros_like(g_idx), jnp.zeros_like(g_idx)]
    gat = [jnp.zeros_like(gmax), jnp.zeros_like(gmax)]
    for j in range(EXPERTS_PER_GROUP):
        rank = jnp.zeros_like(g_idx)
        for i in range(EXPERTS_PER_GROUP):
            if i == j:
                continue
            ahead = (sin_g[i] >= sin_g[j]) if i < j else (sin_g[i] > sin_g[j])
            rank = rank + ahead.astype(i32)
        for kk in range(2):
            hit = rank == kk
            loc[kk] = jnp.where(hit, j, loc[kk])
            gat[kk] = jnp.where(hit, aff_g[j], gat[kk])
    den = gat[0] + gat[1]
    e_ref[0] = jnp.concatenate(
        [g_idx * EXPERTS_PER_GROUP + loc[0], g_idx * EXPERTS_PER_GROUP + loc[1]], axis=0)
    g8 = jnp.concatenate([gat[0] / den, gat[1] / den, jnp.zeros((GATE_COLS - 2, tm), f32)], axis=0)
    g_ref[...] = g8.T


def _attn_post_kernel(xc_ref, xl_ref, o_ref, wo_ref, mod_ref, lng_ref, lnb_ref, wr_ref, rb_ref,
                      x1_ref, hp_ref, e_ref, g_ref, *, rows):
    x = jnp.where(rows.is_ctx(pl.program_id(0)), xc_ref[...], xl_ref[...])
    y = jnp.dot(o_ref[...], wo_ref[...], preferred_element_type=f32)
    _post_epilogue(x, y, mod_ref, lng_ref, lnb_ref, wr_ref, rb_ref, x1_ref, hp_ref, e_ref, g_ref)


def _post_out(n_rows, d):
    shapes = (jax.ShapeDtypeStruct((n_rows, d), f32),
              jax.ShapeDtypeStruct((n_rows, d // 2), u32),
              jax.ShapeDtypeStruct((n_rows // ROW_TILE, 2, ROW_TILE), i32),
              jax.ShapeDtypeStruct((n_rows, GATE_COLS), f32))
    specs = (pl.BlockSpec((ROW_TILE, d), lambda i: (i, 0)),
             pl.BlockSpec((ROW_TILE, d // 2), lambda i: (i, 0)),
             pl.BlockSpec((1, 2, ROW_TILE), lambda i: (i, 0, 0)),
             pl.BlockSpec((ROW_TILE, GATE_COLS), lambda i: (i, 0)))
    return shapes, specs


def _router_specs(d):
    return [pl.BlockSpec((2 * N_EXPERTS, d), lambda i: (0, 0)),
            pl.BlockSpec((N_EXPERTS, 1), lambda i: (0, 0))]


def _attn_post(ctx2, x2, o, w_o_bf, mod_l, ln_g, ln_b, wr_hl, rb_p, rows):
    d = x2.shape[1]
    shapes, specs = _post_out(rows.t_all, d)
    return pl.pallas_call(
        functools.partial(_attn_post_kernel, rows=rows),
        out_shape=shapes,
        grid=(rows.n_tiles,),
        in_specs=[
            pl.BlockSpec((ROW_TILE, d), lambda i: (rows.ctx_tile(i), 0)),
            pl.BlockSpec((ROW_TILE, d), lambda i: (rows.lat_tile(i), 0)),
            pl.BlockSpec((ROW_TILE, d), lambda i: (i, 0)),
            pl.BlockSpec((d, d), lambda i: (0, 0)),
            pl.BlockSpec((1, 6, d), lambda i: (rows.seg(i), 0, 0)),
            pl.BlockSpec((1, d), lambda i: (0, 0)),
            pl.BlockSpec((1, d), lambda i: (0, 0)),
        ] + _router_specs(d),
        out_specs=specs,
        compiler_params=_cparams("arbitrary"),
        name="attn_out_norm_route",
    )(ctx2, x2, o, w_o_bf, mod_l, ln_g, ln_b, wr_hl, rb_p)


def _pool_post_kernel(x_ref, xp_ref, xn_ref, wp_ref, ps_ref, mod_ref, lng_ref, lnb_ref, wr_ref, rb_ref,
                      x1_ref, hp_ref, e_ref, g_ref, *, seq_len, tiles_per_seq):
    tm, d = x_ref.shape
    gd = d // len(POOL_WINDOWS)
    r = pl.program_id(0) % tiles_per_seq
    sh1 = mod_ref[0, 0:1, :]
    sc1 = mod_ref[0, 1:2, :]
    x = x_ref[...]
    ext = jnp.concatenate([xp_ref[...], x, xn_ref[...]], axis=0)
    n_ext = tm + 2 * POOL_HALO
    pos = r * tm - POOL_HALO + lax.broadcasted_iota(i32, (n_ext, 1), 0)
    inside = (pos >= 0) & (pos < seq_len)
    u_ext = jnp.where(inside, ext * (1.0 + sc1) + sh1, 0.0)
    t = pos[POOL_HALO:POOL_HALO + tm]
    ys = []
    for g, w in enumerate(POOL_WINDOWS):
        e = u_ext[:, g * gd:(g + 1) * gd]
        p = e + pltpu.roll(e, 1, 0)
        step = 1
        while 2 * step < w:
            p = pltpu.roll(p, step, 0) + pltpu.roll(p, n_ext - step, 0)
            step *= 2
        cnt = (jnp.clip(t + w // 2, 0, seq_len) - jnp.clip(t - w // 2, 0, seq_len)).astype(f32)
        mean = p[POOL_HALO:POOL_HALO + tm] / cnt
        dlt = (mean - e[POOL_HALO:POOL_HALO + tm]).astype(bf16)
        ys.append(jnp.dot(dlt, wp_ref[g], preferred_element_type=f32))
    y = jnp.concatenate(ys, axis=-1) * ps_ref[...]
    _post_epilogue(x, y, mod_ref, lng_ref, lnb_ref, wr_ref, rb_ref, x1_ref, hp_ref, e_ref, g_ref)


def _pool_post(x_all, w_pool_bf, pool_scale, mod_l, ln_g, ln_b, wr_hl, rb_p, rows):
    t_all, d = x_all.shape
    tiles_per_seq = rows.n_seq_tiles
    n_rows = rows.n_batch * rows.seq_len
    halo_per_tile = ROW_TILE // POOL_HALO
    last_halo = t_all // POOL_HALO - 1
    ng = len(POOL_WINDOWS)

    def src_tile(i):
        return (i // tiles_per_seq) * rows.tiles_per_batch + rows.n_ctx_tiles + i % tiles_per_seq

    shapes, specs = _post_out(n_rows, d)
    kern = functools.partial(_pool_post_kernel, seq_len=rows.seq_len, tiles_per_seq=tiles_per_seq)
    return pl.pallas_call(
        kern,
        out_shape=shapes,
        grid=(n_rows // ROW_TILE,),
        in_specs=[
            pl.BlockSpec((ROW_TILE, d), lambda i: (src_tile(i), 0)),
            pl.BlockSpec((POOL_HALO, d), lambda i: (jnp.maximum(src_tile(i) * halo_per_tile - 1, 0), 0)),
            pl.BlockSpec((POOL_HALO, d), lambda i: (jnp.minimum((src_tile(i) + 1) * halo_per_tile, last_halo), 0)),
            pl.BlockSpec((ng, d // ng, d // ng), lambda i: (0, 0, 0)),
            pl.BlockSpec((1, d), lambda i: (0, 0)),
            pl.BlockSpec((1, 6, d), lambda i: (1 + i // tiles_per_seq, 0, 0)),
            pl.BlockSpec((1, d), lambda i: (0, 0)),
            pl.BlockSpec((1, d), lambda i: (0, 0)),
        ] + _router_specs(d),
        out_specs=specs,
        compiler_params=_cparams("arbitrary"),
        name="pool_norm_route",
    )(x_all, x_all, x_all, w_pool_bf, pool_scale, mod_l, ln_g, ln_b, wr_hl, rb_p)


def _plan_kernel(e_ref, dest_ref, be_ref, nu_ref, carry_sc, start_sc):
    phase = pl.program_id(0)
    i = pl.program_id(1)
    tm = e_ref.shape[2]
    sub = lax.broadcasted_iota(i32, (N_EXPERTS, tm), 0)
    shift = MOE_BLOCK.bit_length() - 1

    @pl.when((phase == 0) & (i == 0))
    def _():
        carry_sc[...] = jnp.zeros(carry_sc.shape, f32)

    @pl.when(phase == 0)
    def _():
        carry = carry_sc[...]
        for kk in range(2):
            hit = sub == e_ref[0, kk:kk + 1, :]
            carry = carry + jnp.sum(jnp.where(hit, 1.0, 0.0), axis=1, keepdims=True)
        carry_sc[...] = carry

    @pl.when((phase == 1) & (i == 0))
    def _():
        counts = carry_sc[...].astype(i32)
        padded = ((counts + (MOE_BLOCK - 1)) >> shift) << shift
        esub = lax.broadcasted_iota(i32, padded.shape, 0)
        start = jnp.zeros(padded.shape, i32)
        for e in range(N_EXPERTS - 1):
            start = start + jnp.where(esub > e, padded[e:e + 1, :], 0)
        ends = start + padded
        nb_pad = be_ref.shape[1]
        blk_start = lax.broadcasted_iota(i32, (1, nb_pad), 1) * MOE_BLOCK
        be = jnp.zeros((1, nb_pad), i32)
        for e in range(N_EXPERTS):
            end_e = jnp.concatenate([ends[e:e + 1, :]] * (nb_pad // LANES), axis=1)
            be = be + jnp.where(end_e <= blk_start, 1, 0)
        be_ref[...] = jnp.minimum(be, N_EXPERTS - 1)
        nu_ref[...] = ends[N_EXPERTS - 1:N_EXPERTS, :] >> shift
        start_sc[...] = start.astype(f32)
        carry_sc[...] = jnp.zeros(carry_sc.shape, f32)

    @pl.when(phase == 1)
    def _():
        before = lax.broadcasted_iota(i32, (tm, tm), 0) < lax.broadcasted_iota(i32, (tm, tm), 1)
        tri = jnp.where(before, 1.0, 0.0).astype(bf16)
        carry = carry_sc[...]
        start = start_sc[...]
        out = []
        for kk in range(2):
            hit = sub == e_ref[0, kk:kk + 1, :]
            onehot = jnp.where(hit, 1.0, 0.0)
            cum = jnp.dot(onehot.astype(bf16), tri, preferred_element_type=f32)
            base = (carry + start)[:, 0:1]
            out.append(jnp.sum(jnp.where(hit, cum + base, 0.0), axis=0, keepdims=True))
            carry = carry + jnp.sum(onehot, axis=1, keepdims=True)
        dest_ref[0] = jnp.concatenate(out, axis=0).astype(i32)
        carry_sc[...] = carry


def _dispatch_plan(e3):
    n_tiles, _, tm = e3.shape
    n_blocks = (2 * n_tiles * tm + N_EXPERTS * (MOE_BLOCK - 1) + MOE_BLOCK - 1) // MOE_BLOCK
    nb_pad = (n_blocks + LANES - 1) // LANES * LANES
    dest3, be, nu = pl.pallas_call(
        _plan_kernel,
        out_shape=(jax.ShapeDtypeStruct(e3.shape, i32),
                   jax.ShapeDtypeStruct((1, nb_pad), i32),
                   jax.ShapeDtypeStruct((1, LANES), i32)),
        grid=(2, n_tiles),
        in_specs=[pl.BlockSpec((1, 2, tm), lambda p, i: (i, 0, 0))],
        out_specs=(pl.BlockSpec((1, 2, tm), lambda p, i: (p * i, 0, 0)),
                   pl.BlockSpec((1, nb_pad), lambda p, i: (0, 0)),
                   pl.BlockSpec((1, LANES), lambda p, i: (0, 0))),
        scratch_shapes=[pltpu.VMEM((N_EXPERTS, LANES), f32), pltpu.VMEM((N_EXPERTS, LANES), f32)],
        compiler_params=_cparams("arbitrary", "arbitrary"),
        name="route_plan",
    )(e3)
    return be[0, :n_blocks], nu[0, :1], dest3, n_blocks


def _dispatch_kernel(dest_ref, hp_ref, xs_in, xs_hbm, buf, sem):
    del xs_in
    i = pl.program_id(0)
    n = pl.num_programs(0)
    tm = hp_ref.shape[0]
    slot = i % 2
    buf[slot] = hp_ref[...]
    for kk in range(2):
        for r in range(tm):
            pltpu.make_async_copy(
                buf.at[slot, pl.ds(r, 1), :], xs_hbm.at[pl.ds(dest_ref[0, kk, r], 1), :], sem.at[slot]).start()

    def drain(s):
        for _ in range(2 * tm):
            pltpu.make_async_copy(buf.at[s, pl.ds(0, 1), :], xs_hbm.at[pl.ds(0, 1), :], sem.at[s]).wait()

    @pl.when(i > 0)
    def _():
        drain(1 - slot)

    @pl.when(i == n - 1)
    def _():
        drain(slot)


def _dispatch(hp, dest3, n_blocks):
    t, dw = hp.shape
    n_tiles = t // ROW_TILE
    xs0 = jnp.zeros((n_blocks * MOE_BLOCK, dw), u32)
    return pl.pallas_call(
        _dispatch_kernel,
        out_shape=jax.ShapeDtypeStruct(xs0.shape, u32),
        grid=(n_tiles,),
        in_specs=[
            pl.BlockSpec((1, 2, ROW_TILE), lambda i: (i, 0, 0), memory_space=pltpu.SMEM),
            pl.BlockSpec((ROW_TILE, dw), lambda i: (i, 0)),
            pl.BlockSpec(memory_space=pl.ANY),
        ],
        out_specs=pl.BlockSpec(memory_space=pl.ANY),
        scratch_shapes=[pltpu.VMEM((2, ROW_TILE, dw), u32), pltpu.SemaphoreType.DMA((2,))],
        input_output_aliases={2: 0},
        compiler_params=_cparams("arbitrary"),
        name="moe_dispatch",
    )(dest3, hp, xs0)


def _ffn_kernel(be_ref, nu_ref, xs_ref, wg_ref, wu_ref, wd_ref, y_ref):
    b = pl.program_id(0)

    @pl.when(b < nu_ref[0])
    def _():
        w = xs_ref[...]
        lo = lax.bitcast_convert_type(w << 16, f32)
        hi = lax.bitcast_convert_type(w & jnp.uint32(0xFFFF0000), f32)
        xb = jnp.concatenate([lo, hi], axis=1).astype(bf16)
        a = jnp.dot(xb, wg_ref[0, 0], preferred_element_type=f32)
        u = jnp.dot(xb, wu_ref[0, 0], preferred_element_type=f32)
        z = (a * jax.nn.sigmoid(a) * u).astype(bf16)
        y_ref[...] = jnp.dot(z, wd_ref[0, 0], preferred_element_type=f32)

    @pl.when(b >= nu_ref[0])
    def _():
        y_ref[...] = jnp.zeros(y_ref.shape, f32)


def _expert_ffn(xs, blk_expert, n_used, layer, wg, wu, wd):
    slots, dw = xs.shape
    d = 2 * dw
    de = wg.shape[3]
    n_blocks = slots // MOE_BLOCK
    grid_spec = pltpu.PrefetchScalarGridSpec(
        num_scalar_prefetch=2,
        grid=(n_blocks,),
        in_specs=[
            pl.BlockSpec((MOE_BLOCK, dw), lambda b, be, nu: (b, 0)),
            pl.BlockSpec((1, 1, d, de), lambda b, be, nu: (layer, be[b], 0, 0)),
            pl.BlockSpec((1, 1, d, de), lambda b, be, nu: (layer, be[b], 0, 0)),
            pl.BlockSpec((1, 1, de, d), lambda b, be, nu: (layer, be[b], 0, 0)),
        ],
        out_specs=pl.BlockSpec((MOE_BLOCK, d), lambda b, be, nu: (b, 0)),
    )
    return pl.pallas_call(
        _ffn_kernel,
        out_shape=jax.ShapeDtypeStruct((slots, d), f32),
        grid_spec=grid_spec,
        compiler_params=_cparams("arbitrary"),
        name="expert_ffn",
    )(blk_expert, n_used, xs, wg, wu, wd)


def _fin_kernel(pos_cur, pos_nxt, x1_ref, gate_ref, mod_ref, lng_ref, lnb_ref, y_hbm, o_ref, ybuf, sem):
    i = pl.program_id(0)
    n = pl.num_programs(0)
    tm = x1_ref.shape[0]
    slot = i % 2

    def copy(pos_ref, kk, r, s):
        return pltpu.make_async_copy(
            y_hbm.at[pl.ds(pos_ref[0, kk, r], 1), :], ybuf.at[s, kk, pl.ds(r, 1), :], sem.at[s])

    def start_gather(pos_ref, s):
        for kk in range(2):
            for r in range(tm):
                copy(pos_ref, kk, r, s).start()

    @pl.when(i == 0)
    def _():
        start_gather(pos_cur, 0)

    @pl.when(i + 1 < n)
    def _():
        start_gather(pos_nxt, 1 - slot)

    for kk in range(2):
        for r in range(tm):
            copy(pos_cur, kk, r, slot).wait()
    g2 = mod_ref[0, 5:6, :]
    gate = gate_ref[...]
    f = gate[:, 0:1] * ybuf[slot, 0] + gate[:, 1:2] * ybuf[slot, 1]
    z = DEEPNORM_ALPHA * x1_ref[...] + g2 * f
    mu = jnp.mean(z, axis=-1, keepdims=True)
    zc = z - mu
    var = jnp.mean(zc * zc, axis=-1, keepdims=True)
    o_ref[...] = zc * lax.rsqrt(var + LN_EPS) * lng_ref[...] + lnb_ref[...]


def _combine_norm(x1, gate, pos3, y_sorted, mod_l, ln_g, ln_b, seg_of_tile):
    t, d = x1.shape
    n_tiles = t // ROW_TILE
    return pl.pallas_call(
        _fin_kernel,
        out_shape=jax.ShapeDtypeStruct((t, d), f32),
        grid=(n_tiles,),
        in_specs=[
            pl.BlockSpec((1, 2, ROW_TILE), lambda i: (i, 0, 0), memory_space=pltpu.SMEM),
            pl.BlockSpec((1, 2, ROW_TILE), lambda i: (jnp.minimum(i + 1, n_tiles - 1), 0, 0),
                         memory_space=pltpu.SMEM),
            pl.BlockSpec((ROW_TILE, d), lambda i: (i, 0)),
            pl.BlockSpec((ROW_TILE, GATE_COLS), lambda i: (i, 0)),
            pl.BlockSpec((1, 6, d), lambda i: (seg_of_tile(i), 0, 0)),
            pl.BlockSpec((1, d), lambda i: (0, 0)),
            pl.BlockSpec((1, d), lambda i: (0, 0)),
            pl.BlockSpec(memory_space=pl.ANY),
        ],
        out_specs=pl.BlockSpec((ROW_TILE, d), lambda i: (i, 0)),
        scratch_shapes=[pltpu.VMEM((2, 2, ROW_TILE, d), f32), pltpu.SemaphoreType.DMA((2,))],
        compiler_params=_cparams("arbitrary"),
        name="combine_norm",
    )(pos3, pos3, x1, gate, mod_l, ln_g, ln_b, y_sorted)


def _moe_combine(x1, hp, e_idx, gate, layer, wg_bf, wu_bf, wd_bf, mod_l, ln_g, ln_b, seg_of_tile):
    blk_expert, n_used, dest3, n_blocks = _dispatch_plan(e_idx)
    xs = _dispatch(hp, dest3, n_blocks)
    y_sorted = _expert_ffn(xs, blk_expert, n_used, layer, wg_bf, wu_bf, wd_bf)
    return _combine_norm(x1, gate, dest3, y_sorted, mod_l, ln_g, ln_b, seg_of_tile)


def _rope_tables(ctx_len, seq_len):
    t = jnp.arange(seq_len)
    row = (t // GRID_W).astype(f32)
    col = (t % GRID_W).astype(f32)
    inv_freq = ROPE_THETA ** (-jnp.arange(ROPE_PAIRS, dtype=f32) / ROPE_PAIRS)
    ang_r = row[:, None] * inv_freq
    ang_c = col[:, None] * inv_freq
    ang = jnp.concatenate([ang_r, ang_r, ang_c, ang_c], axis=-1)
    cos = jnp.cos(ang)
    sin = jnp.sin(ang)
    lane = jnp.arange(HEAD_DIM)
    sign = jnp.where((lane % (2 * ROPE_PAIRS)) < ROPE_PAIRS, -1.0, 1.0).astype(f32)
    cos_t = jnp.concatenate([jnp.ones((ctx_len, HEAD_DIM), f32), cos], axis=0)
    sin_t = jnp.concatenate([jnp.zeros((ctx_len, HEAD_DIM), f32), sin * sign], axis=0)
    return cos_t, sin_t


def kernel(x, c, ctx, c_ctx, w_mod, b_mod, ln_g, ln_b, w_qkv, q_gain, k_gain, w_o, w_pool, pool_scale,
           w_router, router_bias, w_gate, w_up, w_down):
    n_batch, seq_len, d = x.shape
    ctx_len = ctx.shape[1]
    assert w_mod.shape[0] == DEPTH and d == N_HEADS * HEAD_DIM
    assert ctx_len % ROW_TILE == 0 and seq_len % ROW_TILE == 0 and 1 + n_batch <= MOD_ROWS
    rows = _Rows(n_batch, ctx_len, seq_len)

    cond = jnp.concatenate([c_ctx[None, :], c], axis=0)
    mod = _modulation(cond, w_mod, b_mod).reshape(DEPTH, MOD_ROWS, 6, d)

    wr_t = w_router.reshape(d, N_EXPERT_GROUPS, EXPERTS_PER_GROUP).transpose(2, 1, 0).reshape(N_EXPERTS, d)
    wr_hi = wr_t.astype(bf16)
    wr_hl = jnp.concatenate([wr_hi, (wr_t - wr_hi.astype(f32)).astype(bf16)], axis=0)
    rb_p = router_bias.reshape(N_EXPERT_GROUPS, EXPERTS_PER_GROUP).T.reshape(N_EXPERTS, 1)
    cos_t, sin_t = _rope_tables(ctx_len, seq_len)

    ctx2 = ctx.reshape(n_batch * ctx_len, d)
    x2 = x.reshape(n_batch * seq_len, d)
    q, k, v = _qkv_proj(ctx2, x2, mod[0], w_qkv[0].astype(bf16), q_gain[0:1], k_gain[0:1], cos_t, sin_t, rows)
    o, wg_bf, wu_bf, wd_bf = _attention(q, k, v, w_gate, w_up, w_down, rows)
    x1, hp, e_idx, gate = _attn_post(ctx2, x2, o, w_o[0].astype(bf16), mod[0], ln_g[0, 0:1], ln_b[0, 0:1],
                                     wr_hl, rb_p, rows)
    x_all = _moe_combine(x1, hp, e_idx, gate, 0, wg_bf, wu_bf, wd_bf, mod[0], ln_g[0, 1:2], ln_b[0, 1:2],
                         rows.seg)

    x1, hp, e_idx, gate = _pool_post(x_all, w_pool[0].astype(bf16), pool_scale[0:1], mod[1],
                                     ln_g[1, 0:1], ln_b[1, 0:1], wr_hl, rb_p, rows)
    out = _moe_combine(x1, hp, e_idx, gate, 1, wg_bf, wu_bf, wd_bf, mod[1], ln_g[1, 1:2], ln_b[1, 1:2],
                       lambda i: 1 + i // rows.n_seq_tiles)
    return out.reshape(n_batch, seq_len, d)
```

```python
import functools

import jax
import jax.numpy as jnp
from jax import lax
from jax.experimental import pallas as pl
from jax.experimental.pallas import tpu as pltpu

N_HEADS = 16
N_KV_HEADS = 4
HEAD_DIM = 128
Q_PER_KV = N_HEADS // N_KV_HEADS
GRID_W = 64
ROPE_THETA = 10000.0
ROPE_PAIRS = HEAD_DIM // 4
ATTN_SCALE = HEAD_DIM ** -0.5
Q_SCALE = ATTN_SCALE * 1.4426950408889634
POOL_WINDOWS = (2, 4, 8, 16)
POOL_HALO = 8
N_EXPERTS = 32
N_EXPERT_GROUPS = 8
EXPERTS_PER_GROUP = N_EXPERTS // N_EXPERT_GROUPS
DEPTH = 2
DEEPNORM_ALPHA = (2 * DEPTH) ** 0.25
LN_EPS = 1e-6
RMS_EPS = 1e-6

LANES = 128
SUBLANES = 8
ROW_TILE = 256
ATTN_TK = 2048
ATTN_ROWS = 128
SCORE_BOUND_SLACK = 1.02
SCORE_BOUND_MAX = 56.0
MOE_BLOCK = 256
GATE_COLS = 8
MOD_ROWS = 8
MOD_TN = 512
VMEM_LIMIT = 56 * 1024 * 1024

f32 = jnp.float32
bf16 = jnp.bfloat16
u32 = jnp.uint32
i32 = jnp.int32


def _cparams(*sem, flags=None):
    return pltpu.CompilerParams(dimension_semantics=sem, vmem_limit_bytes=VMEM_LIMIT, flags=flags)


def _nt_dot(a, b):
    return lax.dot_general(a, b, (((1,), (1,)), ((), ())), preferred_element_type=f32)


def _mod_kernel(c_ref, w_ref, b_ref, o_ref, s_sc, *, n_rows):
    d = w_ref.shape[1]
    tn = w_ref.shape[2]
    nj = tn // LANES

    @pl.when((pl.program_id(0) == 0) & (pl.program_id(1) == 0))
    def _():
        c = c_ref[...]
        s_sc[...] = c * jax.nn.sigmoid(c)

    def body(k, accs):
        r0 = pl.multiple_of(k * SUBLANES, SUBLANES)
        w8 = w_ref[0, pl.ds(r0, SUBLANES), :]
        out = []
        for r in range(n_rows):
            s8 = s_sc[r, pl.ds(r0, SUBLANES), :]
            for j in range(nj):
                out.append(accs[r * nj + j] + s8 * w8[:, j * LANES:(j + 1) * LANES])
        return tuple(out)

    init = tuple(jnp.zeros((SUBLANES, LANES), f32) for _ in range(n_rows * nj))
    accs = lax.fori_loop(0, d // SUBLANES, body, init, unroll=4)
    rows = []
    for r in range(n_rows):
        row = jnp.concatenate(
            [jnp.sum(accs[r * nj + j], axis=0, keepdims=True) for j in range(nj)], axis=1)
        rows.append(row + b_ref[0])
    rows.append(jnp.zeros((MOD_ROWS - n_rows, tn), f32))
    o_ref[0] = jnp.concatenate(rows, axis=0)


def _modulation(cond, w_mod, b_mod):
    n_rows, d = cond.shape
    depth, _, n6 = w_mod.shape
    c_b = jnp.broadcast_to(cond[:, :, None], (n_rows, d, LANES))
    return pl.pallas_call(
        functools.partial(_mod_kernel, n_rows=n_rows),
        out_shape=jax.ShapeDtypeStruct((depth, MOD_ROWS, n6), f32),
        grid=(depth, n6 // MOD_TN),
        in_specs=[
            pl.BlockSpec((n_rows, d, LANES), lambda l, j: (0, 0, 0)),
            pl.BlockSpec((1, d, MOD_TN), lambda l, j: (l, 0, j)),
            pl.BlockSpec((1, 1, MOD_TN), lambda l, j: (l, 0, j)),
        ],
        out_specs=pl.BlockSpec((1, MOD_ROWS, MOD_TN), lambda l, j: (l, 0, j)),
        scratch_shapes=[pltpu.VMEM((n_rows, d, LANES), f32)],
        compiler_params=_cparams("arbitrary", "arbitrary"),
        name="adaln_mod",
    )(c_b, w_mod, b_mod.reshape(depth, 1, n6))


class _Rows:
    def __init__(self, n_batch, ctx_len, seq_len):
        self.n_batch = n_batch
        self.ctx_len = ctx_len
        self.seq_len = seq_len
        self.n_ctx_tiles = ctx_len // ROW_TILE
        self.n_seq_tiles = seq_len // ROW_TILE
        self.tiles_per_batch = self.n_ctx_tiles + self.n_seq_tiles
        self.n_tiles = n_batch * self.tiles_per_batch
        self.t_all = self.n_tiles * ROW_TILE

    def is_ctx(self, i):
        return i % self.tiles_per_batch < self.n_ctx_tiles

    def seg(self, i):
        return jnp.where(self.is_ctx(i), 0, 1 + i // self.tiles_per_batch)

    def ctx_tile(self, i):
        r = i % self.tiles_per_batch
        return (i // self.tiles_per_batch) * self.n_ctx_tiles + jnp.minimum(r, self.n_ctx_tiles - 1)

    def lat_tile(self, i):
        r = i % self.tiles_per_batch
        return (i // self.tiles_per_batch) * self.n_seq_tiles + jnp.maximum(r - self.n_ctx_tiles, 0)


def _qkv_kernel(xc_ref, xl_ref, mod_ref, w_ref, qg_ref, kg_ref, cos_ref, sin_ref, q_ref, k_ref, v_ref, *, rows):
    tm, d = xl_ref.shape
    kv_dim = k_ref.shape[1]
    x = jnp.where(rows.is_ctx(pl.program_id(0)), xc_ref[...], xl_ref[...])
    sh1 = mod_ref[0, 0:1, :]
    sc1 = mod_ref[0, 1:2, :]
    u = (x * (1.0 + sc1) + sh1).astype(bf16)
    qkv = jnp.dot(u, w_ref[...], preferred_element_type=f32)
    cos = cos_ref[...]
    sin = sin_ref[...]
    lane = lax.broadcasted_iota(i32, (tm, HEAD_DIM), 1)
    first = (lane % (2 * ROPE_PAIRS)) < ROPE_PAIRS

    def norm_rope(xh, gain, scale):
        ms = jnp.mean(xh * xh, axis=-1, keepdims=True)
        xn = xh * lax.rsqrt(ms + RMS_EPS) * gain
        rot = jnp.where(first, pltpu.roll(xn, HEAD_DIM - ROPE_PAIRS, 1), pltpu.roll(xn, ROPE_PAIRS, 1))
        y = xn * cos + rot * sin
        return y * scale if scale != 1.0 else y

    qg = qg_ref[...]
    kg = kg_ref[...]
    for h in range(N_HEADS):
        sl = slice(h * HEAD_DIM, (h + 1) * HEAD_DIM)
        q_ref[:, sl] = norm_rope(qkv[:, sl], qg, Q_SCALE).astype(bf16)
    ones = jnp.ones((tm, HEAD_DIM), bf16)
    for h in range(N_KV_HEADS):
        sl = slice(h * HEAD_DIM, (h + 1) * HEAD_DIM)
        k_ref[:, sl] = norm_rope(qkv[:, d + h * HEAD_DIM:d + (h + 1) * HEAD_DIM], kg, 1.0).astype(bf16)
        v0 = d + kv_dim + h * HEAD_DIM
        v_ref[:, 2 * h * HEAD_DIM:(2 * h + 1) * HEAD_DIM] = qkv[:, v0:v0 + HEAD_DIM].astype(bf16)
        v_ref[:, (2 * h + 1) * HEAD_DIM:(2 * h + 2) * HEAD_DIM] = ones


def _qkv_proj(ctx2, x2, mod_l, w_qkv_bf, q_gain, k_gain, cos_t, sin_t, rows):
    d = x2.shape[1]
    kv_dim = N_KV_HEADS * HEAD_DIM
    tpb = rows.tiles_per_batch
    return pl.pallas_call(
        functools.partial(_qkv_kernel, rows=rows),
        out_shape=(jax.ShapeDtypeStruct((rows.t_all, d), bf16),
                   jax.ShapeDtypeStruct((rows.t_all, kv_dim), bf16),
                   jax.ShapeDtypeStruct((rows.t_all, 2 * kv_dim), bf16)),
        grid=(rows.n_tiles,),
        in_specs=[
            pl.BlockSpec((ROW_TILE, d), lambda i: (rows.ctx_tile(i), 0)),
            pl.BlockSpec((ROW_TILE, d), lambda i: (rows.lat_tile(i), 0)),
            pl.BlockSpec((1, 6, d), lambda i: (rows.seg(i), 0, 0)),
            pl.BlockSpec((d, d + 2 * kv_dim), lambda i: (0, 0)),
            pl.BlockSpec((1, HEAD_DIM), lambda i: (0, 0)),
            pl.BlockSpec((1, HEAD_DIM), lambda i: (0, 0)),
            pl.BlockSpec((ROW_TILE, HEAD_DIM), lambda i: (i % tpb, 0)),
            pl.BlockSpec((ROW_TILE, HEAD_DIM), lambda i: (i % tpb, 0)),
        ],
        out_specs=(pl.BlockSpec((ROW_TILE, d), lambda i: (i, 0)),
                   pl.BlockSpec((ROW_TILE, kv_dim), lambda i: (i, 0)),
                   pl.BlockSpec((ROW_TILE, 2 * kv_dim), lambda i: (i, 0))),
        compiler_params=_cparams("arbitrary"),
        name="qkv_rope",
    )(ctx2, x2, mod_l, w_qkv_bf, q_gain, k_gain, cos_t, sin_t)


def _attn_kernel(bound_ref, q_ref, k_ref, v_ref, wg_ref, wu_ref, wd_ref, o_ref, wgo_ref, wuo_ref, wdo_ref,
                 *, ctx_len, tk, rs, n_ctx_tiles, bounded):
    tq = q_ref.shape[0]
    n_lat_chunks = (k_ref.shape[0] - ctx_len) // tk
    qi = pl.program_id(2)
    subs = [(h, r0) for h in range(Q_PER_KV) for r0 in range(0, tq, rs)]

    def attend(chunks):
        wgo_ref[...] = wg_ref[...].astype(bf16)
        wuo_ref[...] = wu_ref[...].astype(bf16)
        wdo_ref[...] = wd_ref[...].astype(bf16)
        states = [None] * len(subs)
        for off, size in chunks:
            kc = k_ref[off:off + size, :]
            vc = v_ref[off:off + size, :]
            for i, (h, r0) in enumerate(subs):
                s = _nt_dot(q_ref[r0:r0 + rs, h * HEAD_DIM:(h + 1) * HEAD_DIM], kc)
                if bounded:
                    pv = jnp.dot(jnp.exp2(s - bound_ref[0, 0]).astype(bf16), vc, preferred_element_type=f32)
                    states[i] = pv if states[i] is None else states[i] + pv
                    continue
                mx = s[:, 0:LANES]
                for j in range(1, size // LANES):
                    mx = jnp.maximum(mx, s[:, j * LANES:(j + 1) * LANES])
                m_cur = jnp.max(mx, axis=-1, keepdims=True)
                if states[i] is None:
                    m_new = jnp.broadcast_to(m_cur, (rs, LANES))
                else:
                    m_prev, acc_prev = states[i]
                    m_new = jnp.maximum(m_prev, m_cur)
                p = jnp.concatenate(
                    [jnp.exp2(s[:, j * LANES:(j + 1) * LANES] - m_new) for j in range(size // LANES)],
                    axis=1).astype(bf16)
                acc = jnp.dot(p, vc, preferred_element_type=f32)
                if states[i] is not None:
                    alpha = jnp.exp2(m_prev - m_new)
                    acc = acc_prev * jnp.concatenate([alpha, alpha], axis=1) + acc
                states[i] = (m_new, acc)
        for i, (h, r0) in enumerate(subs):
            acc = states[i] if bounded else states[i][1]
            o_ref[r0:r0 + rs, h * HEAD_DIM:(h + 1) * HEAD_DIM] = (
                acc[:, :HEAD_DIM] / acc[:, HEAD_DIM:]).astype(bf16)

    @pl.when(qi < n_ctx_tiles)
    def _():
        attend([(0, ctx_len)])

    @pl.when(qi >= n_ctx_tiles)
    def _():
        attend([(0, ctx_len + tk)] + [(ctx_len + c * tk, tk) for c in range(1, n_lat_chunks)])


def _attention(bound, q, k, v, w_gate, w_up, w_down, rows, bounded):
    t_all, d = q.shape
    tpb = rows.tiles_per_batch
    rows_per_batch = t_all // rows.n_batch
    gw = Q_PER_KV * HEAD_DIM
    tk = min(ATTN_TK, rows.seq_len)
    assert rows.seq_len % tk == 0

    n_steps = rows.n_batch * N_KV_HEADS * tpb
    wg2 = w_gate.reshape(-1, w_gate.shape[-1])
    wu2 = w_up.reshape(-1, w_up.shape[-1])
    wd2 = w_down.reshape(-1, w_down.shape[-1])
    n_slabs = 1 << (n_steps.bit_length() - 1)
    slab_g = wg2.shape[0] // n_slabs
    slab_d = wd2.shape[0] // n_slabs
    assert wg2.shape[0] == n_slabs * slab_g and wd2.shape[0] == n_slabs * slab_d and wu2.shape == wg2.shape
    assert slab_g % SUBLANES == 0 and slab_d % SUBLANES == 0

    def slab(b, g, i):
        return (jnp.minimum((b * N_KV_HEADS + g) * tpb + i, n_slabs - 1), 0)

    kern = functools.partial(_attn_kernel, ctx_len=rows.ctx_len, tk=tk, rs=ATTN_ROWS,
                             n_ctx_tiles=rows.n_ctx_tiles, bounded=bounded)
    o, wg_bf, wu_bf, wd_bf = pl.pallas_call(
        kern,
        out_shape=(jax.ShapeDtypeStruct((t_all, d), bf16),
                   jax.ShapeDtypeStruct(wg2.shape, bf16),
                   jax.ShapeDtypeStruct(wu2.shape, bf16),
                   jax.ShapeDtypeStruct(wd2.shape, bf16)),
        grid=(rows.n_batch, N_KV_HEADS, tpb),
        in_specs=[
            pl.BlockSpec((1, 1), lambda b, g, i: (0, 0), memory_space=pltpu.SMEM),
            pl.BlockSpec((ROW_TILE, gw), lambda b, g, i: (b * tpb + i, g)),
            pl.BlockSpec((rows_per_batch, HEAD_DIM), lambda b, g, i: (b, g)),
            pl.BlockSpec((rows_per_batch, 2 * HEAD_DIM), lambda b, g, i: (b, g)),
            pl.BlockSpec((slab_g, wg2.shape[1]), slab),
            pl.BlockSpec((slab_g, wu2.shape[1]), slab),
            pl.BlockSpec((slab_d, wd2.shape[1]), slab),
        ],
        out_specs=(pl.BlockSpec((ROW_TILE, gw), lambda b, g, i: (b * tpb + i, g)),
                   pl.BlockSpec((slab_g, wg2.shape[1]), slab),
                   pl.BlockSpec((slab_g, wu2.shape[1]), slab),
                   pl.BlockSpec((slab_d, wd2.shape[1]), slab)),
        compiler_params=_cparams("arbitrary", "arbitrary", "arbitrary"),
        name="gqa_attention_bounded" if bounded else "gqa_attention_online",
    )(bound, q, k, v, wg2, wu2, wd2)
    return o, wg_bf.reshape(w_gate.shape), wu_bf.reshape(w_up.shape), wd_bf.reshape(w_down.shape)


def _post_epilogue(x, y, mod_ref, lng_ref, lnb_ref, wr_ref, rb_ref, x1_ref, hp_ref, e_ref, g_ref):
    tm, d = x.shape
    g1 = mod_ref[0, 2:3, :]
    sh2 = mod_ref[0, 3:4, :]
    sc2 = mod_ref[0, 4:5, :]
    z = DEEPNORM_ALPHA * x + g1 * y
    mu = jnp.mean(z, axis=-1, keepdims=True)
    zc = z - mu
    var = jnp.mean(zc * zc, axis=-1, keepdims=True)
    x1 = zc * lax.rsqrt(var + LN_EPS) * lng_ref[...] + lnb_ref[...]
    h = x1 * (1.0 + sc2) + sh2
    x1_ref[...] = x1

    h_hi = h.astype(bf16)
    h_hi32 = h_hi.astype(f32)
    bits = lax.bitcast_convert_type(h_hi32, u32)
    hp_ref[...] = (bits[:, :d // 2] >> 16) | bits[:, d // 2:]

    h_lo = (h - h_hi32).astype(bf16)
    wr = wr_ref[...]
    a2 = _nt_dot(wr, h_hi)
    lt = a2[:N_EXPERTS] + a2[N_EXPERTS:] + _nt_dot(wr[:N_EXPERTS], h_lo)
    aff = jax.nn.sigmoid(lt)
    sel = aff + rb_ref[...]
    ng = N_EXPERT_GROUPS
    a = [aff[j * ng:(j + 1) * ng] for j in range(EXPERTS_PER_GROUP)]
    s = [sel[j * ng:(j + 1) * ng] for j in range(EXPERTS_PER_GROUP)]
    gs = None
    for i in range(EXPERTS_PER_GROUP):
        for j in range(i + 1, EXPERTS_PER_GROUP):
            pij = s[i] + s[j]
            gs = pij if gs is None else jnp.maximum(gs, pij)
    gio = lax.broadcasted_iota(i32, gs.shape, 0)
    gmax = jnp.max(gs, axis=0, keepdims=True)
    g_idx = jnp.min(jnp.where(gs == gmax, gio, ng), axis=0, keepdims=True)
    onehot = gio == g_idx
    sin_g = [jnp.sum(jnp.where(onehot, s[j], 0.0), axis=0, keepdims=True) for j in range(EXPERTS_PER_GROUP)]
    aff_g = [jnp.sum(jnp.where(onehot, a[j], 0.0), axis=0, keepdims=True) for j in range(EXPERTS_PER_GROUP)]
    loc = [jnp.zeros_like(g_idx), jnp.zeros_like(g_idx)]
    gat = [jnp.zeros_like(gmax), jnp.zeros_like(gmax)]
    for j in range(EXPERTS_PER_GROUP):
        rank = jnp.zeros_like(g_idx)
        for i in range(EXPERTS_PER_GROUP):
            if i == j:
                continue
            ahead = (sin_g[i] >= sin_g[j]) if i < j else (sin_g[i] > sin_g[j])
            rank = rank + ahead.astype(i32)
        for kk in range(2):
            hit = rank == kk
            loc[kk] = jnp.where(hit, j, loc[kk])
            gat[kk] = jnp.where(hit, aff_g[j], gat[kk])
    den = gat[0] + gat[1]
    e_ref[0] = jnp.concatenate(
        [g_idx * EXPERTS_PER_GROUP + loc[0], g_idx * EXPERTS_PER_GROUP + loc[1]], axis=0)
    g8 = jnp.concatenate([gat[0] / den, gat[1] / den, jnp.zeros((GATE_COLS - 2, tm), f32)], axis=0)
    g_ref[...] = g8.T


def _attn_post_kernel(xc_ref, xl_ref, o_ref, wo_ref, mod_ref, lng_ref, lnb_ref, wr_ref, rb_ref,
                      x1_ref, hp_ref, e_ref, g_ref, *, rows):
    x = jnp.where(rows.is_ctx(pl.program_id(0)), xc_ref[...], xl_ref[...])
    y = jnp.dot(o_ref[...], wo_ref[...], preferred_element_type=f32)
    _post_epilogue(x, y, mod_ref, lng_ref, lnb_ref, wr_ref, rb_ref, x1_ref, hp_ref, e_ref, g_ref)


def _post_out(n_rows, d):
    shapes = (jax.ShapeDtypeStruct((n_rows, d), f32),
              jax.ShapeDtypeStruct((n_rows, d // 2), u32),
              jax.ShapeDtypeStruct((n_rows // ROW_TILE, 2, ROW_TILE), i32),
              jax.ShapeDtypeStruct((n_rows, GATE_COLS), f32))
    specs = (pl.BlockSpec((ROW_TILE, d), lambda i: (i, 0)),
             pl.BlockSpec((ROW_TILE, d // 2), lambda i: (i, 0)),
             pl.BlockSpec((1, 2, ROW_TILE), lambda i: (i, 0, 0)),
             pl.BlockSpec((ROW_TILE, GATE_COLS), lambda i: (i, 0)))
    return shapes, specs


def _router_specs(d):
    return [pl.BlockSpec((2 * N_EXPERTS, d), lambda i: (0, 0)),
            pl.BlockSpec((N_EXPERTS, 1), lambda i: (0, 0))]


def _attn_post(ctx2, x2, o, w_o_bf, mod_l, ln_g, ln_b, wr_hl, rb_p, rows):
    d = x2.shape[1]
    shapes, specs = _post_out(rows.t_all, d)
    return pl.pallas_call(
        functools.partial(_attn_post_kernel, rows=rows),
        out_shape=shapes,
        grid=(rows.n_tiles,),
        in_specs=[
            pl.BlockSpec((ROW_TILE, d), lambda i: (rows.ctx_tile(i), 0)),
            pl.BlockSpec((ROW_TILE, d), lambda i: (rows.lat_tile(i), 0)),
            pl.BlockSpec((ROW_TILE, d), lambda i: (i, 0)),
            pl.BlockSpec((d, d), lambda i: (0, 0)),
            pl.BlockSpec((1, 6, d), lambda i: (rows.seg(i), 0, 0)),
            pl.BlockSpec((1, d), lambda i: (0, 0)),
            pl.BlockSpec((1, d), lambda i: (0, 0)),
        ] + _router_specs(d),
        out_specs=specs,
        compiler_params=_cparams("arbitrary"),
        name="attn_out_norm_route",
    )(ctx2, x2, o, w_o_bf, mod_l, ln_g, ln_b, wr_hl, rb_p)


def _pool_post_kernel(x_ref, xp_ref, xn_ref, wp_ref, ps_ref, mod_ref, lng_ref, lnb_ref, wr_ref, rb_ref,
                      x1_ref, hp_ref, e_ref, g_ref, *, seq_len, tiles_per_seq):
    tm, d = x_ref.shape
    gd = d // len(POOL_WINDOWS)
    r = pl.program_id(0) % tiles_per_seq
    sh1 = mod_ref[0, 0:1, :]
    sc1 = mod_ref[0, 1:2, :]
    x = x_ref[...]
    ext = jnp.concatenate([xp_ref[...], x, xn_ref[...]], axis=0)
    n_ext = tm + 2 * POOL_HALO
    pos = r * tm - POOL_HALO + lax.broadcasted_iota(i32, (n_ext, 1), 0)
    inside = (pos >= 0) & (pos < seq_len)
    u_ext = jnp.where(inside, ext * (1.0 + sc1) + sh1, 0.0)
    t = pos[POOL_HALO:POOL_HALO + tm]
    ys = []
    for g, w in enumerate(POOL_WINDOWS):
        e = u_ext[:, g * gd:(g + 1) * gd]
        p = e + pltpu.roll(e, 1, 0)
        step = 1
        while 2 * step < w:
            p = pltpu.roll(p, step, 0) + pltpu.roll(p, n_ext - step, 0)
            step *= 2
        cnt = (jnp.clip(t + w // 2, 0, seq_len) - jnp.clip(t - w // 2, 0, seq_len)).astype(f32)
        mean = p[POOL_HALO:POOL_HALO + tm] / cnt
        dlt = (mean - e[POOL_HALO:POOL_HALO + tm]).astype(bf16)
        ys.append(jnp.dot(dlt, wp_ref[g], preferred_element_type=f32))
    y = jnp.concatenate(ys, axis=-1) * ps_ref[...]
    _post_epilogue(x, y, mod_ref, lng_ref, lnb_ref, wr_ref, rb_ref, x1_ref, hp_ref, e_ref, g_ref)


def _pool_post(x_all, w_pool_bf, pool_scale, mod_l, ln_g, ln_b, wr_hl, rb_p, rows):
    t_all, d = x_all.shape
    tiles_per_seq = rows.n_seq_tiles
    n_rows = rows.n_batch * rows.seq_len
    halo_per_tile = ROW_TILE // POOL_HALO
    last_halo = t_all // POOL_HALO - 1
    ng = len(POOL_WINDOWS)

    def src_tile(i):
        return (i // tiles_per_seq) * rows.tiles_per_batch + rows.n_ctx_tiles + i % tiles_per_seq

    shapes, specs = _post_out(n_rows, d)
    kern = functools.partial(_pool_post_kernel, seq_len=rows.seq_len, tiles_per_seq=tiles_per_seq)
    return pl.pallas_call(
        kern,
        out_shape=shapes,
        grid=(n_rows // ROW_TILE,),
        in_specs=[
            pl.BlockSpec((ROW_TILE, d), lambda i: (src_tile(i), 0)),
            pl.BlockSpec((POOL_HALO, d), lambda i: (jnp.maximum(src_tile(i) * halo_per_tile - 1, 0), 0)),
            pl.BlockSpec((POOL_HALO, d), lambda i: (jnp.minimum((src_tile(i) + 1) * halo_per_tile, last_halo), 0)),
            pl.BlockSpec((ng, d // ng, d // ng), lambda i: (0, 0, 0)),
            pl.BlockSpec((1, d), lambda i: (0, 0)),
            pl.BlockSpec((1, 6, d), lambda i: (1 + i // tiles_per_seq, 0, 0)),
            pl.BlockSpec((1, d), lambda i: (0, 0)),
            pl.BlockSpec((1, d), lambda i: (0, 0)),
        ] + _router_specs(d),
        out_specs=specs,
        compiler_params=_cparams("arbitrary"),
        name="pool_norm_route",
    )(x_all, x_all, x_all, w_pool_bf, pool_scale, mod_l, ln_g, ln_b, wr_hl, rb_p)


def _plan_kernel(e_ref, dest_ref, be_ref, nu_ref, carry_sc, start_sc):
    phase = pl.program_id(0)
    i = pl.program_id(1)
    tm = e_ref.shape[2]
    sub = lax.broadcasted_iota(i32, (N_EXPERTS, tm), 0)
    shift = MOE_BLOCK.bit_length() - 1

    @pl.when((phase == 0) & (i == 0))
    def _():
        carry_sc[...] = jnp.zeros(carry_sc.shape, f32)

    @pl.when(phase == 0)
    def _():
        carry = carry_sc[...]
        for kk in range(2):
            hit = sub == e_ref[0, kk:kk + 1, :]
            carry = carry + jnp.sum(jnp.where(hit, 1.0, 0.0), axis=1, keepdims=True)
        carry_sc[...] = carry

    @pl.when((phase == 1) & (i == 0))
    def _():
        counts = carry_sc[...].astype(i32)
        padded = ((counts + (MOE_BLOCK - 1)) >> shift) << shift
        esub = lax.broadcasted_iota(i32, padded.shape, 0)
        start = jnp.zeros(padded.shape, i32)
        for e in range(N_EXPERTS - 1):
            start = start + jnp.where(esub > e, padded[e:e + 1, :], 0)
        ends = start + padded
        nb_pad = be_ref.shape[1]
        blk_start = lax.broadcasted_iota(i32, (1, nb_pad), 1) * MOE_BLOCK
        be = jnp.zeros((1, nb_pad), i32)
        for e in range(N_EXPERTS):
            end_e = jnp.concatenate([ends[e:e + 1, :]] * (nb_pad // LANES), axis=1)
            be = be + jnp.where(end_e <= blk_start, 1, 0)
        be_ref[...] = jnp.minimum(be, N_EXPERTS - 1)
        nu_ref[...] = ends[N_EXPERTS - 1:N_EXPERTS, :] >> shift
        start_sc[...] = start.astype(f32)
        carry_sc[...] = jnp.zeros(carry_sc.shape, f32)

    @pl.when(phase == 1)
    def _():
        before = lax.broadcasted_iota(i32, (tm, tm), 0) < lax.broadcasted_iota(i32, (tm, tm), 1)
        tri = jnp.where(before, 1.0, 0.0).astype(bf16)
        carry = carry_sc[...]
        start = start_sc[...]
        out = []
        for kk in range(2):
            hit = sub == e_ref[0, kk:kk + 1, :]
            onehot = jnp.where(hit, 1.0, 0.0)
            cum = jnp.dot(onehot.astype(bf16), tri, preferred_element_type=f32)
            base = (carry + start)[:, 0:1]
            out.append(jnp.sum(jnp.where(hit, cum + base, 0.0), axis=0, keepdims=True))
            carry = carry + jnp.sum(onehot, axis=1, keepdims=True)
        dest_ref[0] = jnp.concatenate(out, axis=0).astype(i32)
        carry_sc[...] = carry


def _dispatch_plan(e3):
    n_tiles, _, tm = e3.shape
    n_blocks = (2 * n_tiles * tm + N_EXPERTS * (MOE_BLOCK - 1) + MOE_BLOCK - 1) // MOE_BLOCK
    nb_pad = (n_blocks + LANES - 1) // LANES * LANES
    dest3, be, nu = pl.pallas_call(
        _plan_kernel,
        out_shape=(jax.ShapeDtypeStruct(e3.shape, i32),
                   jax.ShapeDtypeStruct((1, nb_pad), i32),
                   jax.ShapeDtypeStruct((1, LANES), i32)),
        grid=(2, n_tiles),
        in_specs=[pl.BlockSpec((1, 2, tm), lambda p, i: (i, 0, 0))],
        out_specs=(pl.BlockSpec((1, 2, tm), lambda p, i: (p * i, 0, 0)),
                   pl.BlockSpec((1, nb_pad), lambda p, i: (0, 0)),
                   pl.BlockSpec((1, LANES), lambda p, i: (0, 0))),
        scratch_shapes=[pltpu.VMEM((N_EXPERTS, LANES), f32), pltpu.VMEM((N_EXPERTS, LANES), f32)],
        compiler_params=_cparams("arbitrary", "arbitrary"),
        name="route_plan",
    )(e3)
    return be[0, :n_blocks], nu[0, :1], dest3, n_blocks


def _dispatch_kernel(dest_ref, hp_ref, xs_in, xs_hbm, buf, sem):
    del xs_in
    i = pl.program_id(0)
    n = pl.num_programs(0)
    tm = hp_ref.shape[0]
    slot = i % 2
    buf[slot] = hp_ref[...]
    for kk in range(2):
        for r in range(tm):
            pltpu.make_async_copy(
                buf.at[slot, pl.ds(r, 1), :], xs_hbm.at[pl.ds(dest_ref[0, kk, r], 1), :], sem.at[slot]).start()

    def drain(s):
        for _ in range(2 * tm):
            pltpu.make_async_copy(buf.at[s, pl.ds(0, 1), :], xs_hbm.at[pl.ds(0, 1), :], sem.at[s]).wait()

    @pl.when(i > 0)
    def _():
        drain(1 - slot)

    @pl.when(i == n - 1)
    def _():
        drain(slot)


def _dispatch(hp, dest3, n_blocks):
    t, dw = hp.shape
    n_tiles = t // ROW_TILE
    xs0 = jnp.zeros((n_blocks * MOE_BLOCK, dw), u32)
    return pl.pallas_call(
        _dispatch_kernel,
        out_shape=jax.ShapeDtypeStruct(xs0.shape, u32),
        grid=(n_tiles,),
        in_specs=[
            pl.BlockSpec((1, 2, ROW_TILE), lambda i: (i, 0, 0), memory_space=pltpu.SMEM),
            pl.BlockSpec((ROW_TILE, dw), lambda i: (i, 0)),
            pl.BlockSpec(memory_space=pl.ANY),
        ],
        out_specs=pl.BlockSpec(memory_space=pl.ANY),
        scratch_shapes=[pltpu.VMEM((2, ROW_TILE, dw), u32), pltpu.SemaphoreType.DMA((2,))],
        input_output_aliases={2: 0},
        compiler_params=_cparams("arbitrary"),
        name="moe_dispatch",
    )(dest3, hp, xs0)


def _ffn_kernel(be_ref, nu_ref, xs_ref, wg_ref, wu_ref, wd_ref, y_ref):
    b = pl.program_id(0)

    @pl.when(b < nu_ref[0])
    def _():
        w = xs_ref[...]
        lo = lax.bitcast_convert_type(w << 16, f32)
        hi = lax.bitcast_convert_type(w & jnp.uint32(0xFFFF0000), f32)
        xb = jnp.concatenate([lo, hi], axis=1).astype(bf16)
        a = jnp.dot(xb, wg_ref[0, 0], preferred_element_type=f32)
        u = jnp.dot(xb, wu_ref[0, 0], preferred_element_type=f32)
        z = (a * jax.nn.sigmoid(a) * u).astype(bf16)
        y_ref[...] = jnp.dot(z, wd_ref[0, 0], preferred_element_type=f32)

    @pl.when(b >= nu_ref[0])
    def _():
        y_ref[...] = jnp.zeros(y_ref.shape, f32)


def _expert_ffn(xs, blk_expert, n_used, layer, wg, wu, wd):
    slots, dw = xs.shape
    d = 2 * dw
    de = wg.shape[3]
    n_blocks = slots // MOE_BLOCK
    grid_spec = pltpu.PrefetchScalarGridSpec(
        num_scalar_prefetch=2,
        grid=(n_blocks,),
        in_specs=[
            pl.BlockSpec((MOE_BLOCK, dw), lambda b, be, nu: (b, 0)),
            pl.BlockSpec((1, 1, d, de), lambda b, be, nu: (layer, be[b], 0, 0)),
            pl.BlockSpec((1, 1, d, de), lambda b, be, nu: (layer, be[b], 0, 0)),
            pl.BlockSpec((1, 1, de, d), lambda b, be, nu: (layer, be[b], 0, 0)),
        ],
        out_specs=pl.BlockSpec((MOE_BLOCK, d), lambda b, be, nu: (b, 0)),
    )
    return pl.pallas_call(
        _ffn_kernel,
        out_shape=jax.ShapeDtypeStruct((slots, d), f32),
        grid_spec=grid_spec,
        compiler_params=_cparams("arbitrary"),
        name="expert_ffn",
    )(blk_expert, n_used, xs, wg, wu, wd)


def _fin_kernel(pos_cur, pos_nxt, x1_ref, gate_ref, mod_ref, lng_ref, lnb_ref, y_hbm, o_ref, ybuf, sem):
    i = pl.program_id(0)
    n = pl.num_programs(0)
    tm = x1_ref.shape[0]
    slot = i % 2

    def copy(pos_ref, kk, r, s):
        return pltpu.make_async_copy(
            y_hbm.at[pl.ds(pos_ref[0, kk, r], 1), :], ybuf.at[s, kk, pl.ds(r, 1), :], sem.at[s])

    def start_gather(pos_ref, s):
        for kk in range(2):
            for r in range(tm):
                copy(pos_ref, kk, r, s).start()

    @pl.when(i == 0)
    def _():
        start_gather(pos_cur, 0)

    @pl.when(i + 1 < n)
    def _():
        start_gather(pos_nxt, 1 - slot)

    for kk in range(2):
        for r in range(tm):
            copy(pos_cur, kk, r, slot).wait()
    g2 = mod_ref[0, 5:6, :]
    gate = gate_ref[...]
    f = gate[:, 0:1] * ybuf[slot, 0] + gate[:, 1:2] * ybuf[slot, 1]
    z = DEEPNORM_ALPHA * x1_ref[...] + g2 * f
    mu = jnp.mean(z, axis=-1, keepdims=True)
    zc = z - mu
    var = jnp.mean(zc * zc, axis=-1, keepdims=True)
    o_ref[...] = zc * lax.rsqrt(var + LN_EPS) * lng_ref[...] + lnb_ref[...]


def _combine_norm(x1, gate, pos3, y_sorted, mod_l, ln_g, ln_b, seg_of_tile):
    t, d = x1.shape
    n_tiles = t // ROW_TILE
    return pl.pallas_call(
        _fin_kernel,
        out_shape=jax.ShapeDtypeStruct((t, d), f32),
        grid=(n_tiles,),
        in_specs=[
            pl.BlockSpec((1, 2, ROW_TILE), lambda i: (i, 0, 0), memory_space=pltpu.SMEM),
            pl.BlockSpec((1, 2, ROW_TILE), lambda i: (jnp.minimum(i + 1, n_tiles - 1), 0, 0),
                         memory_space=pltpu.SMEM),
            pl.BlockSpec((ROW_TILE, d), lambda i: (i, 0)),
            pl.BlockSpec((ROW_TILE, GATE_COLS), lambda i: (i, 0)),
            pl.BlockSpec((1, 6, d), lambda i: (seg_of_tile(i), 0, 0)),
            pl.BlockSpec((1, d), lambda i: (0, 0)),
            pl.BlockSpec((1, d), lambda i: (0, 0)),
            pl.BlockSpec(memory_space=pl.ANY),
        ],
        out_specs=pl.BlockSpec((ROW_TILE, d), lambda i: (i, 0)),
        scratch_shapes=[pltpu.VMEM((2, 2, ROW_TILE, d), f32), pltpu.SemaphoreType.DMA((2,))],
        compiler_params=_cparams("arbitrary"),
        name="combine_norm",
    )(pos3, pos3, x1, gate, mod_l, ln_g, ln_b, y_sorted)


def _moe_combine(x1, hp, e_idx, gate, layer, wg_bf, wu_bf, wd_bf, mod_l, ln_g, ln_b, seg_of_tile):
    blk_expert, n_used, dest3, n_blocks = _dispatch_plan(e_idx)
    xs = _dispatch(hp, dest3, n_blocks)
    y_sorted = _expert_ffn(xs, blk_expert, n_used, layer, wg_bf, wu_bf, wd_bf)
    return _combine_norm(x1, gate, dest3, y_sorted, mod_l, ln_g, ln_b, seg_of_tile)


def _rope_tables(ctx_len, seq_len):
    t = jnp.arange(seq_len)
    row = (t // GRID_W).astype(f32)
    col = (t % GRID_W).astype(f32)
    inv_freq = ROPE_THETA ** (-jnp.arange(ROPE_PAIRS, dtype=f32) / ROPE_PAIRS)
    ang_r = row[:, None] * inv_freq
    ang_c = col[:, None] * inv_freq
    ang = jnp.concatenate([ang_r, ang_r, ang_c, ang_c], axis=-1)
    cos = jnp.cos(ang)
    sin = jnp.sin(ang)
    lane = jnp.arange(HEAD_DIM)
    sign = jnp.where((lane % (2 * ROPE_PAIRS)) < ROPE_PAIRS, -1.0, 1.0).astype(f32)
    cos_t = jnp.concatenate([jnp.ones((ctx_len, HEAD_DIM), f32), cos], axis=0)
    sin_t = jnp.concatenate([jnp.zeros((ctx_len, HEAD_DIM), f32), sin * sign], axis=0)
    return cos_t, sin_t


def kernel(x, c, ctx, c_ctx, w_mod, b_mod, ln_g, ln_b, w_qkv, q_gain, k_gain, w_o, w_pool, pool_scale,
           w_router, router_bias, w_gate, w_up, w_down):
    n_batch, seq_len, d = x.shape
    ctx_len = ctx.shape[1]
    assert w_mod.shape[0] == DEPTH and d == N_HEADS * HEAD_DIM
    assert ctx_len % ROW_TILE == 0 and seq_len % ROW_TILE == 0 and 1 + n_batch <= MOD_ROWS
    rows = _Rows(n_batch, ctx_len, seq_len)

    cond = jnp.concatenate([c_ctx[None, :], c], axis=0)
    mod = _modulation(cond, w_mod, b_mod).reshape(DEPTH, MOD_ROWS, 6, d)

    wr_t = w_router.reshape(d, N_EXPERT_GROUPS, EXPERTS_PER_GROUP).transpose(2, 1, 0).reshape(N_EXPERTS, d)
    wr_hi = wr_t.astype(bf16)
    wr_hl = jnp.concatenate([wr_hi, (wr_t - wr_hi.astype(f32)).astype(bf16)], axis=0)
    rb_p = router_bias.reshape(N_EXPERT_GROUPS, EXPERTS_PER_GROUP).T.reshape(N_EXPERTS, 1)
    cos_t, sin_t = _rope_tables(ctx_len, seq_len)

    ctx2 = ctx.reshape(n_batch * ctx_len, d)
    x2 = x.reshape(n_batch * seq_len, d)
    q, k, v = _qkv_proj(ctx2, x2, mod[0], w_qkv[0].astype(bf16), q_gain[0:1], k_gain[0:1], cos_t, sin_t, rows)
    bound = (Q_SCALE * HEAD_DIM * SCORE_BOUND_SLACK) * jnp.max(jnp.abs(q_gain[0])) * jnp.max(jnp.abs(k_gain[0]))
    attn_args = (bound.reshape(1, 1), q, k, v, w_gate, w_up, w_down)
    o, wg_bf, wu_bf, wd_bf = lax.cond(
        bound <= SCORE_BOUND_MAX,
        lambda a: _attention(*a, rows, True),
        lambda a: _attention(*a, rows, False),
        attn_args)
    x1, hp, e_idx, gate = _attn_post(ctx2, x2, o, w_o[0].astype(bf16), mod[0], ln_g[0, 0:1], ln_b[0, 0:1],
                                     wr_hl, rb_p, rows)
    x_all = _moe_combine(x1, hp, e_idx, gate, 0, wg_bf, wu_bf, wd_bf, mod[0], ln_g[0, 1:2], ln_b[0, 1:2],
                         rows.seg)

    x1, hp, e_idx, gate = _pool_post(x_all, w_pool[0].astype(bf16), pool_scale[0:1], mod[1],
                                     ln_g[1, 0:1], ln_b[1, 0:1], wr_hl, rb_p, rows)
    out = _moe_combine(x1, hp, e_idx, gate, 1, wg_bf, wu_bf, wd_bf, mod[1], ln_g[1, 1:2], ln_b[1, 1:2],
                       lambda i: 1 + i // rows.n_seq_tiles)
    return out.reshape(n_batch, seq_len, d)
```

```python
import functools

import jax
import jax.numpy as jnp
from jax import lax
from jax.experimental import pallas as pl
from jax.experimental.pallas import tpu as pltpu

N_HEADS = 16
N_KV_HEADS = 4
HEAD_DIM = 128
Q_PER_KV = N_HEADS // N_KV_HEADS
GRID_W = 64
ROPE_THETA = 10000.0
ROPE_PAIRS = HEAD_DIM // 4
ATTN_SCALE = HEAD_DIM ** -0.5
Q_SCALE = ATTN_SCALE * 1.4426950408889634
POOL_WINDOWS = (2, 4, 8, 16)
POOL_HALO = 8
N_EXPERTS = 32
N_EXPERT_GROUPS = 8
EXPERTS_PER_GROUP = N_EXPERTS // N_EXPERT_GROUPS
DEPTH = 2
DEEPNORM_ALPHA = (2 * DEPTH) ** 0.25
LN_EPS = 1e-6
RMS_EPS = 1e-6

LANES = 128
SUBLANES = 8
ROW_TILE = 256
ATTN_TK = 2048
ATTN_ROWS = 128
SCORE_BOUND_SLACK = 1.02
SCORE_BOUND_MAX = 56.0
POST_ROWS = 128
MOE_BLOCK = 256
GATE_COLS = 8
MOD_ROWS = 8
MOD_TN = 512
VMEM_LIMIT = 56 * 1024 * 1024

f32 = jnp.float32
bf16 = jnp.bfloat16
u32 = jnp.uint32
i32 = jnp.int32


def _cparams(*sem, flags=None):
    return pltpu.CompilerParams(dimension_semantics=sem, vmem_limit_bytes=VMEM_LIMIT, flags=flags)


def _nt_dot(a, b):
    return lax.dot_general(a, b, (((1,), (1,)), ((), ())), preferred_element_type=f32)


def _mod_kernel(c_ref, w_ref, b_ref, o_ref, s_sc, *, n_rows):
    d = w_ref.shape[1]
    tn = w_ref.shape[2]
    nj = tn // LANES

    @pl.when((pl.program_id(0) == 0) & (pl.program_id(1) == 0))
    def _():
        c = c_ref[...]
        s_sc[...] = c * jax.nn.sigmoid(c)

    def body(k, accs):
        r0 = pl.multiple_of(k * SUBLANES, SUBLANES)
        w8 = w_ref[0, pl.ds(r0, SUBLANES), :]
        out = []
        for r in range(n_rows):
            s8 = s_sc[r, pl.ds(r0, SUBLANES), :]
            for j in range(nj):
                out.append(accs[r * nj + j] + s8 * w8[:, j * LANES:(j + 1) * LANES])
        return tuple(out)

    init = tuple(jnp.zeros((SUBLANES, LANES), f32) for _ in range(n_rows * nj))
    accs = lax.fori_loop(0, d // SUBLANES, body, init, unroll=4)
    rows = []
    for r in range(n_rows):
        row = jnp.concatenate(
            [jnp.sum(accs[r * nj + j], axis=0, keepdims=True) for j in range(nj)], axis=1)
        rows.append(row + b_ref[0])
    rows.append(jnp.zeros((MOD_ROWS - n_rows, tn), f32))
    o_ref[0] = jnp.concatenate(rows, axis=0)


def _modulation(cond, w_mod, b_mod):
    n_rows, d = cond.shape
    depth, _, n6 = w_mod.shape
    c_b = jnp.broadcast_to(cond[:, :, None], (n_rows, d, LANES))
    return pl.pallas_call(
        functools.partial(_mod_kernel, n_rows=n_rows),
        out_shape=jax.ShapeDtypeStruct((depth, MOD_ROWS, n6), f32),
        grid=(depth, n6 // MOD_TN),
        in_specs=[
            pl.BlockSpec((n_rows, d, LANES), lambda l, j: (0, 0, 0)),
            pl.BlockSpec((1, d, MOD_TN), lambda l, j: (l, 0, j)),
            pl.BlockSpec((1, 1, MOD_TN), lambda l, j: (l, 0, j)),
        ],
        out_specs=pl.BlockSpec((1, MOD_ROWS, MOD_TN), lambda l, j: (l, 0, j)),
        scratch_shapes=[pltpu.VMEM((n_rows, d, LANES), f32)],
        compiler_params=_cparams("arbitrary", "arbitrary"),
        name="adaln_mod",
    )(c_b, w_mod, b_mod.reshape(depth, 1, n6))


class _Rows:
    def __init__(self, n_batch, ctx_len, seq_len):
        self.n_batch = n_batch
        self.ctx_len = ctx_len
        self.seq_len = seq_len
        self.n_ctx_tiles = ctx_len // ROW_TILE
        self.n_seq_tiles = seq_len // ROW_TILE
        self.tiles_per_batch = self.n_ctx_tiles + self.n_seq_tiles
        self.n_tiles = n_batch * self.tiles_per_batch
        self.t_all = self.n_tiles * ROW_TILE

    def is_ctx(self, i):
        return i % self.tiles_per_batch < self.n_ctx_tiles

    def seg(self, i):
        return jnp.where(self.is_ctx(i), 0, 1 + i // self.tiles_per_batch)

    def ctx_tile(self, i):
        r = i % self.tiles_per_batch
        return (i // self.tiles_per_batch) * self.n_ctx_tiles + jnp.minimum(r, self.n_ctx_tiles - 1)

    def lat_tile(self, i):
        r = i % self.tiles_per_batch
        return (i // self.tiles_per_batch) * self.n_seq_tiles + jnp.maximum(r - self.n_ctx_tiles, 0)


def _qkv_kernel(xc_ref, xl_ref, mod_ref, w_ref, qg_ref, kg_ref, cos_ref, sin_ref, q_ref, k_ref, v_ref, *, rows):
    tm, d = xl_ref.shape
    kv_dim = k_ref.shape[1]
    x = jnp.where(rows.is_ctx(pl.program_id(0)), xc_ref[...], xl_ref[...])
    sh1 = mod_ref[0, 0:1, :]
    sc1 = mod_ref[0, 1:2, :]
    u = (x * (1.0 + sc1) + sh1).astype(bf16)
    qkv = jnp.dot(u, w_ref[...], preferred_element_type=f32)
    cos = cos_ref[...]
    sin = sin_ref[...]
    lane = lax.broadcasted_iota(i32, (tm, HEAD_DIM), 1)
    first = (lane % (2 * ROPE_PAIRS)) < ROPE_PAIRS

    def norm_rope(xh, gain, scale):
        ms = jnp.mean(xh * xh, axis=-1, keepdims=True)
        xn = xh * lax.rsqrt(ms + RMS_EPS) * gain
        rot = jnp.where(first, pltpu.roll(xn, HEAD_DIM - ROPE_PAIRS, 1), pltpu.roll(xn, ROPE_PAIRS, 1))
        y = xn * cos + rot * sin
        return y * scale if scale != 1.0 else y

    qg = qg_ref[...]
    kg = kg_ref[...]
    for h in range(N_HEADS):
        sl = slice(h * HEAD_DIM, (h + 1) * HEAD_DIM)
        q_ref[:, sl] = norm_rope(qkv[:, sl], qg, Q_SCALE).astype(bf16)
    ones = jnp.ones((tm, HEAD_DIM), bf16)
    for h in range(N_KV_HEADS):
        sl = slice(h * HEAD_DIM, (h + 1) * HEAD_DIM)
        k_ref[:, sl] = norm_rope(qkv[:, d + h * HEAD_DIM:d + (h + 1) * HEAD_DIM], kg, 1.0).astype(bf16)
        v0 = d + kv_dim + h * HEAD_DIM
        v_ref[:, 2 * h * HEAD_DIM:(2 * h + 1) * HEAD_DIM] = qkv[:, v0:v0 + HEAD_DIM].astype(bf16)
        v_ref[:, (2 * h + 1) * HEAD_DIM:(2 * h + 2) * HEAD_DIM] = ones


def _qkv_proj(ctx2, x2, mod_l, w_qkv_bf, q_gain, k_gain, cos_t, sin_t, rows):
    d = x2.shape[1]
    kv_dim = N_KV_HEADS * HEAD_DIM
    tpb = rows.tiles_per_batch
    return pl.pallas_call(
        functools.partial(_qkv_kernel, rows=rows),
        out_shape=(jax.ShapeDtypeStruct((rows.t_all, d), bf16),
                   jax.ShapeDtypeStruct((rows.t_all, kv_dim), bf16),
                   jax.ShapeDtypeStruct((rows.t_all, 2 * kv_dim), bf16)),
        grid=(rows.n_tiles,),
        in_specs=[
            pl.BlockSpec((ROW_TILE, d), lambda i: (rows.ctx_tile(i), 0)),
            pl.BlockSpec((ROW_TILE, d), lambda i: (rows.lat_tile(i), 0)),
            pl.BlockSpec((1, 6, d), lambda i: (rows.seg(i), 0, 0)),
            pl.BlockSpec((d, d + 2 * kv_dim), lambda i: (0, 0)),
            pl.BlockSpec((1, HEAD_DIM), lambda i: (0, 0)),
            pl.BlockSpec((1, HEAD_DIM), lambda i: (0, 0)),
            pl.BlockSpec((ROW_TILE, HEAD_DIM), lambda i: (i % tpb, 0)),
            pl.BlockSpec((ROW_TILE, HEAD_DIM), lambda i: (i % tpb, 0)),
        ],
        out_specs=(pl.BlockSpec((ROW_TILE, d), lambda i: (i, 0)),
                   pl.BlockSpec((ROW_TILE, kv_dim), lambda i: (i, 0)),
                   pl.BlockSpec((ROW_TILE, 2 * kv_dim), lambda i: (i, 0))),
        compiler_params=_cparams("arbitrary"),
        name="qkv_rope",
    )(ctx2, x2, mod_l, w_qkv_bf, q_gain, k_gain, cos_t, sin_t)


def _attn_kernel(bound_ref, q_ref, k_ref, v_ref, wg_ref, wu_ref, wd_ref, o_ref, wgo_ref, wuo_ref, wdo_ref,
                 *, ctx_len, tk, rs, n_ctx_tiles, bounded):
    tq = q_ref.shape[0]
    n_lat_chunks = (k_ref.shape[0] - ctx_len) // tk
    qi = pl.program_id(2)
    subs = [(h, r0) for h in range(Q_PER_KV) for r0 in range(0, tq, rs)]

    def attend(chunks):
        wgo_ref[...] = wg_ref[...].astype(bf16)
        wuo_ref[...] = wu_ref[...].astype(bf16)
        wdo_ref[...] = wd_ref[...].astype(bf16)
        states = [None] * len(subs)
        for off, size in chunks:
            kc = k_ref[off:off + size, :]
            vc = v_ref[off:off + size, :]
            for i, (h, r0) in enumerate(subs):
                s = _nt_dot(q_ref[r0:r0 + rs, h * HEAD_DIM:(h + 1) * HEAD_DIM], kc)
                if bounded:
                    pv = jnp.dot(jnp.exp2(s - bound_ref[0, 0]).astype(bf16), vc, preferred_element_type=f32)
                    states[i] = pv if states[i] is None else states[i] + pv
                    continue
                mx = s[:, 0:LANES]
                for j in range(1, size // LANES):
                    mx = jnp.maximum(mx, s[:, j * LANES:(j + 1) * LANES])
                m_cur = jnp.max(mx, axis=-1, keepdims=True)
                if states[i] is None:
                    m_new = jnp.broadcast_to(m_cur, (rs, LANES))
                else:
                    m_prev, acc_prev = states[i]
                    m_new = jnp.maximum(m_prev, m_cur)
                p = jnp.concatenate(
                    [jnp.exp2(s[:, j * LANES:(j + 1) * LANES] - m_new) for j in range(size // LANES)],
                    axis=1).astype(bf16)
                acc = jnp.dot(p, vc, preferred_element_type=f32)
                if states[i] is not None:
                    alpha = jnp.exp2(m_prev - m_new)
                    acc = acc_prev * jnp.concatenate([alpha, alpha], axis=1) + acc
                states[i] = (m_new, acc)
        for i, (h, r0) in enumerate(subs):
            acc = states[i] if bounded else states[i][1]
            o_ref[r0:r0 + rs, h * HEAD_DIM:(h + 1) * HEAD_DIM] = (
                acc[:, :HEAD_DIM] / acc[:, HEAD_DIM:]).astype(bf16)

    @pl.when(qi < n_ctx_tiles)
    def _():
        attend([(0, ctx_len)])

    @pl.when(qi >= n_ctx_tiles)
    def _():
        attend([(0, ctx_len + tk)] + [(ctx_len + c * tk, tk) for c in range(1, n_lat_chunks)])


def _attention(bound, q, k, v, w_gate, w_up, w_down, rows, bounded):
    t_all, d = q.shape
    tpb = rows.tiles_per_batch
    rows_per_batch = t_all // rows.n_batch
    gw = Q_PER_KV * HEAD_DIM
    tk = min(ATTN_TK, rows.seq_len)
    assert rows.seq_len % tk == 0

    n_steps = rows.n_batch * N_KV_HEADS * tpb
    wg2 = w_gate.reshape(-1, w_gate.shape[-1])
    wu2 = w_up.reshape(-1, w_up.shape[-1])
    wd2 = w_down.reshape(-1, w_down.shape[-1])
    n_slabs = 1 << (n_steps.bit_length() - 1)
    slab_g = wg2.shape[0] // n_slabs
    slab_d = wd2.shape[0] // n_slabs
    assert wg2.shape[0] == n_slabs * slab_g and wd2.shape[0] == n_slabs * slab_d and wu2.shape == wg2.shape
    assert slab_g % SUBLANES == 0 and slab_d % SUBLANES == 0

    def slab(b, g, i):
        return (jnp.minimum((b * N_KV_HEADS + g) * tpb + i, n_slabs - 1), 0)

    kern = functools.partial(_attn_kernel, ctx_len=rows.ctx_len, tk=tk, rs=ATTN_ROWS,
                             n_ctx_tiles=rows.n_ctx_tiles, bounded=bounded)
    o, wg_bf, wu_bf, wd_bf = pl.pallas_call(
        kern,
        out_shape=(jax.ShapeDtypeStruct((t_all, d), bf16),
                   jax.ShapeDtypeStruct(wg2.shape, bf16),
                   jax.ShapeDtypeStruct(wu2.shape, bf16),
                   jax.ShapeDtypeStruct(wd2.shape, bf16)),
        grid=(rows.n_batch, N_KV_HEADS, tpb),
        in_specs=[
            pl.BlockSpec((1, 1), lambda b, g, i: (0, 0), memory_space=pltpu.SMEM),
            pl.BlockSpec((ROW_TILE, gw), lambda b, g, i: (b * tpb + i, g)),
            pl.BlockSpec((rows_per_batch, HEAD_DIM), lambda b, g, i: (b, g)),
            pl.BlockSpec((rows_per_batch, 2 * HEAD_DIM), lambda b, g, i: (b, g)),
            pl.BlockSpec((slab_g, wg2.shape[1]), slab),
            pl.BlockSpec((slab_g, wu2.shape[1]), slab),
            pl.BlockSpec((slab_d, wd2.shape[1]), slab),
        ],
        out_specs=(pl.BlockSpec((ROW_TILE, gw), lambda b, g, i: (b * tpb + i, g)),
                   pl.BlockSpec((slab_g, wg2.shape[1]), slab),
                   pl.BlockSpec((slab_g, wu2.shape[1]), slab),
                   pl.BlockSpec((slab_d, wd2.shape[1]), slab)),
        compiler_params=_cparams("arbitrary", "arbitrary", "arbitrary"),
        name="gqa_attention_bounded" if bounded else "gqa_attention_online",
    )(bound, q, k, v, wg2, wu2, wd2)
    return o, wg_bf.reshape(w_gate.shape), wu_bf.reshape(w_up.shape), wd_bf.reshape(w_down.shape)


def _post_epilogue(x, y, r0, mod_ref, lng_ref, lnb_ref, wr_ref, rb_ref, x1_ref, hp_ref, e_ref, g_ref):
    tm, d = x.shape
    rsl = slice(r0, r0 + tm)
    g1 = mod_ref[0, 2:3, :]
    sh2 = mod_ref[0, 3:4, :]
    sc2 = mod_ref[0, 4:5, :]
    z = DEEPNORM_ALPHA * x + g1 * y
    mu = jnp.mean(z, axis=-1, keepdims=True)
    zc = z - mu
    var = jnp.mean(zc * zc, axis=-1, keepdims=True)
    x1 = zc * lax.rsqrt(var + LN_EPS) * lng_ref[...] + lnb_ref[...]
    h = x1 * (1.0 + sc2) + sh2
    x1_ref[rsl, :] = x1

    h_hi = h.astype(bf16)
    h_hi32 = h_hi.astype(f32)
    bits = lax.bitcast_convert_type(h_hi32, u32)
    hp_ref[rsl, :] = (bits[:, :d // 2] >> 16) | bits[:, d // 2:]

    h_lo = (h - h_hi32).astype(bf16)
    wr = wr_ref[...]
    a2 = _nt_dot(wr, h_hi)
    lt = a2[:N_EXPERTS] + a2[N_EXPERTS:] + _nt_dot(wr[:N_EXPERTS], h_lo)
    aff = jax.nn.sigmoid(lt)
    sel = aff + rb_ref[...]
    ng = N_EXPERT_GROUPS
    a = [aff[j * ng:(j + 1) * ng] for j in range(EXPERTS_PER_GROUP)]
    s = [sel[j * ng:(j + 1) * ng] for j in range(EXPERTS_PER_GROUP)]
    gs = None
    for i in range(EXPERTS_PER_GROUP):
        for j in range(i + 1, EXPERTS_PER_GROUP):
            pij = s[i] + s[j]
            gs = pij if gs is None else jnp.maximum(gs, pij)
    gio = lax.broadcasted_iota(i32, gs.shape, 0)
    gmax = jnp.max(gs, axis=0, keepdims=True)
    g_idx = jnp.min(jnp.where(gs == gmax, gio, ng), axis=0, keepdims=True)
    onehot = gio == g_idx
    sin_g = [jnp.sum(jnp.where(onehot, s[j], 0.0), axis=0, keepdims=True) for j in range(EXPERTS_PER_GROUP)]
    aff_g = [jnp.sum(jnp.where(onehot, a[j], 0.0), axis=0, keepdims=True) for j in range(EXPERTS_PER_GROUP)]
    loc = [jnp.zeros_like(g_idx), jnp.zeros_like(g_idx)]
    gat = [jnp.zeros_like(gmax), jnp.zeros_like(gmax)]
    for j in range(EXPERTS_PER_GROUP):
        rank = jnp.zeros_like(g_idx)
        for i in range(EXPERTS_PER_GROUP):
            if i == j:
                continue
            ahead = (sin_g[i] >= sin_g[j]) if i < j else (sin_g[i] > sin_g[j])
            rank = rank + ahead.astype(i32)
        for kk in range(2):
            hit = rank == kk
            loc[kk] = jnp.where(hit, j, loc[kk])
            gat[kk] = jnp.where(hit, aff_g[j], gat[kk])
    den = gat[0] + gat[1]
    e_ref[0, :, rsl] = jnp.concatenate(
        [g_idx * EXPERTS_PER_GROUP + loc[0], g_idx * EXPERTS_PER_GROUP + loc[1]], axis=0)
    g8 = jnp.concatenate([gat[0] / den, gat[1] / den, jnp.zeros((GATE_COLS - 2, tm), f32)], axis=0)
    g_ref[rsl, :] = g8.T


def _attn_post_kernel(xc_ref, xl_ref, o_ref, wo_ref, mod_ref, lng_ref, lnb_ref, wr_ref, rb_ref,
                      x1_ref, hp_ref, e_ref, g_ref, ya_sc, yb_sc, *, rows):
    s = pl.program_id(0)
    tile = jnp.maximum(s - 1, 0)

    @pl.when(s == 0)
    def _():
        yb_sc[...] = jnp.zeros(yb_sc.shape, f32)

    def body(y_write, y_read):
        y_write[...] = jnp.dot(o_ref[...], wo_ref[...], preferred_element_type=f32)
        x = jnp.where(rows.is_ctx(tile), xc_ref[...], xl_ref[...])
        _post_epilogue(x, y_read[...], 0, mod_ref, lng_ref, lnb_ref, wr_ref, rb_ref,
                       x1_ref, hp_ref, e_ref, g_ref)

    @pl.when(s % 2 == 0)
    def _():
        body(ya_sc, yb_sc)

    @pl.when(s % 2 == 1)
    def _():
        body(yb_sc, ya_sc)


def _post_out(n_rows, d, tile_of):
    shapes = (jax.ShapeDtypeStruct((n_rows, d), f32),
              jax.ShapeDtypeStruct((n_rows, d // 2), u32),
              jax.ShapeDtypeStruct((n_rows // ROW_TILE, 2, ROW_TILE), i32),
              jax.ShapeDtypeStruct((n_rows, GATE_COLS), f32))
    specs = (pl.BlockSpec((ROW_TILE, d), lambda i: (tile_of(i), 0)),
             pl.BlockSpec((ROW_TILE, d // 2), lambda i: (tile_of(i), 0)),
             pl.BlockSpec((1, 2, ROW_TILE), lambda i: (tile_of(i), 0, 0)),
             pl.BlockSpec((ROW_TILE, GATE_COLS), lambda i: (tile_of(i), 0)))
    return shapes, specs


def _router_specs(d):
    return [pl.BlockSpec((2 * N_EXPERTS, d), lambda i: (0, 0)),
            pl.BlockSpec((N_EXPERTS, 1), lambda i: (0, 0))]


def _attn_post(ctx2, x2, o, w_o_bf, mod_l, ln_g, ln_b, wr_hl, rb_p, rows):
    d = x2.shape[1]
    n = rows.n_tiles

    def prev(i):
        return jnp.maximum(i - 1, 0)

    shapes, specs = _post_out(rows.t_all, d, prev)
    return pl.pallas_call(
        functools.partial(_attn_post_kernel, rows=rows),
        out_shape=shapes,
        grid=(n + 1,),
        in_specs=[
            pl.BlockSpec((ROW_TILE, d), lambda i: (rows.ctx_tile(prev(i)), 0)),
            pl.BlockSpec((ROW_TILE, d), lambda i: (rows.lat_tile(prev(i)), 0)),
            pl.BlockSpec((ROW_TILE, d), lambda i: (jnp.minimum(i, n - 1), 0)),
            pl.BlockSpec((d, d), lambda i: (0, 0)),
            pl.BlockSpec((1, 6, d), lambda i: (rows.seg(prev(i)), 0, 0)),
            pl.BlockSpec((1, d), lambda i: (0, 0)),
            pl.BlockSpec((1, d), lambda i: (0, 0)),
        ] + _router_specs(d),
        out_specs=specs,
        scratch_shapes=[pltpu.VMEM((ROW_TILE, d), f32), pltpu.VMEM((ROW_TILE, d), f32)],
        compiler_params=_cparams("arbitrary"),
        name="attn_out_norm_route",
    )(ctx2, x2, o, w_o_bf, mod_l, ln_g, ln_b, wr_hl, rb_p)


def _pool_post_kernel(x_ref, xp_ref, xn_ref, wp_ref, ps_ref, mod_ref, lng_ref, lnb_ref, wr_ref, rb_ref,
                      x1_ref, hp_ref, e_ref, g_ref, *, seq_len, tiles_per_seq):
    tm, d = x_ref.shape
    gd = d // len(POOL_WINDOWS)
    r = pl.program_id(0) % tiles_per_seq
    sh1 = mod_ref[0, 0:1, :]
    sc1 = mod_ref[0, 1:2, :]
    x = x_ref[...]
    ext = jnp.concatenate([xp_ref[...], x, xn_ref[...]], axis=0)
    n_ext = tm + 2 * POOL_HALO
    pos = r * tm - POOL_HALO + lax.broadcasted_iota(i32, (n_ext, 1), 0)
    inside = (pos >= 0) & (pos < seq_len)
    u_ext = jnp.where(inside, ext * (1.0 + sc1) + sh1, 0.0)
    t = pos[POOL_HALO:POOL_HALO + tm]
    dlts = []
    for g, w in enumerate(POOL_WINDOWS):
        e = u_ext[:, g * gd:(g + 1) * gd]
        p = e + pltpu.roll(e, 1, 0)
        step = 1
        while 2 * step < w:
            p = pltpu.roll(p, step, 0) + pltpu.roll(p, n_ext - step, 0)
            step *= 2
        cnt = (jnp.clip(t + w // 2, 0, seq_len) - jnp.clip(t - w // 2, 0, seq_len)).astype(f32)
        mean = p[POOL_HALO:POOL_HALO + tm] / cnt
        dlts.append((mean - e[POOL_HALO:POOL_HALO + tm]).astype(bf16))
    for r0 in range(0, tm, POST_ROWS):
        rsl = slice(r0, r0 + POST_ROWS)
        y = jnp.concatenate(
            [jnp.dot(dlts[g][rsl, :], wp_ref[g], preferred_element_type=f32) for g in range(len(POOL_WINDOWS))],
            axis=-1) * ps_ref[...]
        _post_epilogue(x[rsl, :], y, r0, mod_ref, lng_ref, lnb_ref, wr_ref, rb_ref, x1_ref, hp_ref, e_ref, g_ref)


def _pool_post(x_all, w_pool_bf, pool_scale, mod_l, ln_g, ln_b, wr_hl, rb_p, rows):
    t_all, d = x_all.shape
    tiles_per_seq = rows.n_seq_tiles
    n_rows = rows.n_batch * rows.seq_len
    halo_per_tile = ROW_TILE // POOL_HALO
    last_halo = t_all // POOL_HALO - 1
    ng = len(POOL_WINDOWS)

    def src_tile(i):
        return (i // tiles_per_seq) * rows.tiles_per_batch + rows.n_ctx_tiles + i % tiles_per_seq

    shapes, specs = _post_out(n_rows, d, lambda i: i)
    kern = functools.partial(_pool_post_kernel, seq_len=rows.seq_len, tiles_per_seq=tiles_per_seq)
    return pl.pallas_call(
        kern,
        out_shape=shapes,
        grid=(n_rows // ROW_TILE,),
        in_specs=[
            pl.BlockSpec((ROW_TILE, d), lambda i: (src_tile(i), 0)),
            pl.BlockSpec((POOL_HALO, d), lambda i: (jnp.maximum(src_tile(i) * halo_per_tile - 1, 0), 0)),
            pl.BlockSpec((POOL_HALO, d), lambda i: (jnp.minimum((src_tile(i) + 1) * halo_per_tile, last_halo), 0)),
            pl.BlockSpec((ng, d // ng, d // ng), lambda i: (0, 0, 0)),
            pl.BlockSpec((1, d), lambda i: (0, 0)),
            pl.BlockSpec((1, 6, d), lambda i: (1 + i // tiles_per_seq, 0, 0)),
            pl.BlockSpec((1, d), lambda i: (0, 0)),
            pl.BlockSpec((1, d), lambda i: (0, 0)),
        ] + _router_specs(d),
        out_specs=specs,
        compiler_params=_cparams("arbitrary"),
        name="pool_norm_route",
    )(x_all, x_all, x_all, w_pool_bf, pool_scale, mod_l, ln_g, ln_b, wr_hl, rb_p)


def _plan_kernel(e_ref, dest_ref, be_ref, nu_ref, carry_sc, start_sc):
    phase = pl.program_id(0)
    i = pl.program_id(1)
    tm = e_ref.shape[2]
    sub = lax.broadcasted_iota(i32, (N_EXPERTS, tm), 0)
    shift = MOE_BLOCK.bit_length() - 1

    @pl.when((phase == 0) & (i == 0))
    def _():
        carry_sc[...] = jnp.zeros(carry_sc.shape, f32)

    @pl.when(phase == 0)
    def _():
        carry = carry_sc[...]
        for kk in range(2):
            hit = sub == e_ref[0, kk:kk + 1, :]
            carry = carry + jnp.sum(jnp.where(hit, 1.0, 0.0), axis=1, keepdims=True)
        carry_sc[...] = carry

    @pl.when((phase == 1) & (i == 0))
    def _():
        counts = carry_sc[...].astype(i32)
        padded = ((counts + (MOE_BLOCK - 1)) >> shift) << shift
        esub = lax.broadcasted_iota(i32, padded.shape, 0)
        start = jnp.zeros(padded.shape, i32)
        for e in range(N_EXPERTS - 1):
            start = start + jnp.where(esub > e, padded[e:e + 1, :], 0)
        ends = start + padded
        nb_pad = be_ref.shape[1]
        blk_start = lax.broadcasted_iota(i32, (1, nb_pad), 1) * MOE_BLOCK
        be = jnp.zeros((1, nb_pad), i32)
        for e in range(N_EXPERTS):
            end_e = jnp.concatenate([ends[e:e + 1, :]] * (nb_pad // LANES), axis=1)
            be = be + jnp.where(end_e <= blk_start, 1, 0)
        be_ref[...] = jnp.minimum(be, N_EXPERTS - 1)
        nu_ref[...] = ends[N_EXPERTS - 1:N_EXPERTS, :] >> shift
        start_sc[...] = start.astype(f32)
        carry_sc[...] = jnp.zeros(carry_sc.shape, f32)

    @pl.when(phase == 1)
    def _():
        before = lax.broadcasted_iota(i32, (tm, tm), 0) < lax.broadcasted_iota(i32, (tm, tm), 1)
        tri = jnp.where(before, 1.0, 0.0).astype(bf16)
        carry = carry_sc[...]
        start = start_sc[...]
        out = []
        for kk in range(2):
            hit = sub == e_ref[0, kk:kk + 1, :]
            onehot = jnp.where(hit, 1.0, 0.0)
            cum = jnp.dot(onehot.astype(bf16), tri, preferred_element_type=f32)
            base = (carry + start)[:, 0:1]
            out.append(jnp.sum(jnp.where(hit, cum + base, 0.0), axis=0, keepdims=True))
            carry = carry + jnp.sum(onehot, axis=1, keepdims=True)
        dest_ref[0] = jnp.concatenate(out, axis=0).astype(i32)
        carry_sc[...] = carry


def _dispatch_plan(e3):
    n_tiles, _, tm = e3.shape
    n_blocks = (2 * n_tiles * tm + N_EXPERTS * (MOE_BLOCK - 1) + MOE_BLOCK - 1) // MOE_BLOCK
    nb_pad = (n_blocks + LANES - 1) // LANES * LANES
    dest3, be, nu = pl.pallas_call(
        _plan_kernel,
        out_shape=(jax.ShapeDtypeStruct(e3.shape, i32),
                   jax.ShapeDtypeStruct((1, nb_pad), i32),
                   jax.ShapeDtypeStruct((1, LANES), i32)),
        grid=(2, n_tiles),
        in_specs=[pl.BlockSpec((1, 2, tm), lambda p, i: (i, 0, 0))],
        out_specs=(pl.BlockSpec((1, 2, tm), lambda p, i: (p * i, 0, 0)),
                   pl.BlockSpec((1, nb_pad), lambda p, i: (0, 0)),
                   pl.BlockSpec((1, LANES), lambda p, i: (0, 0))),
        scratch_shapes=[pltpu.VMEM((N_EXPERTS, LANES), f32), pltpu.VMEM((N_EXPERTS, LANES), f32)],
        compiler_params=_cparams("arbitrary", "arbitrary"),
        name="route_plan",
    )(e3)
    return be[0, :n_blocks], nu[0, :1], dest3, n_blocks


def _dispatch_kernel(dest_ref, hp_ref, xs_in, xs_hbm, buf, sem):
    del xs_in
    i = pl.program_id(0)
    n = pl.num_programs(0)
    tm = hp_ref.shape[0]
    slot = i % 2
    buf[slot] = hp_ref[...]
    for kk in range(2):
        for r in range(tm):
            pltpu.make_async_copy(
                buf.at[slot, pl.ds(r, 1), :], xs_hbm.at[pl.ds(dest_ref[0, kk, r], 1), :], sem.at[slot]).start()

    def drain(s):
        for _ in range(2 * tm):
            pltpu.make_async_copy(buf.at[s, pl.ds(0, 1), :], xs_hbm.at[pl.ds(0, 1), :], sem.at[s]).wait()

    @pl.when(i > 0)
    def _():
        drain(1 - slot)

    @pl.when(i == n - 1)
    def _():
        drain(slot)


def _dispatch(hp, dest3, n_blocks):
    t, dw = hp.shape
    n_tiles = t // ROW_TILE
    xs0 = jnp.zeros((n_blocks * MOE_BLOCK, dw), u32)
    return pl.pallas_call(
        _dispatch_kernel,
        out_shape=jax.ShapeDtypeStruct(xs0.shape, u32),
        grid=(n_tiles,),
        in_specs=[
            pl.BlockSpec((1, 2, ROW_TILE), lambda i: (i, 0, 0), memory_space=pltpu.SMEM),
            pl.BlockSpec((ROW_TILE, dw), lambda i: (i, 0)),
            pl.BlockSpec(memory_space=pl.ANY),
        ],
        out_specs=pl.BlockSpec(memory_space=pl.ANY),
        scratch_shapes=[pltpu.VMEM((2, ROW_TILE, dw), u32), pltpu.SemaphoreType.DMA((2,))],
        input_output_aliases={2: 0},
        compiler_params=_cparams("arbitrary"),
        name="moe_dispatch",
    )(dest3, hp, xs0)


def _ffn_kernel(be_ref, nu_ref, xs_ref, wg_ref, wu_ref, wd_ref, y_ref):
    b = pl.program_id(0)

    @pl.when(b < nu_ref[0])
    def _():
        w = xs_ref[...]
        lo = lax.bitcast_convert_type(w << 16, f32)
        hi = lax.bitcast_convert_type(w & jnp.uint32(0xFFFF0000), f32)
        xb = jnp.concatenate([lo, hi], axis=1).astype(bf16)
        a = jnp.dot(xb, wg_ref[0, 0], preferred_element_type=f32)
        u = jnp.dot(xb, wu_ref[0, 0], preferred_element_type=f32)
        z = (a * jax.nn.sigmoid(a) * u).astype(bf16)
        y_ref[...] = jnp.dot(z, wd_ref[0, 0], preferred_element_type=f32)

    @pl.when(b >= nu_ref[0])
    def _():
        y_ref[...] = jnp.zeros(y_ref.shape, f32)


def _expert_ffn(xs, blk_expert, n_used, layer, wg, wu, wd):
    slots, dw = xs.shape
    d = 2 * dw
    de = wg.shape[3]
    n_blocks = slots // MOE_BLOCK
    grid_spec = pltpu.PrefetchScalarGridSpec(
        num_scalar_prefetch=2,
        grid=(n_blocks,),
        in_specs=[
            pl.BlockSpec((MOE_BLOCK, dw), lambda b, be, nu: (b, 0)),
            pl.BlockSpec((1, 1, d, de), lambda b, be, nu: (layer, be[b], 0, 0)),
            pl.BlockSpec((1, 1, d, de), lambda b, be, nu: (layer, be[b], 0, 0)),
            pl.BlockSpec((1, 1, de, d), lambda b, be, nu: (layer, be[b], 0, 0)),
        ],
        out_specs=pl.BlockSpec((MOE_BLOCK, d), lambda b, be, nu: (b, 0)),
    )
    return pl.pallas_call(
        _ffn_kernel,
        out_shape=jax.ShapeDtypeStruct((slots, d), f32),
        grid_spec=grid_spec,
        compiler_params=_cparams("arbitrary"),
        name="expert_ffn",
    )(blk_expert, n_used, xs, wg, wu, wd)


def _fin_kernel(pos_cur, pos_nxt, x1_ref, gate_ref, mod_ref, lng_ref, lnb_ref, y_hbm, o_ref, ybuf, sem):
    i = pl.program_id(0)
    n = pl.num_programs(0)
    tm = x1_ref.shape[0]
    slot = i % 2

    def copy(pos_ref, kk, r, s):
        return pltpu.make_async_copy(
            y_hbm.at[pl.ds(pos_ref[0, kk, r], 1), :], ybuf.at[s, kk, pl.ds(r, 1), :], sem.at[s])

    def start_gather(pos_ref, s):
        for kk in range(2):
            for r in range(tm):
                copy(pos_ref, kk, r, s).start()

    @pl.when(i == 0)
    def _():
        start_gather(pos_cur, 0)

    @pl.when(i + 1 < n)
    def _():
        start_gather(pos_nxt, 1 - slot)

    for kk in range(2):
        for r in range(tm):
            copy(pos_cur, kk, r, slot).wait()
    g2 = mod_ref[0, 5:6, :]
    gate = gate_ref[...]
    f = gate[:, 0:1] * ybuf[slot, 0] + gate[:, 1:2] * ybuf[slot, 1]
    z = DEEPNORM_ALPHA * x1_ref[...] + g2 * f
    mu = jnp.mean(z, axis=-1, keepdims=True)
    zc = z - mu
    var = jnp.mean(zc * zc, axis=-1, keepdims=True)
    o_ref[...] = zc * lax.rsqrt(var + LN_EPS) * lng_ref[...] + lnb_ref[...]


def _combine_norm(x1, gate, pos3, y_sorted, mod_l, ln_g, ln_b, seg_of_tile):
    t, d = x1.shape
    n_tiles = t // ROW_TILE
    return pl.pallas_call(
        _fin_kernel,
        out_shape=jax.ShapeDtypeStruct((t, d), f32),
        grid=(n_tiles,),
        in_specs=[
            pl.BlockSpec((1, 2, ROW_TILE), lambda i: (i, 0, 0), memory_space=pltpu.SMEM),
            pl.BlockSpec((1, 2, ROW_TILE), lambda i: (jnp.minimum(i + 1, n_tiles - 1), 0, 0),
                         memory_space=pltpu.SMEM),
            pl.BlockSpec((ROW_TILE, d), lambda i: (i, 0)),
            pl.BlockSpec((ROW_TILE, GATE_COLS), lambda i: (i, 0)),
            pl.BlockSpec((1, 6, d), lambda i: (seg_of_tile(i), 0, 0)),
            pl.BlockSpec((1, d), lambda i: (0, 0)),
            pl.BlockSpec((1, d), lambda i: (0, 0)),
            pl.BlockSpec(memory_space=pl.ANY),
        ],
        out_specs=pl.BlockSpec((ROW_TILE, d), lambda i: (i, 0)),
        scratch_shapes=[pltpu.VMEM((2, 2, ROW_TILE, d), f32), pltpu.SemaphoreType.DMA((2,))],
        compiler_params=_cparams("arbitrary"),
        name="combine_norm",
    )(pos3, pos3, x1, gate, mod_l, ln_g, ln_b, y_sorted)


def _moe_combine(x1, hp, e_idx, gate, layer, wg_bf, wu_bf, wd_bf, mod_l, ln_g, ln_b, seg_of_tile):
    blk_expert, n_used, dest3, n_blocks = _dispatch_plan(e_idx)
    xs = _dispatch(hp, dest3, n_blocks)
    y_sorted = _expert_ffn(xs, blk_expert, n_used, layer, wg_bf, wu_bf, wd_bf)
    return _combine_norm(x1, gate, dest3, y_sorted, mod_l, ln_g, ln_b, seg_of_tile)


def _rope_tables(ctx_len, seq_len):
    t = jnp.arange(seq_len)
    row = (t // GRID_W).astype(f32)
    col = (t % GRID_W).astype(f32)
    inv_freq = ROPE_THETA ** (-jnp.arange(ROPE_PAIRS, dtype=f32) / ROPE_PAIRS)
    ang_r = row[:, None] * inv_freq
    ang_c = col[:, None] * inv_freq
    ang = jnp.concatenate([ang_r, ang_r, ang_c, ang_c], axis=-1)
    cos = jnp.cos(ang)
    sin = jnp.sin(ang)
    lane = jnp.arange(HEAD_DIM)
    sign = jnp.where((lane % (2 * ROPE_PAIRS)) < ROPE_PAIRS, -1.0, 1.0).astype(f32)
    cos_t = jnp.concatenate([jnp.ones((ctx_len, HEAD_DIM), f32), cos], axis=0)
    sin_t = jnp.concatenate([jnp.zeros((ctx_len, HEAD_DIM), f32), sin * sign], axis=0)
    return cos_t, sin_t


def kernel(x, c, ctx, c_ctx, w_mod, b_mod, ln_g, ln_b, w_qkv, q_gain, k_gain, w_o, w_pool, pool_scale,
           w_router, router_bias, w_gate, w_up, w_down):
    n_batch, seq_len, d = x.shape
    ctx_len = ctx.shape[1]
    assert w_mod.shape[0] == DEPTH and d == N_HEADS * HEAD_DIM
    assert ctx_len % ROW_TILE == 0 and seq_len % ROW_TILE == 0 and 1 + n_batch <= MOD_ROWS
    rows = _Rows(n_batch, ctx_len, seq_len)

    cond = jnp.concatenate([c_ctx[None, :], c], axis=0)
    mod = _modulation(cond, w_mod, b_mod).reshape(DEPTH, MOD_ROWS, 6, d)

    wr_t = w_router.reshape(d, N_EXPERT_GROUPS, EXPERTS_PER_GROUP).transpose(2, 1, 0).reshape(N_EXPERTS, d)
    wr_hi = wr_t.astype(bf16)
    wr_hl = jnp.concatenate([wr_hi, (wr_t - wr_hi.astype(f32)).astype(bf16)], axis=0)
    rb_p = router_bias.reshape(N_EXPERT_GROUPS, EXPERTS_PER_GROUP).T.reshape(N_EXPERTS, 1)
    cos_t, sin_t = _rope_tables(ctx_len, seq_len)

    ctx2 = ctx.reshape(n_batch * ctx_len, d)
    x2 = x.reshape(n_batch * seq_len, d)
    q, k, v = _qkv_proj(ctx2, x2, mod[0], w_qkv[0].astype(bf16), q_gain[0:1], k_gain[0:1], cos_t, sin_t, rows)
    bound = (Q_SCALE * HEAD_DIM * SCORE_BOUND_SLACK) * jnp.max(jnp.abs(q_gain[0])) * jnp.max(jnp.abs(k_gain[0]))
    attn_args = (bound.reshape(1, 1), q, k, v, w_gate, w_up, w_down)
    o, wg_bf, wu_bf, wd_bf = lax.cond(
        bound <= SCORE_BOUND_MAX,
        lambda a: _attention(*a, rows, True),
        lambda a: _attention(*a, rows, False),
        attn_args)
    x1, hp, e_idx, gate = _attn_post(ctx2, x2, o, w_o[0].astype(bf16), mod[0], ln_g[0, 0:1], ln_b[0, 0:1],
                                     wr_hl, rb_p, rows)
    x_all = _moe_combine(x1, hp, e_idx, gate, 0, wg_bf, wu_bf, wd_bf, mod[0], ln_g[0, 1:2], ln_b[0, 1:2],
                         rows.seg)

    x1, hp, e_idx, gate = _pool_post(x_all, w_pool[0].astype(bf16), pool_scale[0:1], mod[1],
                                     ln_g[1, 0:1], ln_b[1, 0:1], wr_hl, rb_p, rows)
    out = _moe_combine(x1, hp, e_idx, gate, 1, wg_bf, wu_bf, wd_bf, mod[1], ln_g[1, 1:2], ln_b[1, 1:2],
                       lambda i: 1 + i // rows.n_seq_tiles)
    return out.reshape(n_batch, seq_len, d)
```

```python
import functools

import jax
import jax.numpy as jnp
from jax import lax
from jax.experimental import pallas as pl
from jax.experimental.pallas import tpu as pltpu

N_HEADS = 16
N_KV_HEADS = 4
HEAD_DIM = 128
Q_PER_KV = N_HEADS // N_KV_HEADS
GRID_W = 64
ROPE_THETA = 10000.0
ROPE_PAIRS = HEAD_DIM // 4
ATTN_SCALE = HEAD_DIM ** -0.5
Q_SCALE = ATTN_SCALE * 1.4426950408889634
POOL_WINDOWS = (2, 4, 8, 16)
POOL_HALO = 8
N_EXPERTS = 32
N_EXPERT_GROUPS = 8
EXPERTS_PER_GROUP = N_EXPERTS // N_EXPERT_GROUPS
DEPTH = 2
DEEPNORM_ALPHA = (2 * DEPTH) ** 0.25
LN_EPS = 1e-6
RMS_EPS = 1e-6

LANES = 128
SUBLANES = 8
ROW_TILE = 256
ATTN_TK = 2048
ATTN_ROWS = 128
SCORE_BOUND_SLACK = 1.02
SCORE_BOUND_MAX = 56.0
POST_ROWS = 128
MOE_BLOCK = 256
NO_EXPERT = N_EXPERTS
INFO_FIRST, INFO_SLOT, INFO_NEXT = 8, 9, 10
GATE_COLS = 8
MOD_ROWS = 8
MOD_TN = 512
VMEM_LIMIT = 56 * 1024 * 1024

f32 = jnp.float32
bf16 = jnp.bfloat16
u32 = jnp.uint32
i32 = jnp.int32


def _cparams(*sem, flags=None):
    return pltpu.CompilerParams(dimension_semantics=sem, vmem_limit_bytes=VMEM_LIMIT, flags=flags)


def _nt_dot(a, b):
    return lax.dot_general(a, b, (((1,), (1,)), ((), ())), preferred_element_type=f32)


def _mod_kernel(c_ref, w_ref, b_ref, o_ref, s_sc, *, n_rows):
    d = w_ref.shape[1]
    tn = w_ref.shape[2]
    nj = tn // LANES

    @pl.when((pl.program_id(0) == 0) & (pl.program_id(1) == 0))
    def _():
        c = c_ref[...]
        s_sc[...] = c * jax.nn.sigmoid(c)

    def body(k, accs):
        r0 = pl.multiple_of(k * SUBLANES, SUBLANES)
        w8 = w_ref[0, pl.ds(r0, SUBLANES), :]
        out = []
        for r in range(n_rows):
            s8 = s_sc[r, pl.ds(r0, SUBLANES), :]
            for j in range(nj):
                out.append(accs[r * nj + j] + s8 * w8[:, j * LANES:(j + 1) * LANES])
        return tuple(out)

    init = tuple(jnp.zeros((SUBLANES, LANES), f32) for _ in range(n_rows * nj))
    accs = lax.fori_loop(0, d // SUBLANES, body, init, unroll=4)
    rows = []
    for r in range(n_rows):
        row = jnp.concatenate(
            [jnp.sum(accs[r * nj + j], axis=0, keepdims=True) for j in range(nj)], axis=1)
        rows.append(row + b_ref[0])
    rows.append(jnp.zeros((MOD_ROWS - n_rows, tn), f32))
    o_ref[0] = jnp.concatenate(rows, axis=0)


def _modulation(cond, w_mod, b_mod):
    n_rows, d = cond.shape
    depth, _, n6 = w_mod.shape
    c_b = jnp.broadcast_to(cond[:, :, None], (n_rows, d, LANES))
    return pl.pallas_call(
        functools.partial(_mod_kernel, n_rows=n_rows),
        out_shape=jax.ShapeDtypeStruct((depth, MOD_ROWS, n6), f32),
        grid=(depth, n6 // MOD_TN),
        in_specs=[
            pl.BlockSpec((n_rows, d, LANES), lambda l, j: (0, 0, 0)),
            pl.BlockSpec((1, d, MOD_TN), lambda l, j: (l, 0, j)),
            pl.BlockSpec((1, 1, MOD_TN), lambda l, j: (l, 0, j)),
        ],
        out_specs=pl.BlockSpec((1, MOD_ROWS, MOD_TN), lambda l, j: (l, 0, j)),
        scratch_shapes=[pltpu.VMEM((n_rows, d, LANES), f32)],
        compiler_params=_cparams("arbitrary", "arbitrary"),
        name="adaln_mod",
    )(c_b, w_mod, b_mod.reshape(depth, 1, n6))


class _Rows:
    def __init__(self, n_batch, ctx_len, seq_len):
        self.n_batch = n_batch
        self.ctx_len = ctx_len
        self.seq_len = seq_len
        self.n_ctx_tiles = ctx_len // ROW_TILE
        self.n_seq_tiles = seq_len // ROW_TILE
        self.tiles_per_batch = self.n_ctx_tiles + self.n_seq_tiles
        self.n_tiles = n_batch * self.tiles_per_batch
        self.t_all = self.n_tiles * ROW_TILE

    def is_ctx(self, i):
        return i % self.tiles_per_batch < self.n_ctx_tiles

    def seg(self, i):
        return jnp.where(self.is_ctx(i), 0, 1 + i // self.tiles_per_batch)

    def ctx_tile(self, i):
        r = i % self.tiles_per_batch
        return (i // self.tiles_per_batch) * self.n_ctx_tiles + jnp.minimum(r, self.n_ctx_tiles - 1)

    def lat_tile(self, i):
        r = i % self.tiles_per_batch
        return (i // self.tiles_per_batch) * self.n_seq_tiles + jnp.maximum(r - self.n_ctx_tiles, 0)


def _qkv_kernel(xc_ref, xl_ref, mod_ref, w_ref, qg_ref, kg_ref, cos_ref, sin_ref, q_ref, k_ref, v_ref, *, rows):
    tm, d = xl_ref.shape
    kv_dim = k_ref.shape[1]
    x = jnp.where(rows.is_ctx(pl.program_id(0)), xc_ref[...], xl_ref[...])
    sh1 = mod_ref[0, 0:1, :]
    sc1 = mod_ref[0, 1:2, :]
    u = (x * (1.0 + sc1) + sh1).astype(bf16)
    qkv = jnp.dot(u, w_ref[...], preferred_element_type=f32)
    cos = cos_ref[...]
    sin = sin_ref[...]
    lane = lax.broadcasted_iota(i32, (tm, HEAD_DIM), 1)
    first = (lane % (2 * ROPE_PAIRS)) < ROPE_PAIRS

    def norm_rope(xh, gain, scale):
        ms = jnp.mean(xh * xh, axis=-1, keepdims=True)
        xn = xh * lax.rsqrt(ms + RMS_EPS) * gain
        rot = jnp.where(first, pltpu.roll(xn, HEAD_DIM - ROPE_PAIRS, 1), pltpu.roll(xn, ROPE_PAIRS, 1))
        y = xn * cos + rot * sin
        return y * scale if scale != 1.0 else y

    qg = qg_ref[...]
    kg = kg_ref[...]
    for h in range(N_HEADS):
        sl = slice(h * HEAD_DIM, (h + 1) * HEAD_DIM)
        q_ref[:, sl] = norm_rope(qkv[:, sl], qg, Q_SCALE).astype(bf16)
    ones = jnp.ones((tm, HEAD_DIM), bf16)
    for h in range(N_KV_HEADS):
        sl = slice(h * HEAD_DIM, (h + 1) * HEAD_DIM)
        k_ref[:, sl] = norm_rope(qkv[:, d + h * HEAD_DIM:d + (h + 1) * HEAD_DIM], kg, 1.0).astype(bf16)
        v0 = d + kv_dim + h * HEAD_DIM
        v_ref[:, 2 * h * HEAD_DIM:(2 * h + 1) * HEAD_DIM] = qkv[:, v0:v0 + HEAD_DIM].astype(bf16)
        v_ref[:, (2 * h + 1) * HEAD_DIM:(2 * h + 2) * HEAD_DIM] = ones


def _qkv_proj(ctx2, x2, mod_l, w_qkv_bf, q_gain, k_gain, cos_t, sin_t, rows):
    d = x2.shape[1]
    kv_dim = N_KV_HEADS * HEAD_DIM
    tpb = rows.tiles_per_batch
    return pl.pallas_call(
        functools.partial(_qkv_kernel, rows=rows),
        out_shape=(jax.ShapeDtypeStruct((rows.t_all, d), bf16),
                   jax.ShapeDtypeStruct((rows.t_all, kv_dim), bf16),
                   jax.ShapeDtypeStruct((rows.t_all, 2 * kv_dim), bf16)),
        grid=(rows.n_tiles,),
        in_specs=[
            pl.BlockSpec((ROW_TILE, d), lambda i: (rows.ctx_tile(i), 0)),
            pl.BlockSpec((ROW_TILE, d), lambda i: (rows.lat_tile(i), 0)),
            pl.BlockSpec((1, 6, d), lambda i: (rows.seg(i), 0, 0)),
            pl.BlockSpec((d, d + 2 * kv_dim), lambda i: (0, 0)),
            pl.BlockSpec((1, HEAD_DIM), lambda i: (0, 0)),
            pl.BlockSpec((1, HEAD_DIM), lambda i: (0, 0)),
            pl.BlockSpec((ROW_TILE, HEAD_DIM), lambda i: (i % tpb, 0)),
            pl.BlockSpec((ROW_TILE, HEAD_DIM), lambda i: (i % tpb, 0)),
        ],
        out_specs=(pl.BlockSpec((ROW_TILE, d), lambda i: (i, 0)),
                   pl.BlockSpec((ROW_TILE, kv_dim), lambda i: (i, 0)),
                   pl.BlockSpec((ROW_TILE, 2 * kv_dim), lambda i: (i, 0))),
        compiler_params=_cparams("arbitrary"),
        name="qkv_rope",
    )(ctx2, x2, mod_l, w_qkv_bf, q_gain, k_gain, cos_t, sin_t)


def _attn_kernel(bound_ref, q_ref, k_ref, v_ref, wg_ref, wu_ref, wd_ref, o_ref, wgo_ref, wuo_ref, wdo_ref,
                 *, ctx_len, tk, rs, n_ctx_tiles, bounded):
    tq = q_ref.shape[0]
    n_lat_chunks = (k_ref.shape[0] - ctx_len) // tk
    qi = pl.program_id(2)
    subs = [(h, r0) for h in range(Q_PER_KV) for r0 in range(0, tq, rs)]

    def attend(chunks):
        states = [None] * len(subs)
        for ci, (off, size) in enumerate(chunks):
            for src, dst in ((wg_ref, wgo_ref), (wu_ref, wuo_ref), (wd_ref, wdo_ref)):
                n = src.shape[0] // len(chunks)
                dst[ci * n:(ci + 1) * n, :] = src[ci * n:(ci + 1) * n, :].astype(bf16)
            kc = k_ref[off:off + size, :]
            vc = v_ref[off:off + size, :]
            for i, (h, r0) in enumerate(subs):
                s = _nt_dot(q_ref[r0:r0 + rs, h * HEAD_DIM:(h + 1) * HEAD_DIM], kc)
                if bounded:
                    pv = jnp.dot(jnp.exp2(s - bound_ref[0, 0]).astype(bf16), vc, preferred_element_type=f32)
                    states[i] = pv if states[i] is None else states[i] + pv
                    continue
                mx = s[:, 0:LANES]
                for j in range(1, size // LANES):
                    mx = jnp.maximum(mx, s[:, j * LANES:(j + 1) * LANES])
                m_cur = jnp.max(mx, axis=-1, keepdims=True)
                if states[i] is None:
                    m_new = jnp.broadcast_to(m_cur, (rs, LANES))
                else:
                    m_prev, acc_prev = states[i]
                    m_new = jnp.maximum(m_prev, m_cur)
                p = jnp.concatenate(
                    [jnp.exp2(s[:, j * LANES:(j + 1) * LANES] - m_new) for j in range(size // LANES)],
                    axis=1).astype(bf16)
                acc = jnp.dot(p, vc, preferred_element_type=f32)
                if states[i] is not None:
                    alpha = jnp.exp2(m_prev - m_new)
                    acc = acc_prev * jnp.concatenate([alpha, alpha], axis=1) + acc
                states[i] = (m_new, acc)
        for i, (h, r0) in enumerate(subs):
            acc = states[i] if bounded else states[i][1]
            o_ref[r0:r0 + rs, h * HEAD_DIM:(h + 1) * HEAD_DIM] = (
                acc[:, :HEAD_DIM] / acc[:, HEAD_DIM:]).astype(bf16)

    @pl.when(qi < n_ctx_tiles)
    def _():
        attend([(0, ctx_len)])

    @pl.when(qi >= n_ctx_tiles)
    def _():
        attend([(0, ctx_len + tk)] + [(ctx_len + c * tk, tk) for c in range(1, n_lat_chunks)])


def _attention(bound, q, k, v, w_gate, w_up, w_down, rows, bounded):
    t_all, d = q.shape
    tpb = rows.tiles_per_batch
    rows_per_batch = t_all // rows.n_batch
    gw = Q_PER_KV * HEAD_DIM
    tk = min(ATTN_TK, rows.seq_len)
    assert rows.seq_len % tk == 0

    n_steps = rows.n_batch * N_KV_HEADS * tpb
    wg2 = w_gate.reshape(-1, w_gate.shape[-1])
    wu2 = w_up.reshape(-1, w_up.shape[-1])
    wd2 = w_down.reshape(-1, w_down.shape[-1])
    n_slabs = 1 << (n_steps.bit_length() - 1)
    slab_g = wg2.shape[0] // n_slabs
    slab_d = wd2.shape[0] // n_slabs
    assert wg2.shape[0] == n_slabs * slab_g and wd2.shape[0] == n_slabs * slab_d and wu2.shape == wg2.shape
    assert slab_g % SUBLANES == 0 and slab_d % SUBLANES == 0

    def slab(b, g, i):
        return (jnp.minimum((b * N_KV_HEADS + g) * tpb + i, n_slabs - 1), 0)

    kern = functools.partial(_attn_kernel, ctx_len=rows.ctx_len, tk=tk, rs=ATTN_ROWS,
                             n_ctx_tiles=rows.n_ctx_tiles, bounded=bounded)
    o, wg_bf, wu_bf, wd_bf = pl.pallas_call(
        kern,
        out_shape=(jax.ShapeDtypeStruct((t_all, d), bf16),
                   jax.ShapeDtypeStruct(wg2.shape, bf16),
                   jax.ShapeDtypeStruct(wu2.shape, bf16),
                   jax.ShapeDtypeStruct(wd2.shape, bf16)),
        grid=(rows.n_batch, N_KV_HEADS, tpb),
        in_specs=[
            pl.BlockSpec((1, 1), lambda b, g, i: (0, 0), memory_space=pltpu.SMEM),
            pl.BlockSpec((ROW_TILE, gw), lambda b, g, i: (b * tpb + i, g)),
            pl.BlockSpec((rows_per_batch, HEAD_DIM), lambda b, g, i: (b, g)),
            pl.BlockSpec((rows_per_batch, 2 * HEAD_DIM), lambda b, g, i: (b, g)),
            pl.BlockSpec((slab_g, wg2.shape[1]), slab),
            pl.BlockSpec((slab_g, wu2.shape[1]), slab),
            pl.BlockSpec((slab_d, wd2.shape[1]), slab),
        ],
        out_specs=(pl.BlockSpec((ROW_TILE, gw), lambda b, g, i: (b * tpb + i, g)),
                   pl.BlockSpec((slab_g, wg2.shape[1]), slab),
                   pl.BlockSpec((slab_g, wu2.shape[1]), slab),
                   pl.BlockSpec((slab_d, wd2.shape[1]), slab)),
        compiler_params=_cparams("arbitrary", "arbitrary", "arbitrary"),
        name="gqa_attention_bounded" if bounded else "gqa_attention_online",
    )(bound, q, k, v, wg2, wu2, wd2)
    return o, wg_bf.reshape(w_gate.shape), wu_bf.reshape(w_up.shape), wd_bf.reshape(w_down.shape)


def _post_epilogue(x, y, r0, mod_ref, lng_ref, lnb_ref, wr_ref, rb_ref, x1_ref, hp_ref, e_ref, g_ref):
    tm, d = x.shape
    rsl = slice(r0, r0 + tm)
    g1 = mod_ref[0, 2:3, :]
    sh2 = mod_ref[0, 3:4, :]
    sc2 = mod_ref[0, 4:5, :]
    z = DEEPNORM_ALPHA * x + g1 * y
    mu = jnp.mean(z, axis=-1, keepdims=True)
    zc = z - mu
    var = jnp.mean(zc * zc, axis=-1, keepdims=True)
    x1 = zc * lax.rsqrt(var + LN_EPS) * lng_ref[...] + lnb_ref[...]
    h = x1 * (1.0 + sc2) + sh2
    x1_ref[rsl, :] = x1

    h_hi = h.astype(bf16)
    h_hi32 = h_hi.astype(f32)
    bits = lax.bitcast_convert_type(h_hi32, u32)
    hp_ref[rsl, :] = (bits[:, :d // 2] >> 16) | bits[:, d // 2:]

    h_lo = (h - h_hi32).astype(bf16)
    wr = wr_ref[...]
    a2 = _nt_dot(wr, h_hi)
    lt = a2[:N_EXPERTS] + a2[N_EXPERTS:] + _nt_dot(wr[:N_EXPERTS], h_lo)
    aff = jax.nn.sigmoid(lt)
    sel = aff + rb_ref[...]
    ng = N_EXPERT_GROUPS
    a = [aff[j * ng:(j + 1) * ng] for j in range(EXPERTS_PER_GROUP)]
    s = [sel[j * ng:(j + 1) * ng] for j in range(EXPERTS_PER_GROUP)]
    gs = None
    for i in range(EXPERTS_PER_GROUP):
        for j in range(i + 1, EXPERTS_PER_GROUP):
            pij = s[i] + s[j]
            gs = pij if gs is None else jnp.maximum(gs, pij)
    gio = lax.broadcasted_iota(i32, gs.shape, 0)
    gmax = jnp.max(gs, axis=0, keepdims=True)
    g_idx = jnp.min(jnp.where(gs == gmax, gio, ng), axis=0, keepdims=True)
    onehot = gio == g_idx
    sin_g = [jnp.sum(jnp.where(onehot, s[j], 0.0), axis=0, keepdims=True) for j in range(EXPERTS_PER_GROUP)]
    aff_g = [jnp.sum(jnp.where(onehot, a[j], 0.0), axis=0, keepdims=True) for j in range(EXPERTS_PER_GROUP)]
    loc = [jnp.zeros_like(g_idx), jnp.zeros_like(g_idx)]
    gat = [jnp.zeros_like(gmax), jnp.zeros_like(gmax)]
    for j in range(EXPERTS_PER_GROUP):
        rank = jnp.zeros_like(g_idx)
        for i in range(EXPERTS_PER_GROUP):
            if i == j:
                continue
            ahead = (sin_g[i] >= sin_g[j]) if i < j else (sin_g[i] > sin_g[j])
            rank = rank + ahead.astype(i32)
        for kk in range(2):
            hit = rank == kk
            loc[kk] = jnp.where(hit, j, loc[kk])
            gat[kk] = jnp.where(hit, aff_g[j], gat[kk])
    den = gat[0] + gat[1]
    e_ref[0, :, rsl] = jnp.concatenate(
        [g_idx * EXPERTS_PER_GROUP + loc[0], g_idx * EXPERTS_PER_GROUP + loc[1]], axis=0)
    g8 = jnp.concatenate([gat[0] / den, gat[1] / den, jnp.zeros((GATE_COLS - 2, tm), f32)], axis=0)
    g_ref[rsl, :] = g8.T


def _attn_post_kernel(xc_ref, xl_ref, o_ref, wo_ref, mod_ref, lng_ref, lnb_ref, wr_ref, rb_ref,
                      x1_ref, hp_ref, e_ref, g_ref, ya_sc, yb_sc, *, rows):
    s = pl.program_id(0)
    tile = jnp.maximum(s - 1, 0)

    @pl.when(s == 0)
    def _():
        yb_sc[...] = jnp.zeros(yb_sc.shape, f32)

    def body(y_write, y_read):
        y_write[...] = jnp.dot(o_ref[...], wo_ref[...], preferred_element_type=f32)
        x = jnp.where(rows.is_ctx(tile), xc_ref[...], xl_ref[...])
        _post_epilogue(x, y_read[...], 0, mod_ref, lng_ref, lnb_ref, wr_ref, rb_ref,
                       x1_ref, hp_ref, e_ref, g_ref)

    @pl.when(s % 2 == 0)
    def _():
        body(ya_sc, yb_sc)

    @pl.when(s % 2 == 1)
    def _():
        body(yb_sc, ya_sc)


def _post_out(n_rows, d, tile_of):
    shapes = (jax.ShapeDtypeStruct((n_rows, d), f32),
              jax.ShapeDtypeStruct((n_rows, d // 2), u32),
              jax.ShapeDtypeStruct((n_rows // ROW_TILE, 2, ROW_TILE), i32),
              jax.ShapeDtypeStruct((n_rows, GATE_COLS), f32))
    specs = (pl.BlockSpec((ROW_TILE, d), lambda i: (tile_of(i), 0)),
             pl.BlockSpec((ROW_TILE, d // 2), lambda i: (tile_of(i), 0)),
             pl.BlockSpec((1, 2, ROW_TILE), lambda i: (tile_of(i), 0, 0)),
             pl.BlockSpec((ROW_TILE, GATE_COLS), lambda i: (tile_of(i), 0)))
    return shapes, specs


def _router_specs(d):
    return [pl.BlockSpec((2 * N_EXPERTS, d), lambda i: (0, 0)),
            pl.BlockSpec((N_EXPERTS, 1), lambda i: (0, 0))]


def _attn_post(ctx2, x2, o, w_o_bf, mod_l, ln_g, ln_b, wr_hl, rb_p, rows):
    d = x2.shape[1]
    n = rows.n_tiles

    def prev(i):
        return jnp.maximum(i - 1, 0)

    shapes, specs = _post_out(rows.t_all, d, prev)
    return pl.pallas_call(
        functools.partial(_attn_post_kernel, rows=rows),
        out_shape=shapes,
        grid=(n + 1,),
        in_specs=[
            pl.BlockSpec((ROW_TILE, d), lambda i: (rows.ctx_tile(prev(i)), 0)),
            pl.BlockSpec((ROW_TILE, d), lambda i: (rows.lat_tile(prev(i)), 0)),
            pl.BlockSpec((ROW_TILE, d), lambda i: (jnp.minimum(i, n - 1), 0)),
            pl.BlockSpec((d, d), lambda i: (0, 0)),
            pl.BlockSpec((1, 6, d), lambda i: (rows.seg(prev(i)), 0, 0)),
            pl.BlockSpec((1, d), lambda i: (0, 0)),
            pl.BlockSpec((1, d), lambda i: (0, 0)),
        ] + _router_specs(d),
        out_specs=specs,
        scratch_shapes=[pltpu.VMEM((ROW_TILE, d), f32), pltpu.VMEM((ROW_TILE, d), f32)],
        compiler_params=_cparams("arbitrary"),
        name="attn_out_norm_route",
    )(ctx2, x2, o, w_o_bf, mod_l, ln_g, ln_b, wr_hl, rb_p)


def _pool_post_kernel(x_ref, xp_ref, xn_ref, wp_ref, ps_ref, mod_ref, lng_ref, lnb_ref, wr_ref, rb_ref,
                      x1_ref, hp_ref, e_ref, g_ref, *, seq_len, tiles_per_seq):
    tm, d = x_ref.shape
    gd = d // len(POOL_WINDOWS)
    r = pl.program_id(0) % tiles_per_seq
    sh1 = mod_ref[0, 0:1, :]
    sc1 = mod_ref[0, 1:2, :]
    x = x_ref[...]
    ext = jnp.concatenate([xp_ref[...], x, xn_ref[...]], axis=0)
    n_ext = tm + 2 * POOL_HALO
    pos = r * tm - POOL_HALO + lax.broadcasted_iota(i32, (n_ext, 1), 0)
    inside = (pos >= 0) & (pos < seq_len)
    u_ext = jnp.where(inside, ext * (1.0 + sc1) + sh1, 0.0)
    t = pos[POOL_HALO:POOL_HALO + tm]
    dlts = []
    for g, w in enumerate(POOL_WINDOWS):
        e = u_ext[:, g * gd:(g + 1) * gd]
        p = e + pltpu.roll(e, 1, 0)
        step = 1
        while 2 * step < w:
            p = pltpu.roll(p, step, 0) + pltpu.roll(p, n_ext - step, 0)
            step *= 2
        cnt = (jnp.clip(t + w // 2, 0, seq_len) - jnp.clip(t - w // 2, 0, seq_len)).astype(f32)
        mean = p[POOL_HALO:POOL_HALO + tm] / cnt
        dlts.append((mean - e[POOL_HALO:POOL_HALO + tm]).astype(bf16))
    for r0 in range(0, tm, POST_ROWS):
        rsl = slice(r0, r0 + POST_ROWS)
        y = jnp.concatenate(
            [jnp.dot(dlts[g][rsl, :], wp_ref[g], preferred_element_type=f32) for g in range(len(POOL_WINDOWS))],
            axis=-1) * ps_ref[...]
        _post_epilogue(x[rsl, :], y, r0, mod_ref, lng_ref, lnb_ref, wr_ref, rb_ref, x1_ref, hp_ref, e_ref, g_ref)


def _pool_post(x_all, w_pool_bf, pool_scale, mod_l, ln_g, ln_b, wr_hl, rb_p, rows):
    t_all, d = x_all.shape
    tiles_per_seq = rows.n_seq_tiles
    n_rows = rows.n_batch * rows.seq_len
    halo_per_tile = ROW_TILE // POOL_HALO
    last_halo = t_all // POOL_HALO - 1
    ng = len(POOL_WINDOWS)

    def src_tile(i):
        return (i // tiles_per_seq) * rows.tiles_per_batch + rows.n_ctx_tiles + i % tiles_per_seq

    shapes, specs = _post_out(n_rows, d, lambda i: i)
    kern = functools.partial(_pool_post_kernel, seq_len=rows.seq_len, tiles_per_seq=tiles_per_seq)
    return pl.pallas_call(
        kern,
        out_shape=shapes,
        grid=(n_rows // ROW_TILE,),
        in_specs=[
            pl.BlockSpec((ROW_TILE, d), lambda i: (src_tile(i), 0)),
            pl.BlockSpec((POOL_HALO, d), lambda i: (jnp.maximum(src_tile(i) * halo_per_tile - 1, 0), 0)),
            pl.BlockSpec((POOL_HALO, d), lambda i: (jnp.minimum((src_tile(i) + 1) * halo_per_tile, last_halo), 0)),
            pl.BlockSpec((ng, d // ng, d // ng), lambda i: (0, 0, 0)),
            pl.BlockSpec((1, d), lambda i: (0, 0)),
            pl.BlockSpec((1, 6, d), lambda i: (1 + i // tiles_per_seq, 0, 0)),
            pl.BlockSpec((1, d), lambda i: (0, 0)),
            pl.BlockSpec((1, d), lambda i: (0, 0)),
        ] + _router_specs(d),
        out_specs=specs,
        compiler_params=_cparams("arbitrary"),
        name="pool_norm_route",
    )(x_all, x_all, x_all, w_pool_bf, pool_scale, mod_l, ln_g, ln_b, wr_hl, rb_p)


def _plan_kernel(e_ref, dest_ref, be_ref, nu_ref, carry_sc, start_sc):
    phase = pl.program_id(0)
    i = pl.program_id(1)
    tm = e_ref.shape[2]
    sub = lax.broadcasted_iota(i32, (N_EXPERTS, tm), 0)
    shift = MOE_BLOCK.bit_length() - 1

    @pl.when((phase == 0) & (i == 0))
    def _():
        carry_sc[...] = jnp.zeros(carry_sc.shape, f32)

    @pl.when(phase == 0)
    def _():
        carry = carry_sc[...]
        for kk in range(2):
            hit = sub == e_ref[0, kk:kk + 1, :]
            carry = carry + jnp.sum(jnp.where(hit, 1.0, 0.0), axis=1, keepdims=True)
        carry_sc[...] = carry

    @pl.when((phase == 1) & (i == 0))
    def _():
        counts = carry_sc[...].astype(i32)
        padded = ((counts + (MOE_BLOCK - 1)) >> shift) << shift
        esub = lax.broadcasted_iota(i32, padded.shape, 0)
        used = jnp.where(padded > 0, 1, 0)
        start = jnp.zeros(padded.shape, i32)
        order = jnp.zeros(padded.shape, i32)
        for e in range(N_EXPERTS - 1):
            start = start + jnp.where(esub > e, padded[e:e + 1, :], 0)
            order = order + jnp.where(esub > e, used[e:e + 1, :], 0)
        ends = start + padded
        nxt = jnp.full(padded.shape, NO_EXPERT, i32)
        cur = jnp.full((1, LANES), NO_EXPERT, i32)
        for e in range(N_EXPERTS - 1, -1, -1):
            nxt = jnp.where(esub == e, cur, nxt)
            cur = jnp.where(used[e:e + 1, :] > 0, e, cur)
        nb_pad = be_ref.shape[1]
        rep = nb_pad // LANES
        lane = lax.broadcasted_iota(i32, (1, nb_pad), 1)
        be = jnp.zeros((1, nb_pad), i32)
        for e in range(N_EXPERTS):
            be = be + jnp.where(jnp.concatenate([ends[e:e + 1, :]] * rep, axis=1) <= lane * MOE_BLOCK, 1, 0)
        be = jnp.minimum(be, N_EXPERTS - 1)
        par_b = jnp.zeros((1, nb_pad), i32)
        nxt_b = jnp.zeros((1, nb_pad), i32)
        for e in range(N_EXPERTS):
            hit = be == e
            par_b = jnp.where(hit, jnp.concatenate([order[e:e + 1, :]] * rep, axis=1) & 1, par_b)
            nxt_b = jnp.where(hit, jnp.concatenate([nxt[e:e + 1, :]] * rep, axis=1), nxt_b)
        first_b = jnp.where((be != pltpu.roll(be, 1, 1)) | (lane == 0), 1, 0)
        be_ref[...] = be | (first_b << INFO_FIRST) | (par_b << INFO_SLOT) | (nxt_b << INFO_NEXT)
        nu_ref[...] = ends[N_EXPERTS - 1:N_EXPERTS, :] >> shift
        start_sc[...] = start.astype(f32)
        carry_sc[...] = jnp.zeros(carry_sc.shape, f32)

    @pl.when(phase == 1)
    def _():
        before = lax.broadcasted_iota(i32, (tm, tm), 0) < lax.broadcasted_iota(i32, (tm, tm), 1)
        tri = jnp.where(before, 1.0, 0.0).astype(bf16)
        carry = carry_sc[...]
        start = start_sc[...]
        out = []
        for kk in range(2):
            hit = sub == e_ref[0, kk:kk + 1, :]
            onehot = jnp.where(hit, 1.0, 0.0)
            cum = jnp.dot(onehot.astype(bf16), tri, preferred_element_type=f32)
            base = (carry + start)[:, 0:1]
            out.append(jnp.sum(jnp.where(hit, cum + base, 0.0), axis=0, keepdims=True))
            carry = carry + jnp.sum(onehot, axis=1, keepdims=True)
        dest_ref[0] = jnp.concatenate(out, axis=0).astype(i32)
        carry_sc[...] = carry


def _dispatch_plan(e3):
    n_tiles, _, tm = e3.shape
    n_blocks = (2 * n_tiles * tm + N_EXPERTS * (MOE_BLOCK - 1) + MOE_BLOCK - 1) // MOE_BLOCK
    nb_pad = (n_blocks + LANES - 1) // LANES * LANES
    dest3, be, nu = pl.pallas_call(
        _plan_kernel,
        out_shape=(jax.ShapeDtypeStruct(e3.shape, i32),
                   jax.ShapeDtypeStruct((1, nb_pad), i32),
                   jax.ShapeDtypeStruct((1, LANES), i32)),
        grid=(2, n_tiles),
        in_specs=[pl.BlockSpec((1, 2, tm), lambda p, i: (i, 0, 0))],
        out_specs=(pl.BlockSpec((1, 2, tm), lambda p, i: (p * i, 0, 0)),
                   pl.BlockSpec((1, nb_pad), lambda p, i: (0, 0)),
                   pl.BlockSpec((1, LANES), lambda p, i: (0, 0))),
        scratch_shapes=[pltpu.VMEM((N_EXPERTS, LANES), f32), pltpu.VMEM((N_EXPERTS, LANES), f32)],
        compiler_params=_cparams("arbitrary", "arbitrary"),
        name="route_plan",
    )(e3)
    return be[0, :n_blocks], nu[0, :1], dest3, n_blocks


def _dispatch_kernel(dest_ref, hp_ref, xs_in, xs_hbm, buf, sem):
    del xs_in
    i = pl.program_id(0)
    n = pl.num_programs(0)
    tm = hp_ref.shape[0]
    slot = i % 2
    buf[slot] = hp_ref[...]
    for kk in range(2):
        for r in range(tm):
            pltpu.make_async_copy(
                buf.at[slot, pl.ds(r, 1), :], xs_hbm.at[pl.ds(dest_ref[0, kk, r], 1), :], sem.at[slot]).start()

    def drain(s):
        for _ in range(2 * tm):
            pltpu.make_async_copy(buf.at[s, pl.ds(0, 1), :], xs_hbm.at[pl.ds(0, 1), :], sem.at[s]).wait()

    @pl.when(i > 0)
    def _():
        drain(1 - slot)

    @pl.when(i == n - 1)
    def _():
        drain(slot)


def _dispatch(hp, dest3, n_blocks):
    t, dw = hp.shape
    n_tiles = t // ROW_TILE
    xs0 = jnp.zeros((n_blocks * MOE_BLOCK, dw), u32)
    return pl.pallas_call(
        _dispatch_kernel,
        out_shape=jax.ShapeDtypeStruct(xs0.shape, u32),
        grid=(n_tiles,),
        in_specs=[
            pl.BlockSpec((1, 2, ROW_TILE), lambda i: (i, 0, 0), memory_space=pltpu.SMEM),
            pl.BlockSpec((ROW_TILE, dw), lambda i: (i, 0)),
            pl.BlockSpec(memory_space=pl.ANY),
        ],
        out_specs=pl.BlockSpec(memory_space=pl.ANY),
        scratch_shapes=[pltpu.VMEM((2, ROW_TILE, dw), u32), pltpu.SemaphoreType.DMA((2,))],
        input_output_aliases={2: 0},
        compiler_params=_cparams("arbitrary"),
        name="moe_dispatch",
    )(dest3, hp, xs0)


def _ffn_kernel(info_ref, nu_ref, xs_ref, wg_hbm, wu_hbm, wd_hbm, y_ref, wg_sc, wu_sc, wd_sc, sem, *, layer):
    b = pl.program_id(0)
    info = info_ref[b]
    expert = info & ((1 << INFO_FIRST) - 1)
    first = (info >> INFO_FIRST) & 1
    slot = (info >> INFO_SLOT) & 1
    nxt = info >> INFO_NEXT
    live = b < nu_ref[0]

    def weight_copies(e, s):
        return [pltpu.make_async_copy(src.at[layer, e], dst.at[s], sem.at[s, j])
                for j, (src, dst) in enumerate(((wg_hbm, wg_sc), (wu_hbm, wu_sc), (wd_hbm, wd_sc)))]

    @pl.when(b == 0)
    def _():
        for c in weight_copies(expert, 0):
            c.start()

    @pl.when(live & (first == 1) & (nxt != NO_EXPERT))
    def _():
        for c in weight_copies(nxt, 1 - slot):
            c.start()

    @pl.when(live & (first == 1))
    def _():
        for c in weight_copies(expert, slot):
            c.wait()

    def compute(s):
        w = xs_ref[...]
        lo = lax.bitcast_convert_type(w << 16, f32)
        hi = lax.bitcast_convert_type(w & jnp.uint32(0xFFFF0000), f32)
        xb = jnp.concatenate([lo, hi], axis=1).astype(bf16)
        a = jnp.dot(xb, wg_sc[s], preferred_element_type=f32)
        u = jnp.dot(xb, wu_sc[s], preferred_element_type=f32)
        z = (a * jax.nn.sigmoid(a) * u).astype(bf16)
        y_ref[...] = jnp.dot(z, wd_sc[s], preferred_element_type=f32)

    for s in range(2):
        @pl.when(live & (slot == s))
        def _():
            compute(s)

    @pl.when(jnp.logical_not(live))
    def _():
        y_ref[...] = jnp.zeros(y_ref.shape, f32)


def _expert_ffn(xs, blk_expert, n_used, layer, wg, wu, wd):
    slots, dw = xs.shape
    d = 2 * dw
    de = wg.shape[3]
    n_blocks = slots // MOE_BLOCK
    grid_spec = pltpu.PrefetchScalarGridSpec(
        num_scalar_prefetch=2,
        grid=(n_blocks,),
        in_specs=[
            pl.BlockSpec((MOE_BLOCK, dw), lambda b, be, nu: (b, 0)),
            pl.BlockSpec(memory_space=pl.ANY),
            pl.BlockSpec(memory_space=pl.ANY),
            pl.BlockSpec(memory_space=pl.ANY),
        ],
        out_specs=pl.BlockSpec((MOE_BLOCK, d), lambda b, be, nu: (b, 0)),
        scratch_shapes=[pltpu.VMEM((2, d, de), bf16), pltpu.VMEM((2, d, de), bf16), pltpu.VMEM((2, de, d), bf16),
                        pltpu.SemaphoreType.DMA((2, 3))],
    )
    return pl.pallas_call(
        functools.partial(_ffn_kernel, layer=layer),
        out_shape=jax.ShapeDtypeStruct((slots, d), f32),
        grid_spec=grid_spec,
        compiler_params=_cparams("arbitrary"),
        name="expert_ffn",
    )(blk_expert, n_used, xs, wg, wu, wd)


def _fin_kernel(pos_cur, pos_nxt, x1_ref, gate_ref, mod_ref, lng_ref, lnb_ref, y_hbm, o_ref, ybuf, sem):
    i = pl.program_id(0)
    n = pl.num_programs(0)
    tm = x1_ref.shape[0]
    slot = i % 2

    def copy(pos_ref, kk, r, s):
        return pltpu.make_async_copy(
            y_hbm.at[pl.ds(pos_ref[0, kk, r], 1), :], ybuf.at[s, kk, pl.ds(r, 1), :], sem.at[s])

    def start_gather(pos_ref, s):
        for kk in range(2):
            for r in range(tm):
                copy(pos_ref, kk, r, s).start()

    @pl.when(i == 0)
    def _():
        start_gather(pos_cur, 0)

    @pl.when(i + 1 < n)
    def _():
        start_gather(pos_nxt, 1 - slot)

    for kk in range(2):
        for r in range(tm):
            copy(pos_cur, kk, r, slot).wait()
    g2 = mod_ref[0, 5:6, :]
    gate = gate_ref[...]
    f = gate[:, 0:1] * ybuf[slot, 0] + gate[:, 1:2] * ybuf[slot, 1]
    z = DEEPNORM_ALPHA * x1_ref[...] + g2 * f
    mu = jnp.mean(z, axis=-1, keepdims=True)
    zc = z - mu
    var = jnp.mean(zc * zc, axis=-1, keepdims=True)
    o_ref[...] = zc * lax.rsqrt(var + LN_EPS) * lng_ref[...] + lnb_ref[...]


def _combine_norm(x1, gate, pos3, y_sorted, mod_l, ln_g, ln_b, seg_of_tile):
    t, d = x1.shape
    n_tiles = t // ROW_TILE
    return pl.pallas_call(
        _fin_kernel,
        out_shape=jax.ShapeDtypeStruct((t, d), f32),
        grid=(n_tiles,),
        in_specs=[
            pl.BlockSpec((1, 2, ROW_TILE), lambda i: (i, 0, 0), memory_space=pltpu.SMEM),
            pl.BlockSpec((1, 2, ROW_TILE), lambda i: (jnp.minimum(i + 1, n_tiles - 1), 0, 0),
                         memory_space=pltpu.SMEM),
            pl.BlockSpec((ROW_TILE, d), lambda i: (i, 0)),
            pl.BlockSpec((ROW_TILE, GATE_COLS), lambda i: (i, 0)),
            pl.BlockSpec((1, 6, d), lambda i: (seg_of_tile(i), 0, 0)),
            pl.BlockSpec((1, d), lambda i: (0, 0)),
            pl.BlockSpec((1, d), lambda i: (0, 0)),
            pl.BlockSpec(memory_space=pl.ANY),
        ],
        out_specs=pl.BlockSpec((ROW_TILE, d), lambda i: (i, 0)),
        scratch_shapes=[pltpu.VMEM((2, 2, ROW_TILE, d), f32), pltpu.SemaphoreType.DMA((2,))],
        compiler_params=_cparams("arbitrary"),
        name="combine_norm",
    )(pos3, pos3, x1, gate, mod_l, ln_g, ln_b, y_sorted)


def _moe_combine(x1, hp, e_idx, gate, layer, wg_bf, wu_bf, wd_bf, mod_l, ln_g, ln_b, seg_of_tile):
    blk_expert, n_used, dest3, n_blocks = _dispatch_plan(e_idx)
    xs = _dispatch(hp, dest3, n_blocks)
    y_sorted = _expert_ffn(xs, blk_expert, n_used, layer, wg_bf, wu_bf, wd_bf)
    return _combine_norm(x1, gate, dest3, y_sorted, mod_l, ln_g, ln_b, seg_of_tile)


def _rope_tables(ctx_len, seq_len):
    t = jnp.arange(seq_len)
    row = (t // GRID_W).astype(f32)
    col = (t % GRID_W).astype(f32)
    inv_freq = ROPE_THETA ** (-jnp.arange(ROPE_PAIRS, dtype=f32) / ROPE_PAIRS)
    ang_r = row[:, None] * inv_freq
    ang_c = col[:, None] * inv_freq
    ang = jnp.concatenate([ang_r, ang_r, ang_c, ang_c], axis=-1)
    cos = jnp.cos(ang)
    sin = jnp.sin(ang)
    lane = jnp.arange(HEAD_DIM)
    sign = jnp.where((lane % (2 * ROPE_PAIRS)) < ROPE_PAIRS, -1.0, 1.0).astype(f32)
    cos_t = jnp.concatenate([jnp.ones((ctx_len, HEAD_DIM), f32), cos], axis=0)
    sin_t = jnp.concatenate([jnp.zeros((ctx_len, HEAD_DIM), f32), sin * sign], axis=0)
    return cos_t, sin_t


def kernel(x, c, ctx, c_ctx, w_mod, b_mod, ln_g, ln_b, w_qkv, q_gain, k_gain, w_o, w_pool, pool_scale,
           w_router, router_bias, w_gate, w_up, w_down):
    n_batch, seq_len, d = x.shape
    ctx_len = ctx.shape[1]
    assert w_mod.shape[0] == DEPTH and d == N_HEADS * HEAD_DIM
    assert ctx_len % ROW_TILE == 0 and seq_len % ROW_TILE == 0 and 1 + n_batch <= MOD_ROWS
    rows = _Rows(n_batch, ctx_len, seq_len)

    cond = jnp.concatenate([c_ctx[None, :], c], axis=0)
    mod = _modulation(cond, w_mod, b_mod).reshape(DEPTH, MOD_ROWS, 6, d)

    wr_t = w_router.reshape(d, N_EXPERT_GROUPS, EXPERTS_PER_GROUP).transpose(2, 1, 0).reshape(N_EXPERTS, d)
    wr_hi = wr_t.astype(bf16)
    wr_hl = jnp.concatenate([wr_hi, (wr_t - wr_hi.astype(f32)).astype(bf16)], axis=0)
    rb_p = router_bias.reshape(N_EXPERT_GROUPS, EXPERTS_PER_GROUP).T.reshape(N_EXPERTS, 1)
    cos_t, sin_t = _rope_tables(ctx_len, seq_len)

    ctx2 = ctx.reshape(n_batch * ctx_len, d)
    x2 = x.reshape(n_batch * seq_len, d)
    q, k, v = _qkv_proj(ctx2, x2, mod[0], w_qkv[0].astype(bf16), q_gain[0:1], k_gain[0:1], cos_t, sin_t, rows)
    bound = (Q_SCALE * HEAD_DIM * SCORE_BOUND_SLACK) * jnp.max(jnp.abs(q_gain[0])) * jnp.max(jnp.abs(k_gain[0]))
    attn_args = (bound.reshape(1, 1), q, k, v, w_gate, w_up, w_down)
    o, wg_bf, wu_bf, wd_bf = lax.cond(
        bound <= SCORE_BOUND_MAX,
        lambda a: _attention(*a, rows, True),
        lambda a: _attention(*a, rows, False),
        attn_args)
    x1, hp, e_idx, gate = _attn_post(ctx2, x2, o, w_o[0].astype(bf16), mod[0], ln_g[0, 0:1], ln_b[0, 0:1],
                                     wr_hl, rb_p, rows)
    x_all = _moe_combine(x1, hp, e_idx, gate, 0, wg_bf, wu_bf, wd_bf, mod[0], ln_g[0, 1:2], ln_b[0, 1:2],
                         rows.seg)

    x1, hp, e_idx, gate = _pool_post(x_all, w_pool[0].astype(bf16), pool_scale[0:1], mod[1],
                                     ln_g[1, 0:1], ln_b[1, 0:1], wr_hl, rb_p, rows)
    out = _moe_combine(x1, hp, e_idx, gate, 1, wg_bf, wu_bf, wd_bf, mod[1], ln_g[1, 1:2], ln_b[1, 1:2],
                       lambda i: 1 + i // rows.n_seq_tiles)
    return out.reshape(n_batch, seq_len, d)
```

```python
import functools

import jax
import jax.numpy as jnp
from jax import lax
from jax.experimental import pallas as pl
from jax.experimental.pallas import tpu as pltpu

N_HEADS = 16
N_KV_HEADS = 4
HEAD_DIM = 128
Q_PER_KV = N_HEADS // N_KV_HEADS
GRID_W = 64
ROPE_THETA = 10000.0
ROPE_PAIRS = HEAD_DIM // 4
ATTN_SCALE = HEAD_DIM ** -0.5
Q_SCALE = ATTN_SCALE * 1.4426950408889634
POOL_WINDOWS = (2, 4, 8, 16)
POOL_HALO = 8
N_EXPERTS = 32
N_EXPERT_GROUPS = 8
EXPERTS_PER_GROUP = N_EXPERTS // N_EXPERT_GROUPS
DEPTH = 2
DEEPNORM_ALPHA = (2 * DEPTH) ** 0.25
LN_EPS = 1e-6
RMS_EPS = 1e-6

LANES = 128
SUBLANES = 8
ROW_TILE = 256
ATTN_TK = 2048
ATTN_ROWS = 128
SCORE_BOUND_SLACK = 1.02
SCORE_BOUND_MAX = 56.0
POST_ROWS = 128
MOE_BLOCK = 256
NO_EXPERT = N_EXPERTS
INFO_FIRST, INFO_SLOT, INFO_NEXT = 8, 9, 10
GATE_COLS = 8
MOD_ROWS = 8
MOD_TN = 512
VMEM_LIMIT = 56 * 1024 * 1024

f32 = jnp.float32
bf16 = jnp.bfloat16
u32 = jnp.uint32
i32 = jnp.int32


def _cparams(*sem, flags=None):
    return pltpu.CompilerParams(dimension_semantics=sem, vmem_limit_bytes=VMEM_LIMIT, flags=flags)


def _nt_dot(a, b):
    return lax.dot_general(a, b, (((1,), (1,)), ((), ())), preferred_element_type=f32)


def _mod_kernel(c_ref, w_ref, b_ref, o_ref, s_sc, *, n_rows):
    d = w_ref.shape[1]
    tn = w_ref.shape[2]
    nj = tn // LANES

    @pl.when((pl.program_id(0) == 0) & (pl.program_id(1) == 0))
    def _():
        c = c_ref[...]
        s_sc[...] = c * jax.nn.sigmoid(c)

    def body(k, accs):
        r0 = pl.multiple_of(k * SUBLANES, SUBLANES)
        w8 = w_ref[0, pl.ds(r0, SUBLANES), :]
        out = []
        for r in range(n_rows):
            s8 = s_sc[r, pl.ds(r0, SUBLANES), :]
            for j in range(nj):
                out.append(accs[r * nj + j] + s8 * w8[:, j * LANES:(j + 1) * LANES])
        return tuple(out)

    init = tuple(jnp.zeros((SUBLANES, LANES), f32) for _ in range(n_rows * nj))
    accs = lax.fori_loop(0, d // SUBLANES, body, init, unroll=4)
    rows = []
    for r in range(n_rows):
        row = jnp.concatenate(
            [jnp.sum(accs[r * nj + j], axis=0, keepdims=True) for j in range(nj)], axis=1)
        rows.append(row + b_ref[0])
    rows.append(jnp.zeros((MOD_ROWS - n_rows, tn), f32))
    o_ref[0] = jnp.concatenate(rows, axis=0)


def _modulation(cond, w_mod, b_mod):
    n_rows, d = cond.shape
    depth, _, n6 = w_mod.shape
    c_b = jnp.broadcast_to(cond[:, :, None], (n_rows, d, LANES))
    return pl.pallas_call(
        functools.partial(_mod_kernel, n_rows=n_rows),
        out_shape=jax.ShapeDtypeStruct((depth, MOD_ROWS, n6), f32),
        grid=(depth, n6 // MOD_TN),
        in_specs=[
            pl.BlockSpec((n_rows, d, LANES), lambda l, j: (0, 0, 0)),
            pl.BlockSpec((1, d, MOD_TN), lambda l, j: (l, 0, j)),
            pl.BlockSpec((1, 1, MOD_TN), lambda l, j: (l, 0, j)),
        ],
        out_specs=pl.BlockSpec((1, MOD_ROWS, MOD_TN), lambda l, j: (l, 0, j)),
        scratch_shapes=[pltpu.VMEM((n_rows, d, LANES), f32)],
        compiler_params=_cparams("arbitrary", "arbitrary"),
        name="adaln_mod",
    )(c_b, w_mod, b_mod.reshape(depth, 1, n6))


class _Rows:
    def __init__(self, n_batch, ctx_len, seq_len):
        self.n_batch = n_batch
        self.ctx_len = ctx_len
        self.seq_len = seq_len
        self.n_ctx_tiles = ctx_len // ROW_TILE
        self.n_seq_tiles = seq_len // ROW_TILE
        self.tiles_per_batch = self.n_ctx_tiles + self.n_seq_tiles
        self.n_tiles = n_batch * self.tiles_per_batch
        self.t_all = self.n_tiles * ROW_TILE

    def is_ctx(self, i):
        return i % self.tiles_per_batch < self.n_ctx_tiles

    def seg(self, i):
        return jnp.where(self.is_ctx(i), 0, 1 + i // self.tiles_per_batch)

    def ctx_tile(self, i):
        r = i % self.tiles_per_batch
        return (i // self.tiles_per_batch) * self.n_ctx_tiles + jnp.minimum(r, self.n_ctx_tiles - 1)

    def lat_tile(self, i):
        r = i % self.tiles_per_batch
        return (i // self.tiles_per_batch) * self.n_seq_tiles + jnp.maximum(r - self.n_ctx_tiles, 0)


def _qkv_kernel(xc_ref, xl_ref, mod_ref, w_ref, qg_ref, kg_ref, cos_ref, sin_ref, q_ref, k_ref, v_ref, *, rows):
    tm, d = xl_ref.shape
    kv_dim = k_ref.shape[1]
    x = jnp.where(rows.is_ctx(pl.program_id(0)), xc_ref[...], xl_ref[...])
    sh1 = mod_ref[0, 0:1, :]
    sc1 = mod_ref[0, 1:2, :]
    u = (x * (1.0 + sc1) + sh1).astype(bf16)
    qkv = jnp.dot(u, w_ref[...], preferred_element_type=f32)
    cos = cos_ref[...]
    sin = sin_ref[...]
    lane = lax.broadcasted_iota(i32, (tm, HEAD_DIM), 1)
    first = (lane % (2 * ROPE_PAIRS)) < ROPE_PAIRS

    def norm_rope(xh, gain, scale):
        ms = jnp.mean(xh * xh, axis=-1, keepdims=True)
        xn = xh * lax.rsqrt(ms + RMS_EPS) * gain
        rot = jnp.where(first, pltpu.roll(xn, HEAD_DIM - ROPE_PAIRS, 1), pltpu.roll(xn, ROPE_PAIRS, 1))
        y = xn * cos + rot * sin
        return y * scale if scale != 1.0 else y

    qg = qg_ref[...]
    kg = kg_ref[...]
    for h in range(N_HEADS):
        sl = slice(h * HEAD_DIM, (h + 1) * HEAD_DIM)
        q_ref[:, sl] = norm_rope(qkv[:, sl], qg, Q_SCALE).astype(bf16)
    ones = jnp.ones((tm, HEAD_DIM), bf16)
    for h in range(N_KV_HEADS):
        sl = slice(h * HEAD_DIM, (h + 1) * HEAD_DIM)
        k_ref[:, sl] = norm_rope(qkv[:, d + h * HEAD_DIM:d + (h + 1) * HEAD_DIM], kg, 1.0).astype(bf16)
        v0 = d + kv_dim + h * HEAD_DIM
        v_ref[:, 2 * h * HEAD_DIM:(2 * h + 1) * HEAD_DIM] = qkv[:, v0:v0 + HEAD_DIM].astype(bf16)
        v_ref[:, (2 * h + 1) * HEAD_DIM:(2 * h + 2) * HEAD_DIM] = ones


def _qkv_proj(ctx2, x2, mod_l, w_qkv_bf, q_gain, k_gain, cos_t, sin_t, rows):
    d = x2.shape[1]
    kv_dim = N_KV_HEADS * HEAD_DIM
    tpb = rows.tiles_per_batch
    return pl.pallas_call(
        functools.partial(_qkv_kernel, rows=rows),
        out_shape=(jax.ShapeDtypeStruct((rows.t_all, d), bf16),
                   jax.ShapeDtypeStruct((rows.t_all, kv_dim), bf16),
                   jax.ShapeDtypeStruct((rows.t_all, 2 * kv_dim), bf16)),
        grid=(rows.n_tiles,),
        in_specs=[
            pl.BlockSpec((ROW_TILE, d), lambda i: (rows.ctx_tile(i), 0)),
            pl.BlockSpec((ROW_TILE, d), lambda i: (rows.lat_tile(i), 0)),
            pl.BlockSpec((1, 6, d), lambda i: (rows.seg(i), 0, 0)),
            pl.BlockSpec((d, d + 2 * kv_dim), lambda i: (0, 0)),
            pl.BlockSpec((1, HEAD_DIM), lambda i: (0, 0)),
            pl.BlockSpec((1, HEAD_DIM), lambda i: (0, 0)),
            pl.BlockSpec((ROW_TILE, HEAD_DIM), lambda i: (i % tpb, 0)),
            pl.BlockSpec((ROW_TILE, HEAD_DIM), lambda i: (i % tpb, 0)),
        ],
        out_specs=(pl.BlockSpec((ROW_TILE, d), lambda i: (i, 0)),
                   pl.BlockSpec((ROW_TILE, kv_dim), lambda i: (i, 0)),
                   pl.BlockSpec((ROW_TILE, 2 * kv_dim), lambda i: (i, 0))),
        compiler_params=_cparams("arbitrary"),
        name="qkv_rope",
    )(ctx2, x2, mod_l, w_qkv_bf, q_gain, k_gain, cos_t, sin_t)


def _attn_kernel(bound_ref, q_ref, k_ref, v_ref, wg_ref, wu_ref, wd_ref, o_ref, wgo_ref, wuo_ref, wdo_ref,
                 *, ctx_len, tk, rs, n_ctx_tiles, bounded):
    tq = q_ref.shape[0]
    n_lat_chunks = (k_ref.shape[0] - ctx_len) // tk
    qi = pl.program_id(2)
    subs = [(h, r0) for h in range(Q_PER_KV) for r0 in range(0, tq, rs)]

    def attend(chunks):
        states = [None] * len(subs)
        for ci, (off, size) in enumerate(chunks):
            for src, dst in ((wg_ref, wgo_ref), (wu_ref, wuo_ref), (wd_ref, wdo_ref)):
                n = src.shape[0] // len(chunks)
                dst[ci * n:(ci + 1) * n, :] = src[ci * n:(ci + 1) * n, :].astype(bf16)
            kc = k_ref[off:off + size, :]
            vc = v_ref[off:off + size, :]
            for i, (h, r0) in enumerate(subs):
                s = _nt_dot(q_ref[r0:r0 + rs, h * HEAD_DIM:(h + 1) * HEAD_DIM], kc)
                if bounded:
                    pv = jnp.dot(jnp.exp2(s - bound_ref[0, 0]).astype(bf16), vc, preferred_element_type=f32)
                    states[i] = pv if states[i] is None else states[i] + pv
                    continue
                mx = s[:, 0:LANES]
                for j in range(1, size // LANES):
                    mx = jnp.maximum(mx, s[:, j * LANES:(j + 1) * LANES])
                m_cur = jnp.max(mx, axis=-1, keepdims=True)
                if states[i] is None:
                    m_new = jnp.broadcast_to(m_cur, (rs, LANES))
                else:
                    m_prev, acc_prev = states[i]
                    m_new = jnp.maximum(m_prev, m_cur)
                p = jnp.concatenate(
                    [jnp.exp2(s[:, j * LANES:(j + 1) * LANES] - m_new) for j in range(size // LANES)],
                    axis=1).astype(bf16)
                acc = jnp.dot(p, vc, preferred_element_type=f32)
                if states[i] is not None:
                    alpha = jnp.exp2(m_prev - m_new)
                    acc = acc_prev * jnp.concatenate([alpha, alpha], axis=1) + acc
                states[i] = (m_new, acc)
        for i, (h, r0) in enumerate(subs):
            acc = states[i] if bounded else states[i][1]
            o_ref[r0:r0 + rs, h * HEAD_DIM:(h + 1) * HEAD_DIM] = (
                acc[:, :HEAD_DIM] / acc[:, HEAD_DIM:]).astype(bf16)

    @pl.when(qi < n_ctx_tiles)
    def _():
        attend([(0, ctx_len)])

    @pl.when(qi >= n_ctx_tiles)
    def _():
        attend([(0, ctx_len + tk)] + [(ctx_len + c * tk, tk) for c in range(1, n_lat_chunks)])


def _attention(bound, q, k, v, w_gate, w_up, w_down, rows, bounded):
    t_all, d = q.shape
    tpb = rows.tiles_per_batch
    rows_per_batch = t_all // rows.n_batch
    gw = Q_PER_KV * HEAD_DIM
    tk = min(ATTN_TK, rows.seq_len)
    assert rows.seq_len % tk == 0

    n_steps = rows.n_batch * N_KV_HEADS * tpb
    wg2 = w_gate.reshape(-1, w_gate.shape[-1])
    wu2 = w_up.reshape(-1, w_up.shape[-1])
    wd2 = w_down.reshape(-1, w_down.shape[-1])
    n_slabs = 1 << (n_steps.bit_length() - 1)
    slab_g = wg2.shape[0] // n_slabs
    slab_d = wd2.shape[0] // n_slabs
    assert wg2.shape[0] == n_slabs * slab_g and wd2.shape[0] == n_slabs * slab_d and wu2.shape == wg2.shape
    assert slab_g % SUBLANES == 0 and slab_d % SUBLANES == 0

    def slab(b, g, i):
        return (jnp.minimum((b * N_KV_HEADS + g) * tpb + i, n_slabs - 1), 0)

    kern = functools.partial(_attn_kernel, ctx_len=rows.ctx_len, tk=tk, rs=ATTN_ROWS,
                             n_ctx_tiles=rows.n_ctx_tiles, bounded=bounded)
    o, wg_bf, wu_bf, wd_bf = pl.pallas_call(
        kern,
        out_shape=(jax.ShapeDtypeStruct((t_all, d), bf16),
                   jax.ShapeDtypeStruct(wg2.shape, bf16),
                   jax.ShapeDtypeStruct(wu2.shape, bf16),
                   jax.ShapeDtypeStruct(wd2.shape, bf16)),
        grid=(rows.n_batch, N_KV_HEADS, tpb),
        in_specs=[
            pl.BlockSpec((1, 1), lambda b, g, i: (0, 0), memory_space=pltpu.SMEM),
            pl.BlockSpec((ROW_TILE, gw), lambda b, g, i: (b * tpb + i, g)),
            pl.BlockSpec((rows_per_batch, HEAD_DIM), lambda b, g, i: (b, g)),
            pl.BlockSpec((rows_per_batch, 2 * HEAD_DIM), lambda b, g, i: (b, g)),
            pl.BlockSpec((slab_g, wg2.shape[1]), slab),
            pl.BlockSpec((slab_g, wu2.shape[1]), slab),
            pl.BlockSpec((slab_d, wd2.shape[1]), slab),
        ],
        out_specs=(pl.BlockSpec((ROW_TILE, gw), lambda b, g, i: (b * tpb + i, g)),
                   pl.BlockSpec((slab_g, wg2.shape[1]), slab),
                   pl.BlockSpec((slab_g, wu2.shape[1]), slab),
                   pl.BlockSpec((slab_d, wd2.shape[1]), slab)),
        compiler_params=_cparams("arbitrary", "arbitrary", "arbitrary"),
        name="gqa_attention_bounded" if bounded else "gqa_attention_online",
    )(bound, q, k, v, wg2, wu2, wd2)
    return o, wg_bf.reshape(w_gate.shape), wu_bf.reshape(w_up.shape), wd_bf.reshape(w_down.shape)


def _post_epilogue(x, y, r0, mod_ref, lng_ref, lnb_ref, wr_ref, rb_ref, x1_ref, hp_ref, e_ref, g_ref):
    tm, d = x.shape
    rsl = slice(r0, r0 + tm)
    g1 = mod_ref[0, 2:3, :]
    sh2 = mod_ref[0, 3:4, :]
    sc2 = mod_ref[0, 4:5, :]
    z = DEEPNORM_ALPHA * x + g1 * y
    mu = jnp.mean(z, axis=-1, keepdims=True)
    zc = z - mu
    var = jnp.mean(zc * zc, axis=-1, keepdims=True)
    x1 = zc * lax.rsqrt(var + LN_EPS) * lng_ref[...] + lnb_ref[...]
    h = x1 * (1.0 + sc2) + sh2
    x1_ref[rsl, :] = x1

    h_hi = h.astype(bf16)
    h_hi32 = h_hi.astype(f32)
    bits = lax.bitcast_convert_type(h_hi32, u32)
    hp_ref[rsl, :] = (bits[:, :d // 2] >> 16) | bits[:, d // 2:]

    h_lo = (h - h_hi32).astype(bf16)
    wr = wr_ref[...]
    a2 = _nt_dot(wr, h_hi)
    lt = a2[:N_EXPERTS] + a2[N_EXPERTS:] + _nt_dot(wr[:N_EXPERTS], h_lo)
    aff = jax.nn.sigmoid(lt)
    sel = aff + rb_ref[...]
    ng = N_EXPERT_GROUPS
    a = [aff[j * ng:(j + 1) * ng] for j in range(EXPERTS_PER_GROUP)]
    s = [sel[j * ng:(j + 1) * ng] for j in range(EXPERTS_PER_GROUP)]
    gs = None
    for i in range(EXPERTS_PER_GROUP):
        for j in range(i + 1, EXPERTS_PER_GROUP):
            pij = s[i] + s[j]
            gs = pij if gs is None else jnp.maximum(gs, pij)
    gio = lax.broadcasted_iota(i32, gs.shape, 0)
    gmax = jnp.max(gs, axis=0, keepdims=True)
    g_idx = jnp.min(jnp.where(gs == gmax, gio, ng), axis=0, keepdims=True)
    onehot = gio == g_idx
    sin_g = [jnp.sum(jnp.where(onehot, s[j], 0.0), axis=0, keepdims=True) for j in range(EXPERTS_PER_GROUP)]
    aff_g = [jnp.sum(jnp.where(onehot, a[j], 0.0), axis=0, keepdims=True) for j in range(EXPERTS_PER_GROUP)]
    loc = [jnp.zeros_like(g_idx), jnp.zeros_like(g_idx)]
    gat = [jnp.zeros_like(gmax), jnp.zeros_like(gmax)]
    for j in range(EXPERTS_PER_GROUP):
        rank = jnp.zeros_like(g_idx)
        for i in range(EXPERTS_PER_GROUP):
            if i == j:
                continue
            ahead = (sin_g[i] >= sin_g[j]) if i < j else (sin_g[i] > sin_g[j])
            rank = rank + ahead.astype(i32)
        for kk in range(2):
            hit = rank == kk
            loc[kk] = jnp.where(hit, j, loc[kk])
            gat[kk] = jnp.where(hit, aff_g[j], gat[kk])
    den = gat[0] + gat[1]
    e_ref[0, :, rsl] = jnp.concatenate(
        [g_idx * EXPERTS_PER_GROUP + loc[0], g_idx * EXPERTS_PER_GROUP + loc[1]], axis=0)
    g8 = jnp.concatenate([gat[0] / den, gat[1] / den, jnp.zeros((GATE_COLS - 2, tm), f32)], axis=0)
    g_ref[rsl, :] = g8.T


def _attn_post_kernel(xc_ref, xl_ref, o_ref, wo_ref, mod_ref, lng_ref, lnb_ref, wr_ref, rb_ref,
                      x1_ref, hp_ref, e_ref, g_ref, ya_sc, yb_sc, *, rows):
    s = pl.program_id(0)
    tile = jnp.maximum(s - 1, 0)

    @pl.when(s == 0)
    def _():
        yb_sc[...] = jnp.zeros(yb_sc.shape, f32)

    def body(y_write, y_read):
        y_write[...] = jnp.dot(o_ref[...], wo_ref[...], preferred_element_type=f32)
        x = jnp.where(rows.is_ctx(tile), xc_ref[...], xl_ref[...])
        _post_epilogue(x, y_read[...], 0, mod_ref, lng_ref, lnb_ref, wr_ref, rb_ref,
                       x1_ref, hp_ref, e_ref, g_ref)

    @pl.when(s % 2 == 0)
    def _():
        body(ya_sc, yb_sc)

    @pl.when(s % 2 == 1)
    def _():
        body(yb_sc, ya_sc)


def _post_out(n_rows, d, tile_of):
    shapes = (jax.ShapeDtypeStruct((n_rows, d), f32),
              jax.ShapeDtypeStruct((n_rows, d // 2), u32),
              jax.ShapeDtypeStruct((n_rows // ROW_TILE, 2, ROW_TILE), i32),
              jax.ShapeDtypeStruct((n_rows, GATE_COLS), f32))
    specs = (pl.BlockSpec((ROW_TILE, d), lambda i: (tile_of(i), 0)),
             pl.BlockSpec((ROW_TILE, d // 2), lambda i: (tile_of(i), 0)),
             pl.BlockSpec((1, 2, ROW_TILE), lambda i: (tile_of(i), 0, 0)),
             pl.BlockSpec((ROW_TILE, GATE_COLS), lambda i: (tile_of(i), 0)))
    return shapes, specs


def _router_specs(d):
    return [pl.BlockSpec((2 * N_EXPERTS, d), lambda i: (0, 0)),
            pl.BlockSpec((N_EXPERTS, 1), lambda i: (0, 0))]


def _attn_post(ctx2, x2, o, w_o_bf, mod_l, ln_g, ln_b, wr_hl, rb_p, rows):
    d = x2.shape[1]
    n = rows.n_tiles

    def prev(i):
        return jnp.maximum(i - 1, 0)

    shapes, specs = _post_out(rows.t_all, d, prev)
    return pl.pallas_call(
        functools.partial(_attn_post_kernel, rows=rows),
        out_shape=shapes,
        grid=(n + 1,),
        in_specs=[
            pl.BlockSpec((ROW_TILE, d), lambda i: (rows.ctx_tile(prev(i)), 0)),
            pl.BlockSpec((ROW_TILE, d), lambda i: (rows.lat_tile(prev(i)), 0)),
            pl.BlockSpec((ROW_TILE, d), lambda i: (jnp.minimum(i, n - 1), 0)),
            pl.BlockSpec((d, d), lambda i: (0, 0)),
            pl.BlockSpec((1, 6, d), lambda i: (rows.seg(prev(i)), 0, 0)),
            pl.BlockSpec((1, d), lambda i: (0, 0)),
            pl.BlockSpec((1, d), lambda i: (0, 0)),
        ] + _router_specs(d),
        out_specs=specs,
        scratch_shapes=[pltpu.VMEM((ROW_TILE, d), f32), pltpu.VMEM((ROW_TILE, d), f32)],
        compiler_params=_cparams("arbitrary"),
        name="attn_out_norm_route",
    )(ctx2, x2, o, w_o_bf, mod_l, ln_g, ln_b, wr_hl, rb_p)


def _pool_post_kernel(x_ref, xp_ref, xn_ref, wp_ref, ps_ref, mod_ref, lng_ref, lnb_ref, wr_ref, rb_ref,
                      x1_ref, hp_ref, e_ref, g_ref, *, seq_len, tiles_per_seq):
    tm, d = x_ref.shape
    gd = d // len(POOL_WINDOWS)
    r = pl.program_id(0) % tiles_per_seq
    sh1 = mod_ref[0, 0:1, :]
    sc1 = mod_ref[0, 1:2, :]
    x = x_ref[...]
    ext = jnp.concatenate([xp_ref[...], x, xn_ref[...]], axis=0)
    n_ext = tm + 2 * POOL_HALO
    pos = r * tm - POOL_HALO + lax.broadcasted_iota(i32, (n_ext, 1), 0)
    inside = (pos >= 0) & (pos < seq_len)
    u_ext = jnp.where(inside, ext * (1.0 + sc1) + sh1, 0.0)
    t = pos[POOL_HALO:POOL_HALO + tm]
    dlts = []
    for g, w in enumerate(POOL_WINDOWS):
        e = u_ext[:, g * gd:(g + 1) * gd]
        p = e + pltpu.roll(e, 1, 0)
        step = 1
        while 2 * step < w:
            p = pltpu.roll(p, step, 0) + pltpu.roll(p, n_ext - step, 0)
            step *= 2
        cnt = (jnp.clip(t + w // 2, 0, seq_len) - jnp.clip(t - w // 2, 0, seq_len)).astype(f32)
        mean = p[POOL_HALO:POOL_HALO + tm] / cnt
        dlts.append((mean - e[POOL_HALO:POOL_HALO + tm]).astype(bf16))
    for r0 in range(0, tm, POST_ROWS):
        rsl = slice(r0, r0 + POST_ROWS)
        y = jnp.concatenate(
            [jnp.dot(dlts[g][rsl, :], wp_ref[g], preferred_element_type=f32) for g in range(len(POOL_WINDOWS))],
            axis=-1) * ps_ref[...]
        _post_epilogue(x[rsl, :], y, r0, mod_ref, lng_ref, lnb_ref, wr_ref, rb_ref, x1_ref, hp_ref, e_ref, g_ref)


def _pool_post(x_all, w_pool_bf, pool_scale, mod_l, ln_g, ln_b, wr_hl, rb_p, rows):
    t_all, d = x_all.shape
    tiles_per_seq = rows.n_seq_tiles
    n_rows = rows.n_batch * rows.seq_len
    halo_per_tile = ROW_TILE // POOL_HALO
    last_halo = t_all // POOL_HALO - 1
    ng = len(POOL_WINDOWS)

    def src_tile(i):
        return (i // tiles_per_seq) * rows.tiles_per_batch + rows.n_ctx_tiles + i % tiles_per_seq

    shapes, specs = _post_out(n_rows, d, lambda i: i)
    kern = functools.partial(_pool_post_kernel, seq_len=rows.seq_len, tiles_per_seq=tiles_per_seq)
    return pl.pallas_call(
        kern,
        out_shape=shapes,
        grid=(n_rows // ROW_TILE,),
        in_specs=[
            pl.BlockSpec((ROW_TILE, d), lambda i: (src_tile(i), 0)),
            pl.BlockSpec((POOL_HALO, d), lambda i: (jnp.maximum(src_tile(i) * halo_per_tile - 1, 0), 0)),
            pl.BlockSpec((POOL_HALO, d), lambda i: (jnp.minimum((src_tile(i) + 1) * halo_per_tile, last_halo), 0)),
            pl.BlockSpec((ng, d // ng, d // ng), lambda i: (0, 0, 0)),
            pl.BlockSpec((1, d), lambda i: (0, 0)),
            pl.BlockSpec((1, 6, d), lambda i: (1 + i // tiles_per_seq, 0, 0)),
            pl.BlockSpec((1, d), lambda i: (0, 0)),
            pl.BlockSpec((1, d), lambda i: (0, 0)),
        ] + _router_specs(d),
        out_specs=specs,
        compiler_params=_cparams("arbitrary"),
        name="pool_norm_route",
    )(x_all, x_all, x_all, w_pool_bf, pool_scale, mod_l, ln_g, ln_b, wr_hl, rb_p)


def _plan_kernel(e_ref, dest_ref, be_ref, nu_ref, carry_sc, start_sc):
    phase = pl.program_id(0)
    i = pl.program_id(1)
    tm = e_ref.shape[2]
    sub = lax.broadcasted_iota(i32, (N_EXPERTS, tm), 0)
    shift = MOE_BLOCK.bit_length() - 1

    @pl.when((phase == 0) & (i == 0))
    def _():
        carry_sc[...] = jnp.zeros(carry_sc.shape, f32)

    @pl.when(phase == 0)
    def _():
        carry = carry_sc[...]
        for kk in range(2):
            hit = sub == e_ref[0, kk:kk + 1, :]
            carry = carry + jnp.sum(jnp.where(hit, 1.0, 0.0), axis=1, keepdims=True)
        carry_sc[...] = carry

    @pl.when((phase == 1) & (i == 0))
    def _():
        counts = carry_sc[...].astype(i32)
        padded = ((counts + (MOE_BLOCK - 1)) >> shift) << shift
        esub = lax.broadcasted_iota(i32, padded.shape, 0)
        used = jnp.where(padded > 0, 1, 0)
        start = jnp.zeros(padded.shape, i32)
        order = jnp.zeros(padded.shape, i32)
        for e in range(N_EXPERTS - 1):
            start = start + jnp.where(esub > e, padded[e:e + 1, :], 0)
            order = order + jnp.where(esub > e, used[e:e + 1, :], 0)
        ends = start + padded
        nxt = jnp.full(padded.shape, NO_EXPERT, i32)
        cur = jnp.full((1, LANES), NO_EXPERT, i32)
        for e in range(N_EXPERTS - 1, -1, -1):
            nxt = jnp.where(esub == e, cur, nxt)
            cur = jnp.where(used[e:e + 1, :] > 0, e, cur)
        nb_pad = be_ref.shape[1]
        rep = nb_pad // LANES
        lane = lax.broadcasted_iota(i32, (1, nb_pad), 1)
        be = jnp.zeros((1, nb_pad), i32)
        for e in range(N_EXPERTS):
            be = be + jnp.where(jnp.concatenate([ends[e:e + 1, :]] * rep, axis=1) <= lane * MOE_BLOCK, 1, 0)
        be = jnp.minimum(be, N_EXPERTS - 1)
        par_b = jnp.zeros((1, nb_pad), i32)
        nxt_b = jnp.zeros((1, nb_pad), i32)
        for e in range(N_EXPERTS):
            hit = be == e
            par_b = jnp.where(hit, jnp.concatenate([order[e:e + 1, :]] * rep, axis=1) & 1, par_b)
            nxt_b = jnp.where(hit, jnp.concatenate([nxt[e:e + 1, :]] * rep, axis=1), nxt_b)
        first_b = jnp.where((be != pltpu.roll(be, 1, 1)) | (lane == 0), 1, 0)
        be_ref[...] = be | (first_b << INFO_FIRST) | (par_b << INFO_SLOT) | (nxt_b << INFO_NEXT)
        nu_ref[...] = ends[N_EXPERTS - 1:N_EXPERTS, :] >> shift
        start_sc[...] = start.astype(f32)
        carry_sc[...] = jnp.zeros(carry_sc.shape, f32)

    @pl.when(phase == 1)
    def _():
        before = lax.broadcasted_iota(i32, (tm, tm), 0) < lax.broadcasted_iota(i32, (tm, tm), 1)
        tri = jnp.where(before, 1.0, 0.0).astype(bf16)
        carry = carry_sc[...]
        start = start_sc[...]
        out = []
        for kk in range(2):
            hit = sub == e_ref[0, kk:kk + 1, :]
            onehot = jnp.where(hit, 1.0, 0.0)
            cum = jnp.dot(onehot.astype(bf16), tri, preferred_element_type=f32)
            base = (carry + start)[:, 0:1]
            out.append(jnp.sum(jnp.where(hit, cum + base, 0.0), axis=0, keepdims=True))
            carry = carry + jnp.sum(onehot, axis=1, keepdims=True)
        dest_ref[0] = jnp.concatenate(out, axis=0).astype(i32)
        carry_sc[...] = carry


def _dispatch_plan(e3):
    n_tiles, _, tm = e3.shape
    n_blocks = (2 * n_tiles * tm + N_EXPERTS * (MOE_BLOCK - 1) + MOE_BLOCK - 1) // MOE_BLOCK
    nb_pad = (n_blocks + LANES - 1) // LANES * LANES
    dest3, be, nu = pl.pallas_call(
        _plan_kernel,
        out_shape=(jax.ShapeDtypeStruct(e3.shape, i32),
                   jax.ShapeDtypeStruct((1, nb_pad), i32),
                   jax.ShapeDtypeStruct((1, LANES), i32)),
        grid=(2, n_tiles),
        in_specs=[pl.BlockSpec((1, 2, tm), lambda p, i: (i, 0, 0))],
        out_specs=(pl.BlockSpec((1, 2, tm), lambda p, i: (p * i, 0, 0)),
                   pl.BlockSpec((1, nb_pad), lambda p, i: (0, 0)),
                   pl.BlockSpec((1, LANES), lambda p, i: (0, 0))),
        scratch_shapes=[pltpu.VMEM((N_EXPERTS, LANES), f32), pltpu.VMEM((N_EXPERTS, LANES), f32)],
        compiler_params=_cparams("arbitrary", "arbitrary"),
        name="route_plan",
    )(e3)
    return be[0, :n_blocks], nu[0, :1], dest3, n_blocks


def _dispatch_kernel(info_ref, nu_ref, dest_ref, hp_ref, xs_hbm, buf, zbuf, sem, zsem):
    i = pl.program_id(0)
    n = pl.num_programs(0)
    tm = hp_ref.shape[0]
    n_blocks = xs_hbm.shape[0] // MOE_BLOCK

    @pl.when(i == 0)
    def _():
        zbuf[...] = jnp.zeros(zbuf.shape, u32)

        def fill(b):
            return pltpu.make_async_copy(
                zbuf, xs_hbm.at[pl.ds(pl.multiple_of(b * MOE_BLOCK, MOE_BLOCK), MOE_BLOCK), :], zsem.at[0])

        def needs_fill(b):
            next_first = (info_ref[jnp.minimum(b + 1, n_blocks - 1)] >> INFO_FIRST) & 1
            return (b >= nu_ref[0] - 1) | (next_first == 1)

        def start(b, c):
            @pl.when(needs_fill(b))
            def _():
                fill(b).start()
            return c

        def wait(b, c):
            @pl.when(needs_fill(b))
            def _():
                fill(b).wait()
            return c

        lax.fori_loop(0, n_blocks, start, 0)
        lax.fori_loop(0, n_blocks, wait, 0)

    def issue(s):
        buf[s] = hp_ref[...]
        for kk in range(2):
            for r in range(tm):
                pltpu.make_async_copy(
                    buf.at[s, pl.ds(r, 1), :], xs_hbm.at[pl.ds(dest_ref[0, kk, r], 1), :], sem.at[s]).start()

    def drain(s):
        for _ in range(2 * tm):
            pltpu.make_async_copy(buf.at[s, pl.ds(0, 1), :], xs_hbm.at[pl.ds(0, 1), :], sem.at[s]).wait()

    for s in range(2):
        @pl.when(i % 2 == s)
        def _():
            issue(s)

            @pl.when(i > 0)
            def _():
                drain(1 - s)

            @pl.when(i == n - 1)
            def _():
                drain(s)


def _dispatch(hp, dest3, blk_info, n_used):
    t, dw = hp.shape
    n_tiles = t // ROW_TILE
    slots = blk_info.shape[0] * MOE_BLOCK
    grid_spec = pltpu.PrefetchScalarGridSpec(
        num_scalar_prefetch=2,
        grid=(n_tiles,),
        in_specs=[
            pl.BlockSpec((1, 2, ROW_TILE), lambda i, info, nu: (i, 0, 0), memory_space=pltpu.SMEM),
            pl.BlockSpec((ROW_TILE, dw), lambda i, info, nu: (i, 0)),
        ],
        out_specs=pl.BlockSpec(memory_space=pl.ANY),
        scratch_shapes=[pltpu.VMEM((2, ROW_TILE, dw), u32), pltpu.VMEM((MOE_BLOCK, dw), u32),
                        pltpu.SemaphoreType.DMA((2,)), pltpu.SemaphoreType.DMA((1,))],
    )
    return pl.pallas_call(
        _dispatch_kernel,
        out_shape=jax.ShapeDtypeStruct((slots, dw), u32),
        grid_spec=grid_spec,
        compiler_params=_cparams("arbitrary"),
        name="moe_dispatch",
    )(blk_info, n_used, dest3, hp)


def _ffn_kernel(info_ref, nu_ref, xs_ref, wg_hbm, wu_hbm, wd_hbm, y_ref, wg_sc, wu_sc, wd_sc, sem, *, layer):
    b = pl.program_id(0)
    info = info_ref[b]
    expert = info & ((1 << INFO_FIRST) - 1)
    first = (info >> INFO_FIRST) & 1
    slot = (info >> INFO_SLOT) & 1
    nxt = info >> INFO_NEXT
    live = b < nu_ref[0]

    def weight_copies(e, s):
        return [pltpu.make_async_copy(src.at[layer, e], dst.at[s], sem.at[s, j])
                for j, (src, dst) in enumerate(((wg_hbm, wg_sc), (wu_hbm, wu_sc), (wd_hbm, wd_sc)))]

    @pl.when(b == 0)
    def _():
        for c in weight_copies(expert, 0):
            c.start()

    @pl.when(live & (first == 1) & (nxt != NO_EXPERT))
    def _():
        for c in weight_copies(nxt, 1 - slot):
            c.start()

    @pl.when(live & (first == 1))
    def _():
        for c in weight_copies(expert, slot):
            c.wait()

    def compute(s):
        w = xs_ref[...]
        lo = lax.bitcast_convert_type(w << 16, f32)
        hi = lax.bitcast_convert_type(w & jnp.uint32(0xFFFF0000), f32)
        xb = jnp.concatenate([lo, hi], axis=1).astype(bf16)
        a = jnp.dot(xb, wg_sc[s], preferred_element_type=f32)
        u = jnp.dot(xb, wu_sc[s], preferred_element_type=f32)
        z = (a * jax.nn.sigmoid(a) * u).astype(bf16)
        y_ref[...] = jnp.dot(z, wd_sc[s], preferred_element_type=f32)

    for s in range(2):
        @pl.when(live & (slot == s))
        def _():
            compute(s)

    @pl.when(jnp.logical_not(live))
    def _():
        y_ref[...] = jnp.zeros(y_ref.shape, f32)


def _expert_ffn(xs, blk_expert, n_used, layer, wg, wu, wd):
    slots, dw = xs.shape
    d = 2 * dw
    de = wg.shape[3]
    n_blocks = slots // MOE_BLOCK
    grid_spec = pltpu.PrefetchScalarGridSpec(
        num_scalar_prefetch=2,
        grid=(n_blocks,),
        in_specs=[
            pl.BlockSpec((MOE_BLOCK, dw), lambda b, be, nu: (b, 0)),
            pl.BlockSpec(memory_space=pl.ANY),
            pl.BlockSpec(memory_space=pl.ANY),
            pl.BlockSpec(memory_space=pl.ANY),
        ],
        out_specs=pl.BlockSpec((MOE_BLOCK, d), lambda b, be, nu: (b, 0)),
        scratch_shapes=[pltpu.VMEM((2, d, de), bf16), pltpu.VMEM((2, d, de), bf16), pltpu.VMEM((2, de, d), bf16),
                        pltpu.SemaphoreType.DMA((2, 3))],
    )
    return pl.pallas_call(
        functools.partial(_ffn_kernel, layer=layer),
        out_shape=jax.ShapeDtypeStruct((slots, d), f32),
        grid_spec=grid_spec,
        compiler_params=_cparams("arbitrary"),
        name="expert_ffn",
    )(blk_expert, n_used, xs, wg, wu, wd)


def _fin_kernel(pos_cur, pos_nxt, x1_ref, gate_ref, mod_ref, lng_ref, lnb_ref, y_hbm, o_ref, ybuf, sem):
    i = pl.program_id(0)
    n = pl.num_programs(0)
    tm = x1_ref.shape[0]
    slot = i % 2

    def copy(pos_ref, kk, r, s):
        return pltpu.make_async_copy(
            y_hbm.at[pl.ds(pos_ref[0, kk, r], 1), :], ybuf.at[s, kk, pl.ds(r, 1), :], sem.at[s])

    def start_gather(pos_ref, s):
        for kk in range(2):
            for r in range(tm):
                copy(pos_ref, kk, r, s).start()

    @pl.when(i == 0)
    def _():
        start_gather(pos_cur, 0)

    @pl.when(i + 1 < n)
    def _():
        start_gather(pos_nxt, 1 - slot)

    for kk in range(2):
        for r in range(tm):
            copy(pos_cur, kk, r, slot).wait()
    g2 = mod_ref[0, 5:6, :]
    gate = gate_ref[...]
    f = gate[:, 0:1] * ybuf[slot, 0] + gate[:, 1:2] * ybuf[slot, 1]
    z = DEEPNORM_ALPHA * x1_ref[...] + g2 * f
    mu = jnp.mean(z, axis=-1, keepdims=True)
    zc = z - mu
    var = jnp.mean(zc * zc, axis=-1, keepdims=True)
    o_ref[...] = zc * lax.rsqrt(var + LN_EPS) * lng_ref[...] + lnb_ref[...]


def _combine_norm(x1, gate, pos3, y_sorted, mod_l, ln_g, ln_b, seg_of_tile):
    t, d = x1.shape
    n_tiles = t // ROW_TILE
    return pl.pallas_call(
        _fin_kernel,
        out_shape=jax.ShapeDtypeStruct((t, d), f32),
        grid=(n_tiles,),
        in_specs=[
            pl.BlockSpec((1, 2, ROW_TILE), lambda i: (i, 0, 0), memory_space=pltpu.SMEM),
            pl.BlockSpec((1, 2, ROW_TILE), lambda i: (jnp.minimum(i + 1, n_tiles - 1), 0, 0),
                         memory_space=pltpu.SMEM),
            pl.BlockSpec((ROW_TILE, d), lambda i: (i, 0)),
            pl.BlockSpec((ROW_TILE, GATE_COLS), lambda i: (i, 0)),
            pl.BlockSpec((1, 6, d), lambda i: (seg_of_tile(i), 0, 0)),
            pl.BlockSpec((1, d), lambda i: (0, 0)),
            pl.BlockSpec((1, d), lambda i: (0, 0)),
            pl.BlockSpec(memory_space=pl.ANY),
        ],
        out_specs=pl.BlockSpec((ROW_TILE, d), lambda i: (i, 0)),
        scratch_shapes=[pltpu.VMEM((2, 2, ROW_TILE, d), f32), pltpu.SemaphoreType.DMA((2,))],
        compiler_params=_cparams("arbitrary"),
        name="combine_norm",
    )(pos3, pos3, x1, gate, mod_l, ln_g, ln_b, y_sorted)


def _moe_combine(x1, hp, e_idx, gate, layer, wg_bf, wu_bf, wd_bf, mod_l, ln_g, ln_b, seg_of_tile):
    blk_expert, n_used, dest3, n_blocks = _dispatch_plan(e_idx)
    xs = _dispatch(hp, dest3, blk_expert, n_used)
    y_sorted = _expert_ffn(xs, blk_expert, n_used, layer, wg_bf, wu_bf, wd_bf)
    return _combine_norm(x1, gate, dest3, y_sorted, mod_l, ln_g, ln_b, seg_of_tile)


def _rope_tables(ctx_len, seq_len):
    t = jnp.arange(seq_len)
    row = (t // GRID_W).astype(f32)
    col = (t % GRID_W).astype(f32)
    inv_freq = ROPE_THETA ** (-jnp.arange(ROPE_PAIRS, dtype=f32) / ROPE_PAIRS)
    ang_r = row[:, None] * inv_freq
    ang_c = col[:, None] * inv_freq
    ang = jnp.concatenate([ang_r, ang_r, ang_c, ang_c], axis=-1)
    cos = jnp.cos(ang)
    sin = jnp.sin(ang)
    lane = jnp.arange(HEAD_DIM)
    sign = jnp.where((lane % (2 * ROPE_PAIRS)) < ROPE_PAIRS, -1.0, 1.0).astype(f32)
    cos_t = jnp.concatenate([jnp.ones((ctx_len, HEAD_DIM), f32), cos], axis=0)
    sin_t = jnp.concatenate([jnp.zeros((ctx_len, HEAD_DIM), f32), sin * sign], axis=0)
    return cos_t, sin_t


def kernel(x, c, ctx, c_ctx, w_mod, b_mod, ln_g, ln_b, w_qkv, q_gain, k_gain, w_o, w_pool, pool_scale,
           w_router, router_bias, w_gate, w_up, w_down):
    n_batch, seq_len, d = x.shape
    ctx_len = ctx.shape[1]
    assert w_mod.shape[0] == DEPTH and d == N_HEADS * HEAD_DIM
    assert ctx_len % ROW_TILE == 0 and seq_len % ROW_TILE == 0 and 1 + n_batch <= MOD_ROWS
    rows = _Rows(n_batch, ctx_len, seq_len)

    cond = jnp.concatenate([c_ctx[None, :], c], axis=0)
    mod = _modulation(cond, w_mod, b_mod).reshape(DEPTH, MOD_ROWS, 6, d)

    wr_t = w_router.reshape(d, N_EXPERT_GROUPS, EXPERTS_PER_GROUP).transpose(2, 1, 0).reshape(N_EXPERTS, d)
    wr_hi = wr_t.astype(bf16)
    wr_hl = jnp.concatenate([wr_hi, (wr_t - wr_hi.astype(f32)).astype(bf16)], axis=0)
    rb_p = router_bias.reshape(N_EXPERT_GROUPS, EXPERTS_PER_GROUP).T.reshape(N_EXPERTS, 1)
    cos_t, sin_t = _rope_tables(ctx_len, seq_len)

    ctx2 = ctx.reshape(n_batch * ctx_len, d)
    x2 = x.reshape(n_batch * seq_len, d)
    q, k, v = _qkv_proj(ctx2, x2, mod[0], w_qkv[0].astype(bf16), q_gain[0:1], k_gain[0:1], cos_t, sin_t, rows)
    bound = (Q_SCALE * HEAD_DIM * SCORE_BOUND_SLACK) * jnp.max(jnp.abs(q_gain[0])) * jnp.max(jnp.abs(k_gain[0]))
    attn_args = (bound.reshape(1, 1), q, k, v, w_gate, w_up, w_down)
    o, wg_bf, wu_bf, wd_bf = lax.cond(
        bound <= SCORE_BOUND_MAX,
        lambda a: _attention(*a, rows, True),
        lambda a: _attention(*a, rows, False),
        attn_args)
    x1, hp, e_idx, gate = _attn_post(ctx2, x2, o, w_o[0].astype(bf16), mod[0], ln_g[0, 0:1], ln_b[0, 0:1],
                                     wr_hl, rb_p, rows)
    x_all = _moe_combine(x1, hp, e_idx, gate, 0, wg_bf, wu_bf, wd_bf, mod[0], ln_g[0, 1:2], ln_b[0, 1:2],
                         rows.seg)

    x1, hp, e_idx, gate = _pool_post(x_all, w_pool[0].astype(bf16), pool_scale[0:1], mod[1],
                                     ln_g[1, 0:1], ln_b[1, 0:1], wr_hl, rb_p, rows)
    out = _moe_combine(x1, hp, e_idx, gate, 1, wg_bf, wu_bf, wd_bf, mod[1], ln_g[1, 1:2], ln_b[1, 1:2],
                       lambda i: 1 + i // rows.n_seq_tiles)
    return out.reshape(n_batch, seq_len, d)
```

```python
import functools

import jax
import jax.numpy as jnp
from jax import lax
from jax.experimental import pallas as pl
from jax.experimental.pallas import tpu as pltpu

N_HEADS = 16
N_KV_HEADS = 4
HEAD_DIM = 128
Q_PER_KV = N_HEADS // N_KV_HEADS
GRID_W = 64
ROPE_THETA = 10000.0
ROPE_PAIRS = HEAD_DIM // 4
ATTN_SCALE = HEAD_DIM ** -0.5
Q_SCALE = ATTN_SCALE * 1.4426950408889634
POOL_WINDOWS = (2, 4, 8, 16)
POOL_HALO = 8
N_EXPERTS = 32
N_EXPERT_GROUPS = 8
EXPERTS_PER_GROUP = N_EXPERTS // N_EXPERT_GROUPS
DEPTH = 2
DEEPNORM_ALPHA = (2 * DEPTH) ** 0.25
LN_EPS = 1e-6
RMS_EPS = 1e-6

LANES = 128
SUBLANES = 8
ROW_TILE = 256
ATTN_TK = 2048
ATTN_ROWS = 128
SCORE_BOUND_SLACK = 1.02
SCORE_BOUND_MAX = 56.0
POST_ROWS = 128
MOE_BLOCK = 256
PLAN_TILES_PER_STEP = (8, 4, 2, 1)
NO_EXPERT = N_EXPERTS
INFO_FIRST, INFO_SLOT, INFO_NEXT = 8, 9, 10
GATE_COLS = 8
MOD_ROWS = 8
MOD_TN = 512
VMEM_LIMIT = 56 * 1024 * 1024

f32 = jnp.float32
bf16 = jnp.bfloat16
u32 = jnp.uint32
i32 = jnp.int32


def _cparams(*sem, flags=None):
    return pltpu.CompilerParams(dimension_semantics=sem, vmem_limit_bytes=VMEM_LIMIT, flags=flags)


def _nt_dot(a, b):
    return lax.dot_general(a, b, (((1,), (1,)), ((), ())), preferred_element_type=f32)


def _mod_kernel(c_ref, w_ref, b_ref, o_ref, s_sc, *, n_rows):
    d = w_ref.shape[1]
    tn = w_ref.shape[2]
    nj = tn // LANES

    @pl.when((pl.program_id(0) == 0) & (pl.program_id(1) == 0))
    def _():
        c = c_ref[...]
        s_sc[...] = c * jax.nn.sigmoid(c)

    def body(k, accs):
        r0 = pl.multiple_of(k * SUBLANES, SUBLANES)
        w8 = w_ref[0, pl.ds(r0, SUBLANES), :]
        out = []
        for r in range(n_rows):
            s8 = s_sc[r, pl.ds(r0, SUBLANES), :]
            for j in range(nj):
                out.append(accs[r * nj + j] + s8 * w8[:, j * LANES:(j + 1) * LANES])
        return tuple(out)

    init = tuple(jnp.zeros((SUBLANES, LANES), f32) for _ in range(n_rows * nj))
    accs = lax.fori_loop(0, d // SUBLANES, body, init, unroll=4)
    rows = []
    for r in range(n_rows):
        row = jnp.concatenate(
            [jnp.sum(accs[r * nj + j], axis=0, keepdims=True) for j in range(nj)], axis=1)
        rows.append(row + b_ref[0])
    rows.append(jnp.zeros((MOD_ROWS - n_rows, tn), f32))
    o_ref[0] = jnp.concatenate(rows, axis=0)


def _modulation(cond, w_mod, b_mod):
    n_rows, d = cond.shape
    depth, _, n6 = w_mod.shape
    c_b = jnp.broadcast_to(cond[:, :, None], (n_rows, d, LANES))
    return pl.pallas_call(
        functools.partial(_mod_kernel, n_rows=n_rows),
        out_shape=jax.ShapeDtypeStruct((depth, MOD_ROWS, n6), f32),
        grid=(depth, n6 // MOD_TN),
        in_specs=[
            pl.BlockSpec((n_rows, d, LANES), lambda l, j: (0, 0, 0)),
            pl.BlockSpec((1, d, MOD_TN), lambda l, j: (l, 0, j)),
            pl.BlockSpec((1, 1, MOD_TN), lambda l, j: (l, 0, j)),
        ],
        out_specs=pl.BlockSpec((1, MOD_ROWS, MOD_TN), lambda l, j: (l, 0, j)),
        scratch_shapes=[pltpu.VMEM((n_rows, d, LANES), f32)],
        compiler_params=_cparams("arbitrary", "arbitrary"),
        name="adaln_mod",
    )(c_b, w_mod, b_mod.reshape(depth, 1, n6))


class _Rows:
    def __init__(self, n_batch, ctx_len, seq_len):
        self.n_batch = n_batch
        self.ctx_len = ctx_len
        self.seq_len = seq_len
        self.n_ctx_tiles = ctx_len // ROW_TILE
        self.n_seq_tiles = seq_len // ROW_TILE
        self.tiles_per_batch = self.n_ctx_tiles + self.n_seq_tiles
        self.n_tiles = n_batch * self.tiles_per_batch
        self.t_all = self.n_tiles * ROW_TILE

    def is_ctx(self, i):
        return i % self.tiles_per_batch < self.n_ctx_tiles

    def seg(self, i):
        return jnp.where(self.is_ctx(i), 0, 1 + i // self.tiles_per_batch)

    def ctx_tile(self, i):
        r = i % self.tiles_per_batch
        return (i // self.tiles_per_batch) * self.n_ctx_tiles + jnp.minimum(r, self.n_ctx_tiles - 1)

    def lat_tile(self, i):
        r = i % self.tiles_per_batch
        return (i // self.tiles_per_batch) * self.n_seq_tiles + jnp.maximum(r - self.n_ctx_tiles, 0)


def _qkv_kernel(xc_ref, xl_ref, mod_ref, w_ref, qg_ref, kg_ref, cos_ref, sin_ref, q_ref, k_ref, v_ref, *, rows):
    tm, d = xl_ref.shape
    kv_dim = k_ref.shape[1]
    x = jnp.where(rows.is_ctx(pl.program_id(0)), xc_ref[...], xl_ref[...])
    sh1 = mod_ref[0, 0:1, :]
    sc1 = mod_ref[0, 1:2, :]
    u = (x * (1.0 + sc1) + sh1).astype(bf16)
    qkv = jnp.dot(u, w_ref[...], preferred_element_type=f32)
    cos = cos_ref[...]
    sin = sin_ref[...]
    lane = lax.broadcasted_iota(i32, (tm, HEAD_DIM), 1)
    first = (lane % (2 * ROPE_PAIRS)) < ROPE_PAIRS

    def norm_rope(xh, gain, scale):
        ms = jnp.mean(xh * xh, axis=-1, keepdims=True)
        xn = xh * lax.rsqrt(ms + RMS_EPS) * gain
        rot = jnp.where(first, pltpu.roll(xn, HEAD_DIM - ROPE_PAIRS, 1), pltpu.roll(xn, ROPE_PAIRS, 1))
        y = xn * cos + rot * sin
        return y * scale if scale != 1.0 else y

    qg = qg_ref[...]
    kg = kg_ref[...]
    for h in range(N_HEADS):
        sl = slice(h * HEAD_DIM, (h + 1) * HEAD_DIM)
        q_ref[:, sl] = norm_rope(qkv[:, sl], qg, Q_SCALE).astype(bf16)
    ones = jnp.ones((tm, HEAD_DIM), bf16)
    for h in range(N_KV_HEADS):
        sl = slice(h * HEAD_DIM, (h + 1) * HEAD_DIM)
        k_ref[:, sl] = norm_rope(qkv[:, d + h * HEAD_DIM:d + (h + 1) * HEAD_DIM], kg, 1.0).astype(bf16)
        v0 = d + kv_dim + h * HEAD_DIM
        v_ref[:, 2 * h * HEAD_DIM:(2 * h + 1) * HEAD_DIM] = qkv[:, v0:v0 + HEAD_DIM].astype(bf16)
        v_ref[:, (2 * h + 1) * HEAD_DIM:(2 * h + 2) * HEAD_DIM] = ones


def _qkv_proj(ctx2, x2, mod_l, w_qkv_bf, q_gain, k_gain, cos_t, sin_t, rows):
    d = x2.shape[1]
    kv_dim = N_KV_HEADS * HEAD_DIM
    tpb = rows.tiles_per_batch
    return pl.pallas_call(
        functools.partial(_qkv_kernel, rows=rows),
        out_shape=(jax.ShapeDtypeStruct((rows.t_all, d), bf16),
                   jax.ShapeDtypeStruct((rows.t_all, kv_dim), bf16),
                   jax.ShapeDtypeStruct((rows.t_all, 2 * kv_dim), bf16)),
        grid=(rows.n_tiles,),
        in_specs=[
            pl.BlockSpec((ROW_TILE, d), lambda i: (rows.ctx_tile(i), 0)),
            pl.BlockSpec((ROW_TILE, d), lambda i: (rows.lat_tile(i), 0)),
            pl.BlockSpec((1, 6, d), lambda i: (rows.seg(i), 0, 0)),
            pl.BlockSpec((d, d + 2 * kv_dim), lambda i: (0, 0)),
            pl.BlockSpec((1, HEAD_DIM), lambda i: (0, 0)),
            pl.BlockSpec((1, HEAD_DIM), lambda i: (0, 0)),
            pl.BlockSpec((ROW_TILE, HEAD_DIM), lambda i: (i % tpb, 0)),
            pl.BlockSpec((ROW_TILE, HEAD_DIM), lambda i: (i % tpb, 0)),
        ],
        out_specs=(pl.BlockSpec((ROW_TILE, d), lambda i: (i, 0)),
                   pl.BlockSpec((ROW_TILE, kv_dim), lambda i: (i, 0)),
                   pl.BlockSpec((ROW_TILE, 2 * kv_dim), lambda i: (i, 0))),
        compiler_params=_cparams("arbitrary"),
        name="qkv_rope",
    )(ctx2, x2, mod_l, w_qkv_bf, q_gain, k_gain, cos_t, sin_t)


def _attn_kernel(bound_ref, q_ref, k_ref, v_ref, wg_ref, wu_ref, wd_ref, o_ref, wgo_ref, wuo_ref, wdo_ref,
                 *, ctx_len, tk, rs, n_ctx_tiles, bounded):
    tq = q_ref.shape[0]
    n_lat_chunks = (k_ref.shape[0] - ctx_len) // tk
    qi = pl.program_id(2)
    subs = [(h, r0) for h in range(Q_PER_KV) for r0 in range(0, tq, rs)]

    def attend(chunks):
        states = [None] * len(subs)
        for ci, (off, size) in enumerate(chunks):
            for src, dst in ((wg_ref, wgo_ref), (wu_ref, wuo_ref), (wd_ref, wdo_ref)):
                n = src.shape[0] // len(chunks)
                dst[ci * n:(ci + 1) * n, :] = src[ci * n:(ci + 1) * n, :].astype(bf16)
            kc = k_ref[off:off + size, :]
            vc = v_ref[off:off + size, :]
            for i, (h, r0) in enumerate(subs):
                s = _nt_dot(q_ref[r0:r0 + rs, h * HEAD_DIM:(h + 1) * HEAD_DIM], kc)
                if bounded:
                    pv = jnp.dot(jnp.exp2(s - bound_ref[0, 0]).astype(bf16), vc, preferred_element_type=f32)
                    states[i] = pv if states[i] is None else states[i] + pv
                    continue
                mx = s[:, 0:LANES]
                for j in range(1, size // LANES):
                    mx = jnp.maximum(mx, s[:, j * LANES:(j + 1) * LANES])
                m_cur = jnp.max(mx, axis=-1, keepdims=True)
                if states[i] is None:
                    m_new = jnp.broadcast_to(m_cur, (rs, LANES))
                else:
                    m_prev, acc_prev = states[i]
                    m_new = jnp.maximum(m_prev, m_cur)
                p = jnp.concatenate(
                    [jnp.exp2(s[:, j * LANES:(j + 1) * LANES] - m_new) for j in range(size // LANES)],
                    axis=1).astype(bf16)
                acc = jnp.dot(p, vc, preferred_element_type=f32)
                if states[i] is not None:
                    alpha = jnp.exp2(m_prev - m_new)
                    acc = acc_prev * jnp.concatenate([alpha, alpha], axis=1) + acc
                states[i] = (m_new, acc)
        for i, (h, r0) in enumerate(subs):
            acc = states[i] if bounded else states[i][1]
            o_ref[r0:r0 + rs, h * HEAD_DIM:(h + 1) * HEAD_DIM] = (
                acc[:, :HEAD_DIM] / acc[:, HEAD_DIM:]).astype(bf16)

    @pl.when(qi < n_ctx_tiles)
    def _():
        attend([(0, ctx_len)])

    @pl.when(qi >= n_ctx_tiles)
    def _():
        attend([(0, ctx_len + tk)] + [(ctx_len + c * tk, tk) for c in range(1, n_lat_chunks)])


def _attention(bound, q, k, v, w_gate, w_up, w_down, rows, bounded):
    t_all, d = q.shape
    tpb = rows.tiles_per_batch
    rows_per_batch = t_all // rows.n_batch
    gw = Q_PER_KV * HEAD_DIM
    tk = min(ATTN_TK, rows.seq_len)
    assert rows.seq_len % tk == 0

    n_steps = rows.n_batch * N_KV_HEADS * tpb
    wg2 = w_gate.reshape(-1, w_gate.shape[-1])
    wu2 = w_up.reshape(-1, w_up.shape[-1])
    wd2 = w_down.reshape(-1, w_down.shape[-1])
    n_slabs = 1 << (n_steps.bit_length() - 1)
    slab_g = wg2.shape[0] // n_slabs
    slab_d = wd2.shape[0] // n_slabs
    assert wg2.shape[0] == n_slabs * slab_g and wd2.shape[0] == n_slabs * slab_d and wu2.shape == wg2.shape
    assert slab_g % SUBLANES == 0 and slab_d % SUBLANES == 0

    def slab(b, g, i):
        return (jnp.minimum((b * N_KV_HEADS + g) * tpb + i, n_slabs - 1), 0)

    kern = functools.partial(_attn_kernel, ctx_len=rows.ctx_len, tk=tk, rs=ATTN_ROWS,
                             n_ctx_tiles=rows.n_ctx_tiles, bounded=bounded)
    o, wg_bf, wu_bf, wd_bf = pl.pallas_call(
        kern,
        out_shape=(jax.ShapeDtypeStruct((t_all, d), bf16),
                   jax.ShapeDtypeStruct(wg2.shape, bf16),
                   jax.ShapeDtypeStruct(wu2.shape, bf16),
                   jax.ShapeDtypeStruct(wd2.shape, bf16)),
        grid=(rows.n_batch, N_KV_HEADS, tpb),
        in_specs=[
            pl.BlockSpec((1, 1), lambda b, g, i: (0, 0), memory_space=pltpu.SMEM),
            pl.BlockSpec((ROW_TILE, gw), lambda b, g, i: (b * tpb + i, g)),
            pl.BlockSpec((rows_per_batch, HEAD_DIM), lambda b, g, i: (b, g)),
            pl.BlockSpec((rows_per_batch, 2 * HEAD_DIM), lambda b, g, i: (b, g)),
            pl.BlockSpec((slab_g, wg2.shape[1]), slab),
            pl.BlockSpec((slab_g, wu2.shape[1]), slab),
            pl.BlockSpec((slab_d, wd2.shape[1]), slab),
        ],
        out_specs=(pl.BlockSpec((ROW_TILE, gw), lambda b, g, i: (b * tpb + i, g)),
                   pl.BlockSpec((slab_g, wg2.shape[1]), slab),
                   pl.BlockSpec((slab_g, wu2.shape[1]), slab),
                   pl.BlockSpec((slab_d, wd2.shape[1]), slab)),
        compiler_params=_cparams("arbitrary", "arbitrary", "arbitrary"),
        name="gqa_attention_bounded" if bounded else "gqa_attention_online",
    )(bound, q, k, v, wg2, wu2, wd2)
    return o, wg_bf.reshape(w_gate.shape), wu_bf.reshape(w_up.shape), wd_bf.reshape(w_down.shape)


def _post_epilogue(x, y, r0, mod_ref, lng_ref, lnb_ref, wr_ref, rb_ref, x1_ref, hp_ref, e_ref, g_ref):
    tm, d = x.shape
    rsl = slice(r0, r0 + tm)
    g1 = mod_ref[0, 2:3, :]
    sh2 = mod_ref[0, 3:4, :]
    sc2 = mod_ref[0, 4:5, :]
    z = DEEPNORM_ALPHA * x + g1 * y
    mu = jnp.mean(z, axis=-1, keepdims=True)
    zc = z - mu
    var = jnp.mean(zc * zc, axis=-1, keepdims=True)
    x1 = zc * lax.rsqrt(var + LN_EPS) * lng_ref[...] + lnb_ref[...]
    h = x1 * (1.0 + sc2) + sh2
    x1_ref[rsl, :] = x1

    h_hi = h.astype(bf16)
    h_hi32 = h_hi.astype(f32)
    bits = lax.bitcast_convert_type(h_hi32, u32)
    hp_ref[rsl, :] = (bits[:, :d // 2] >> 16) | bits[:, d // 2:]

    h_lo = (h - h_hi32).astype(bf16)
    wr = wr_ref[...]
    a2 = _nt_dot(wr, h_hi)
    lt = a2[:N_EXPERTS] + a2[N_EXPERTS:] + _nt_dot(wr[:N_EXPERTS], h_lo)
    aff = jax.nn.sigmoid(lt)
    sel = aff + rb_ref[...]
    ng = N_EXPERT_GROUPS
    a = [aff[j * ng:(j + 1) * ng] for j in range(EXPERTS_PER_GROUP)]
    s = [sel[j * ng:(j + 1) * ng] for j in range(EXPERTS_PER_GROUP)]
    gs = None
    for i in range(EXPERTS_PER_GROUP):
        for j in range(i + 1, EXPERTS_PER_GROUP):
            pij = s[i] + s[j]
            gs = pij if gs is None else jnp.maximum(gs, pij)
    gio = lax.broadcasted_iota(i32, gs.shape, 0)
    gmax = jnp.max(gs, axis=0, keepdims=True)
    g_idx = jnp.min(jnp.where(gs == gmax, gio, ng), axis=0, keepdims=True)
    onehot = gio == g_idx
    sin_g = [jnp.sum(jnp.where(onehot, s[j], 0.0), axis=0, keepdims=True) for j in range(EXPERTS_PER_GROUP)]
    aff_g = [jnp.sum(jnp.where(onehot, a[j], 0.0), axis=0, keepdims=True) for j in range(EXPERTS_PER_GROUP)]
    loc = [jnp.zeros_like(g_idx), jnp.zeros_like(g_idx)]
    gat = [jnp.zeros_like(gmax), jnp.zeros_like(gmax)]
    for j in range(EXPERTS_PER_GROUP):
        rank = jnp.zeros_like(g_idx)
        for i in range(EXPERTS_PER_GROUP):
            if i == j:
                continue
            ahead = (sin_g[i] >= sin_g[j]) if i < j else (sin_g[i] > sin_g[j])
            rank = rank + ahead.astype(i32)
        for kk in range(2):
            hit = rank == kk
            loc[kk] = jnp.where(hit, j, loc[kk])
            gat[kk] = jnp.where(hit, aff_g[j], gat[kk])
    den = gat[0] + gat[1]
    e_ref[0, :, rsl] = jnp.concatenate(
        [g_idx * EXPERTS_PER_GROUP + loc[0], g_idx * EXPERTS_PER_GROUP + loc[1]], axis=0)
    g8 = jnp.concatenate([gat[0] / den, gat[1] / den, jnp.zeros((GATE_COLS - 2, tm), f32)], axis=0)
    g_ref[rsl, :] = g8.T


def _attn_post_kernel(xc_ref, xl_ref, o_ref, wo_ref, mod_ref, lng_ref, lnb_ref, wr_ref, rb_ref,
                      x1_ref, hp_ref, e_ref, g_ref, ya_sc, yb_sc, *, rows):
    s = pl.program_id(0)
    tile = jnp.maximum(s - 1, 0)

    @pl.when(s == 0)
    def _():
        yb_sc[...] = jnp.zeros(yb_sc.shape, f32)

    def body(y_write, y_read):
        y_write[...] = jnp.dot(o_ref[...], wo_ref[...], preferred_element_type=f32)
        x = jnp.where(rows.is_ctx(tile), xc_ref[...], xl_ref[...])
        _post_epilogue(x, y_read[...], 0, mod_ref, lng_ref, lnb_ref, wr_ref, rb_ref,
                       x1_ref, hp_ref, e_ref, g_ref)

    @pl.when(s % 2 == 0)
    def _():
        body(ya_sc, yb_sc)

    @pl.when(s % 2 == 1)
    def _():
        body(yb_sc, ya_sc)


def _post_out(n_rows, d, tile_of):
    shapes = (jax.ShapeDtypeStruct((n_rows, d), f32),
              jax.ShapeDtypeStruct((n_rows, d // 2), u32),
              jax.ShapeDtypeStruct((n_rows // ROW_TILE, 2, ROW_TILE), i32),
              jax.ShapeDtypeStruct((n_rows, GATE_COLS), f32))
    specs = (pl.BlockSpec((ROW_TILE, d), lambda i: (tile_of(i), 0)),
             pl.BlockSpec((ROW_TILE, d // 2), lambda i: (tile_of(i), 0)),
             pl.BlockSpec((1, 2, ROW_TILE), lambda i: (tile_of(i), 0, 0)),
             pl.BlockSpec((ROW_TILE, GATE_COLS), lambda i: (tile_of(i), 0)))
    return shapes, specs


def _router_specs(d):
    return [pl.BlockSpec((2 * N_EXPERTS, d), lambda i: (0, 0)),
            pl.BlockSpec((N_EXPERTS, 1), lambda i: (0, 0))]


def _attn_post(ctx2, x2, o, w_o_bf, mod_l, ln_g, ln_b, wr_hl, rb_p, rows):
    d = x2.shape[1]
    n = rows.n_tiles

    def prev(i):
        return jnp.maximum(i - 1, 0)

    shapes, specs = _post_out(rows.t_all, d, prev)
    return pl.pallas_call(
        functools.partial(_attn_post_kernel, rows=rows),
        out_shape=shapes,
        grid=(n + 1,),
        in_specs=[
            pl.BlockSpec((ROW_TILE, d), lambda i: (rows.ctx_tile(prev(i)), 0)),
            pl.BlockSpec((ROW_TILE, d), lambda i: (rows.lat_tile(prev(i)), 0)),
            pl.BlockSpec((ROW_TILE, d), lambda i: (jnp.minimum(i, n - 1), 0)),
            pl.BlockSpec((d, d), lambda i: (0, 0)),
            pl.BlockSpec((1, 6, d), lambda i: (rows.seg(prev(i)), 0, 0)),
            pl.BlockSpec((1, d), lambda i: (0, 0)),
            pl.BlockSpec((1, d), lambda i: (0, 0)),
        ] + _router_specs(d),
        out_specs=specs,
        scratch_shapes=[pltpu.VMEM((ROW_TILE, d), f32), pltpu.VMEM((ROW_TILE, d), f32)],
        compiler_params=_cparams("arbitrary"),
        name="attn_out_norm_route",
    )(ctx2, x2, o, w_o_bf, mod_l, ln_g, ln_b, wr_hl, rb_p)


def _pool_post_kernel(x_ref, xp_ref, xn_ref, wp_ref, ps_ref, mod_ref, lng_ref, lnb_ref, wr_ref, rb_ref,
                      x1_ref, hp_ref, e_ref, g_ref, *, seq_len, tiles_per_seq):
    tm, d = x_ref.shape
    gd = d // len(POOL_WINDOWS)
    r = pl.program_id(0) % tiles_per_seq
    sh1 = mod_ref[0, 0:1, :]
    sc1 = mod_ref[0, 1:2, :]
    x = x_ref[...]
    ext = jnp.concatenate([xp_ref[...], x, xn_ref[...]], axis=0)
    n_ext = tm + 2 * POOL_HALO
    pos = r * tm - POOL_HALO + lax.broadcasted_iota(i32, (n_ext, 1), 0)
    inside = (pos >= 0) & (pos < seq_len)
    u_ext = jnp.where(inside, ext * (1.0 + sc1) + sh1, 0.0)
    t = pos[POOL_HALO:POOL_HALO + tm]
    dlts = []
    for g, w in enumerate(POOL_WINDOWS):
        e = u_ext[:, g * gd:(g + 1) * gd]
        p = e + pltpu.roll(e, 1, 0)
        step = 1
        while 2 * step < w:
            p = pltpu.roll(p, step, 0) + pltpu.roll(p, n_ext - step, 0)
            step *= 2
        cnt = (jnp.clip(t + w // 2, 0, seq_len) - jnp.clip(t - w // 2, 0, seq_len)).astype(f32)
        mean = p[POOL_HALO:POOL_HALO + tm] / cnt
        dlts.append((mean - e[POOL_HALO:POOL_HALO + tm]).astype(bf16))
    for r0 in range(0, tm, POST_ROWS):
        rsl = slice(r0, r0 + POST_ROWS)
        y = jnp.concatenate(
            [jnp.dot(dlts[g][rsl, :], wp_ref[g], preferred_element_type=f32) for g in range(len(POOL_WINDOWS))],
            axis=-1) * ps_ref[...]
        _post_epilogue(x[rsl, :], y, r0, mod_ref, lng_ref, lnb_ref, wr_ref, rb_ref, x1_ref, hp_ref, e_ref, g_ref)


def _pool_post(x_all, w_pool_bf, pool_scale, mod_l, ln_g, ln_b, wr_hl, rb_p, rows):
    t_all, d = x_all.shape
    tiles_per_seq = rows.n_seq_tiles
    n_rows = rows.n_batch * rows.seq_len
    halo_per_tile = ROW_TILE // POOL_HALO
    last_halo = t_all // POOL_HALO - 1
    ng = len(POOL_WINDOWS)

    def src_tile(i):
        return (i // tiles_per_seq) * rows.tiles_per_batch + rows.n_ctx_tiles + i % tiles_per_seq

    shapes, specs = _post_out(n_rows, d, lambda i: i)
    kern = functools.partial(_pool_post_kernel, seq_len=rows.seq_len, tiles_per_seq=tiles_per_seq)
    return pl.pallas_call(
        kern,
        out_shape=shapes,
        grid=(n_rows // ROW_TILE,),
        in_specs=[
            pl.BlockSpec((ROW_TILE, d), lambda i: (src_tile(i), 0)),
            pl.BlockSpec((POOL_HALO, d), lambda i: (jnp.maximum(src_tile(i) * halo_per_tile - 1, 0), 0)),
            pl.BlockSpec((POOL_HALO, d), lambda i: (jnp.minimum((src_tile(i) + 1) * halo_per_tile, last_halo), 0)),
            pl.BlockSpec((ng, d // ng, d // ng), lambda i: (0, 0, 0)),
            pl.BlockSpec((1, d), lambda i: (0, 0)),
            pl.BlockSpec((1, 6, d), lambda i: (1 + i // tiles_per_seq, 0, 0)),
            pl.BlockSpec((1, d), lambda i: (0, 0)),
            pl.BlockSpec((1, d), lambda i: (0, 0)),
        ] + _router_specs(d),
        out_specs=specs,
        compiler_params=_cparams("arbitrary"),
        name="pool_norm_route",
    )(x_all, x_all, x_all, w_pool_bf, pool_scale, mod_l, ln_g, ln_b, wr_hl, rb_p)


def _plan_kernel(e_ref, dest_ref, be_ref, nu_ref, carry_sc, start_sc):
    phase = pl.program_id(0)
    i = pl.program_id(1)
    tm = e_ref.shape[2]
    sub = lax.broadcasted_iota(i32, (N_EXPERTS, tm), 0)
    shift = MOE_BLOCK.bit_length() - 1

    @pl.when((phase == 0) & (i == 0))
    def _():
        carry_sc[...] = jnp.zeros(carry_sc.shape, f32)

    @pl.when(phase == 0)
    def _():
        carry = carry_sc[...]
        for j in range(e_ref.shape[0]):
            for kk in range(2):
                hit = sub == e_ref[j, kk:kk + 1, :]
                carry = carry + jnp.sum(jnp.where(hit, 1.0, 0.0), axis=1, keepdims=True)
        carry_sc[...] = carry

    @pl.when((phase == 1) & (i == 0))
    def _():
        counts = carry_sc[...].astype(i32)
        padded = ((counts + (MOE_BLOCK - 1)) >> shift) << shift
        esub = lax.broadcasted_iota(i32, padded.shape, 0)
        used = jnp.where(padded > 0, 1, 0)
        start = jnp.zeros(padded.shape, i32)
        order = jnp.zeros(padded.shape, i32)
        for e in range(N_EXPERTS - 1):
            start = start + jnp.where(esub > e, padded[e:e + 1, :], 0)
            order = order + jnp.where(esub > e, used[e:e + 1, :], 0)
        ends = start + padded
        nxt = jnp.full(padded.shape, NO_EXPERT, i32)
        cur = jnp.full((1, LANES), NO_EXPERT, i32)
        for e in range(N_EXPERTS - 1, -1, -1):
            nxt = jnp.where(esub == e, cur, nxt)
            cur = jnp.where(used[e:e + 1, :] > 0, e, cur)
        nb_pad = be_ref.shape[1]
        rep = nb_pad // LANES
        lane = lax.broadcasted_iota(i32, (1, nb_pad), 1)
        be = jnp.zeros((1, nb_pad), i32)
        for e in range(N_EXPERTS):
            be = be + jnp.where(jnp.concatenate([ends[e:e + 1, :]] * rep, axis=1) <= lane * MOE_BLOCK, 1, 0)
        be = jnp.minimum(be, N_EXPERTS - 1)
        par_b = jnp.zeros((1, nb_pad), i32)
        nxt_b = jnp.zeros((1, nb_pad), i32)
        for e in range(N_EXPERTS):
            hit = be == e
            par_b = jnp.where(hit, jnp.concatenate([order[e:e + 1, :]] * rep, axis=1) & 1, par_b)
            nxt_b = jnp.where(hit, jnp.concatenate([nxt[e:e + 1, :]] * rep, axis=1), nxt_b)
        first_b = jnp.where((be != pltpu.roll(be, 1, 1)) | (lane == 0), 1, 0)
        be_ref[...] = be | (first_b << INFO_FIRST) | (par_b << INFO_SLOT) | (nxt_b << INFO_NEXT)
        nu_ref[...] = ends[N_EXPERTS - 1:N_EXPERTS, :] >> shift
        start_sc[...] = start.astype(f32)
        carry_sc[...] = jnp.zeros(carry_sc.shape, f32)

    @pl.when(phase == 1)
    def _():
        before = lax.broadcasted_iota(i32, (tm, tm), 0) < lax.broadcasted_iota(i32, (tm, tm), 1)
        tri = jnp.where(before, 1.0, 0.0).astype(bf16)
        carry = carry_sc[...]
        start = start_sc[...]
        for j in range(e_ref.shape[0]):
            out = []
            for kk in range(2):
                hit = sub == e_ref[j, kk:kk + 1, :]
                onehot = jnp.where(hit, 1.0, 0.0)
                cum = jnp.dot(onehot.astype(bf16), tri, preferred_element_type=f32)
                base = (carry + start)[:, 0:1]
                out.append(jnp.sum(jnp.where(hit, cum + base, 0.0), axis=0, keepdims=True))
                carry = carry + jnp.sum(onehot, axis=1, keepdims=True)
            dest_ref[j] = jnp.concatenate(out, axis=0).astype(i32)
        carry_sc[...] = carry


def _dispatch_plan(e3):
    n_tiles, _, tm = e3.shape
    n_blocks = (2 * n_tiles * tm + N_EXPERTS * (MOE_BLOCK - 1) + MOE_BLOCK - 1) // MOE_BLOCK
    nb_pad = (n_blocks + LANES - 1) // LANES * LANES
    tps = max(c for c in PLAN_TILES_PER_STEP if n_tiles % c == 0)
    dest3, be, nu = pl.pallas_call(
        _plan_kernel,
        out_shape=(jax.ShapeDtypeStruct(e3.shape, i32),
                   jax.ShapeDtypeStruct((1, nb_pad), i32),
                   jax.ShapeDtypeStruct((1, LANES), i32)),
        grid=(2, n_tiles // tps),
        in_specs=[pl.BlockSpec((tps, 2, tm), lambda p, i: (i, 0, 0))],
        out_specs=(pl.BlockSpec((tps, 2, tm), lambda p, i: (p * i, 0, 0)),
                   pl.BlockSpec((1, nb_pad), lambda p, i: (0, 0)),
                   pl.BlockSpec((1, LANES), lambda p, i: (0, 0))),
        scratch_shapes=[pltpu.VMEM((N_EXPERTS, LANES), f32), pltpu.VMEM((N_EXPERTS, LANES), f32)],
        compiler_params=_cparams("arbitrary", "arbitrary"),
        name="route_plan",
    )(e3)
    return be[0, :n_blocks], nu[0, :1], dest3, n_blocks


def _dispatch_kernel(info_ref, nu_ref, dest_ref, hp_ref, xs_hbm, buf, zbuf, sem, zsem):
    i = pl.program_id(0)
    n = pl.num_programs(0)
    tm = hp_ref.shape[0]
    n_blocks = xs_hbm.shape[0] // MOE_BLOCK

    @pl.when(i == 0)
    def _():
        zbuf[...] = jnp.zeros(zbuf.shape, u32)

        def fill(b):
            return pltpu.make_async_copy(
                zbuf, xs_hbm.at[pl.ds(pl.multiple_of(b * MOE_BLOCK, MOE_BLOCK), MOE_BLOCK), :], zsem.at[0])

        def needs_fill(b):
            next_first = (info_ref[jnp.minimum(b + 1, n_blocks - 1)] >> INFO_FIRST) & 1
            return (b >= nu_ref[0] - 1) | (next_first == 1)

        def start(b, c):
            @pl.when(needs_fill(b))
            def _():
                fill(b).start()
            return c

        def wait(b, c):
            @pl.when(needs_fill(b))
            def _():
                fill(b).wait()
            return c

        lax.fori_loop(0, n_blocks, start, 0)
        lax.fori_loop(0, n_blocks, wait, 0)

    def issue(s):
        buf[s] = hp_ref[...]
        for kk in range(2):
            for r in range(tm):
                pltpu.make_async_copy(
                    buf.at[s, pl.ds(r, 1), :], xs_hbm.at[pl.ds(dest_ref[0, kk, r], 1), :], sem.at[s]).start()

    def drain(s):
        for _ in range(2 * tm):
            pltpu.make_async_copy(buf.at[s, pl.ds(0, 1), :], xs_hbm.at[pl.ds(0, 1), :], sem.at[s]).wait()

    for s in range(2):
        @pl.when(i % 2 == s)
        def _():
            issue(s)

            @pl.when(i > 0)
            def _():
                drain(1 - s)

            @pl.when(i == n - 1)
            def _():
                drain(s)


def _dispatch(hp, dest3, blk_info, n_used):
    t, dw = hp.shape
    n_tiles = t // ROW_TILE
    slots = blk_info.shape[0] * MOE_BLOCK
    grid_spec = pltpu.PrefetchScalarGridSpec(
        num_scalar_prefetch=2,
        grid=(n_tiles,),
        in_specs=[
            pl.BlockSpec((1, 2, ROW_TILE), lambda i, info, nu: (i, 0, 0), memory_space=pltpu.SMEM),
            pl.BlockSpec((ROW_TILE, dw), lambda i, info, nu: (i, 0)),
        ],
        out_specs=pl.BlockSpec(memory_space=pl.ANY),
        scratch_shapes=[pltpu.VMEM((2, ROW_TILE, dw), u32), pltpu.VMEM((MOE_BLOCK, dw), u32),
                        pltpu.SemaphoreType.DMA((2,)), pltpu.SemaphoreType.DMA((1,))],
    )
    return pl.pallas_call(
        _dispatch_kernel,
        out_shape=jax.ShapeDtypeStruct((slots, dw), u32),
        grid_spec=grid_spec,
        compiler_params=_cparams("arbitrary"),
        name="moe_dispatch",
    )(blk_info, n_used, dest3, hp)


def _ffn_kernel(info_ref, nu_ref, xs_ref, wg_hbm, wu_hbm, wd_hbm, y_ref, wg_sc, wu_sc, wd_sc, sem, *, layer):
    b = pl.program_id(0)
    info = info_ref[b]
    expert = info & ((1 << INFO_FIRST) - 1)
    first = (info >> INFO_FIRST) & 1
    slot = (info >> INFO_SLOT) & 1
    nxt = info >> INFO_NEXT
    live = b < nu_ref[0]

    def weight_copies(e, s):
        return [pltpu.make_async_copy(src.at[layer, e], dst.at[s], sem.at[s, j])
                for j, (src, dst) in enumerate(((wg_hbm, wg_sc), (wu_hbm, wu_sc), (wd_hbm, wd_sc)))]

    @pl.when(b == 0)
    def _():
        for c in weight_copies(expert, 0):
            c.start()

    @pl.when(live & (first == 1) & (nxt != NO_EXPERT))
    def _():
        for c in weight_copies(nxt, 1 - slot):
            c.start()

    @pl.when(live & (first == 1))
    def _():
        for c in weight_copies(expert, slot):
            c.wait()

    def compute(s):
        w = xs_ref[...]
        lo = lax.bitcast_convert_type(w << 16, f32)
        hi = lax.bitcast_convert_type(w & jnp.uint32(0xFFFF0000), f32)
        xb = jnp.concatenate([lo, hi], axis=1).astype(bf16)
        a = jnp.dot(xb, wg_sc[s], preferred_element_type=f32)
        u = jnp.dot(xb, wu_sc[s], preferred_element_type=f32)
        z = (a * jax.nn.sigmoid(a) * u).astype(bf16)
        y_ref[...] = jnp.dot(z, wd_sc[s], preferred_element_type=f32)

    for s in range(2):
        @pl.when(live & (slot == s))
        def _():
            compute(s)

    @pl.when(jnp.logical_not(live))
    def _():
        y_ref[...] = jnp.zeros(y_ref.shape, f32)


def _expert_ffn(xs, blk_expert, n_used, layer, wg, wu, wd):
    slots, dw = xs.shape
    d = 2 * dw
    de = wg.shape[3]
    n_blocks = slots // MOE_BLOCK
    grid_spec = pltpu.PrefetchScalarGridSpec(
        num_scalar_prefetch=2,
        grid=(n_blocks,),
        in_specs=[
            pl.BlockSpec((MOE_BLOCK, dw), lambda b, be, nu: (b, 0)),
            pl.BlockSpec(memory_space=pl.ANY),
            pl.BlockSpec(memory_space=pl.ANY),
            pl.BlockSpec(memory_space=pl.ANY),
        ],
        out_specs=pl.BlockSpec((MOE_BLOCK, d), lambda b, be, nu: (b, 0)),
        scratch_shapes=[pltpu.VMEM((2, d, de), bf16), pltpu.VMEM((2, d, de), bf16), pltpu.VMEM((2, de, d), bf16),
                        pltpu.SemaphoreType.DMA((2, 3))],
    )
    return pl.pallas_call(
        functools.partial(_ffn_kernel, layer=layer),
        out_shape=jax.ShapeDtypeStruct((slots, d), f32),
        grid_spec=grid_spec,
        compiler_params=_cparams("arbitrary"),
        name="expert_ffn",
    )(blk_expert, n_used, xs, wg, wu, wd)


def _fin_kernel(pos_cur, pos_nxt, x1_ref, gate_ref, mod_ref, lng_ref, lnb_ref, y_hbm, o_ref, ybuf, sem):
    i = pl.program_id(0)
    n = pl.num_programs(0)
    tm = x1_ref.shape[0]
    slot = i % 2

    def copy(pos_ref, kk, r, s):
        return pltpu.make_async_copy(
            y_hbm.at[pl.ds(pos_ref[0, kk, r], 1), :], ybuf.at[s, kk, pl.ds(r, 1), :], sem.at[s])

    def start_gather(pos_ref, s):
        for kk in range(2):
            for r in range(tm):
                copy(pos_ref, kk, r, s).start()

    @pl.when(i == 0)
    def _():
        start_gather(pos_cur, 0)

    @pl.when(i + 1 < n)
    def _():
        start_gather(pos_nxt, 1 - slot)

    for kk in range(2):
        for r in range(tm):
            copy(pos_cur, kk, r, slot).wait()
    g2 = mod_ref[0, 5:6, :]
    gate = gate_ref[...]
    f = gate[:, 0:1] * ybuf[slot, 0] + gate[:, 1:2] * ybuf[slot, 1]
    z = DEEPNORM_ALPHA * x1_ref[...] + g2 * f
    mu = jnp.mean(z, axis=-1, keepdims=True)
    zc = z - mu
    var = jnp.mean(zc * zc, axis=-1, keepdims=True)
    o_ref[...] = zc * lax.rsqrt(var + LN_EPS) * lng_ref[...] + lnb_ref[...]


def _combine_norm(x1, gate, pos3, y_sorted, mod_l, ln_g, ln_b, seg_of_tile):
    t, d = x1.shape
    n_tiles = t // ROW_TILE
    return pl.pallas_call(
        _fin_kernel,
        out_shape=jax.ShapeDtypeStruct((t, d), f32),
        grid=(n_tiles,),
        in_specs=[
            pl.BlockSpec((1, 2, ROW_TILE), lambda i: (i, 0, 0), memory_space=pltpu.SMEM),
            pl.BlockSpec((1, 2, ROW_TILE), lambda i: (jnp.minimum(i + 1, n_tiles - 1), 0, 0),
                         memory_space=pltpu.SMEM),
            pl.BlockSpec((ROW_TILE, d), lambda i: (i, 0)),
            pl.BlockSpec((ROW_TILE, GATE_COLS), lambda i: (i, 0)),
            pl.BlockSpec((1, 6, d), lambda i: (seg_of_tile(i), 0, 0)),
            pl.BlockSpec((1, d), lambda i: (0, 0)),
            pl.BlockSpec((1, d), lambda i: (0, 0)),
            pl.BlockSpec(memory_space=pl.ANY),
        ],
        out_specs=pl.BlockSpec((ROW_TILE, d), lambda i: (i, 0)),
        scratch_shapes=[pltpu.VMEM((2, 2, ROW_TILE, d), f32), pltpu.SemaphoreType.DMA((2,))],
        compiler_params=_cparams("arbitrary"),
        name="combine_norm",
    )(pos3, pos3, x1, gate, mod_l, ln_g, ln_b, y_sorted)


def _moe_combine(x1, hp, e_idx, gate, layer, wg_bf, wu_bf, wd_bf, mod_l, ln_g, ln_b, seg_of_tile):
    blk_expert, n_used, dest3, n_blocks = _dispatch_plan(e_idx)
    xs = _dispatch(hp, dest3, blk_expert, n_used)
    y_sorted = _expert_ffn(xs, blk_expert, n_used, layer, wg_bf, wu_bf, wd_bf)
    return _combine_norm(x1, gate, dest3, y_sorted, mod_l, ln_g, ln_b, seg_of_tile)


def _rope_tables(ctx_len, seq_len):
    t = jnp.arange(seq_len)
    row = (t // GRID_W).astype(f32)
    col = (t % GRID_W).astype(f32)
    inv_freq = ROPE_THETA ** (-jnp.arange(ROPE_PAIRS, dtype=f32) / ROPE_PAIRS)
    ang_r = row[:, None] * inv_freq
    ang_c = col[:, None] * inv_freq
    ang = jnp.concatenate([ang_r, ang_r, ang_c, ang_c], axis=-1)
    cos = jnp.cos(ang)
    sin = jnp.sin(ang)
    lane = jnp.arange(HEAD_DIM)
    sign = jnp.where((lane % (2 * ROPE_PAIRS)) < ROPE_PAIRS, -1.0, 1.0).astype(f32)
    cos_t = jnp.concatenate([jnp.ones((ctx_len, HEAD_DIM), f32), cos], axis=0)
    sin_t = jnp.concatenate([jnp.zeros((ctx_len, HEAD_DIM), f32), sin * sign], axis=0)
    return cos_t, sin_t


def kernel(x, c, ctx, c_ctx, w_mod, b_mod, ln_g, ln_b, w_qkv, q_gain, k_gain, w_o, w_pool, pool_scale,
           w_router, router_bias, w_gate, w_up, w_down):
    n_batch, seq_len, d = x.shape
    ctx_len = ctx.shape[1]
    assert w_mod.shape[0] == DEPTH and d == N_HEADS * HEAD_DIM
    assert ctx_len % ROW_TILE == 0 and seq_len % ROW_TILE == 0 and 1 + n_batch <= MOD_ROWS
    rows = _Rows(n_batch, ctx_len, seq_len)

    cond = jnp.concatenate([c_ctx[None, :], c], axis=0)
    mod = _modulation(cond, w_mod, b_mod).reshape(DEPTH, MOD_ROWS, 6, d)

    wr_t = w_router.reshape(d, N_EXPERT_GROUPS, EXPERTS_PER_GROUP).transpose(2, 1, 0).reshape(N_EXPERTS, d)
    wr_hi = wr_t.astype(bf16)
    wr_hl = jnp.concatenate([wr_hi, (wr_t - wr_hi.astype(f32)).astype(bf16)], axis=0)
    rb_p = router_bias.reshape(N_EXPERT_GROUPS, EXPERTS_PER_GROUP).T.reshape(N_EXPERTS, 1)
    cos_t, sin_t = _rope_tables(ctx_len, seq_len)

    ctx2 = ctx.reshape(n_batch * ctx_len, d)
    x2 = x.reshape(n_batch * seq_len, d)
    q, k, v = _qkv_proj(ctx2, x2, mod[0], w_qkv[0].astype(bf16), q_gain[0:1], k_gain[0:1], cos_t, sin_t, rows)
    bound = (Q_SCALE * HEAD_DIM * SCORE_BOUND_SLACK) * jnp.max(jnp.abs(q_gain[0])) * jnp.max(jnp.abs(k_gain[0]))
    attn_args = (bound.reshape(1, 1), q, k, v, w_gate, w_up, w_down)
    o, wg_bf, wu_bf, wd_bf = lax.cond(
        bound <= SCORE_BOUND_MAX,
        lambda a: _attention(*a, rows, True),
        lambda a: _attention(*a, rows, False),
        attn_args)
    x1, hp, e_idx, gate = _attn_post(ctx2, x2, o, w_o[0].astype(bf16), mod[0], ln_g[0, 0:1], ln_b[0, 0:1],
                                     wr_hl, rb_p, rows)
    x_all = _moe_combine(x1, hp, e_idx, gate, 0, wg_bf, wu_bf, wd_bf, mod[0], ln_g[0, 1:2], ln_b[0, 1:2],
                         rows.seg)

    x1, hp, e_idx, gate = _pool_post(x_all, w_pool[0].astype(bf16), pool_scale[0:1], mod[1],
                                     ln_g[1, 0:1], ln_b[1, 0:1], wr_hl, rb_p, rows)
    out = _moe_combine(x1, hp, e_idx, gate, 1, wg_bf, wu_bf, wd_bf, mod[1], ln_g[1, 1:2], ln_b[1, 1:2],
                       lambda i: 1 + i // rows.n_seq_tiles)
    return out.reshape(n_batch, seq_len, d)
```

```python
import functools

import jax
import jax.numpy as jnp
from jax import lax
from jax.experimental import pallas as pl
from jax.experimental.pallas import tpu as pltpu

N_HEADS = 16
N_KV_HEADS = 4
HEAD_DIM = 128
Q_PER_KV = N_HEADS // N_KV_HEADS
GRID_W = 64
ROPE_THETA = 10000.0
ROPE_PAIRS = HEAD_DIM // 4
ATTN_SCALE = HEAD_DIM ** -0.5
Q_SCALE = ATTN_SCALE * 1.4426950408889634
POOL_WINDOWS = (2, 4, 8, 16)
POOL_HALO = 8
N_EXPERTS = 32
N_EXPERT_GROUPS = 8
EXPERTS_PER_GROUP = N_EXPERTS // N_EXPERT_GROUPS
DEPTH = 2
DEEPNORM_ALPHA = (2 * DEPTH) ** 0.25
LN_EPS = 1e-6
RMS_EPS = 1e-6

LANES = 128
SUBLANES = 8
ROW_TILE = 256
ATTN_TK = 2048
ATTN_ROWS = 128
SCORE_BOUND_SLACK = 1.02
SCORE_BOUND_MAX = 56.0
POST_ROWS = 128
MOE_BLOCK = 256
PLAN_TILES_PER_STEP = (11, 8, 6, 4, 3, 2, 1)
NO_EXPERT = N_EXPERTS
INFO_FIRST, INFO_SLOT, INFO_NEXT = 8, 9, 10
GATE_COLS = 8
MOD_ROWS = 8
MOD_TN = 1024
VMEM_LIMIT = 56 * 1024 * 1024

f32 = jnp.float32
bf16 = jnp.bfloat16
u32 = jnp.uint32
i32 = jnp.int32


def _cparams(*sem, flags=None):
    return pltpu.CompilerParams(dimension_semantics=sem, vmem_limit_bytes=VMEM_LIMIT, flags=flags)


def _nt_dot(a, b):
    return lax.dot_general(a, b, (((1,), (1,)), ((), ())), preferred_element_type=f32)


def _mod_kernel(c_ref, w_ref, b_ref, o_ref, s_sc, *, n_rows):
    d = w_ref.shape[1]
    tn = w_ref.shape[2]
    nj = tn // LANES

    @pl.when((pl.program_id(0) == 0) & (pl.program_id(1) == 0))
    def _():
        c = c_ref[...]
        s_sc[...] = c * jax.nn.sigmoid(c)

    def body(k, accs):
        r0 = pl.multiple_of(k * SUBLANES, SUBLANES)
        w8 = w_ref[0, pl.ds(r0, SUBLANES), :]
        out = []
        for r in range(n_rows):
            s8 = s_sc[r, pl.ds(r0, SUBLANES), :]
            for j in range(nj):
                out.append(accs[r * nj + j] + s8 * w8[:, j * LANES:(j + 1) * LANES])
        return tuple(out)

    init = tuple(jnp.zeros((SUBLANES, LANES), f32) for _ in range(n_rows * nj))
    accs = lax.fori_loop(0, d // SUBLANES, body, init, unroll=4)
    rows = []
    for r in range(n_rows):
        row = jnp.concatenate(
            [jnp.sum(accs[r * nj + j], axis=0, keepdims=True) for j in range(nj)], axis=1)
        rows.append(row + b_ref[0])
    rows.append(jnp.zeros((MOD_ROWS - n_rows, tn), f32))
    o_ref[0] = jnp.concatenate(rows, axis=0)


def _modulation(cond, w_mod, b_mod):
    n_rows, d = cond.shape
    depth, _, n6 = w_mod.shape
    c_b = jnp.broadcast_to(cond[:, :, None], (n_rows, d, LANES))
    return pl.pallas_call(
        functools.partial(_mod_kernel, n_rows=n_rows),
        out_shape=jax.ShapeDtypeStruct((depth, MOD_ROWS, n6), f32),
        grid=(depth, n6 // MOD_TN),
        in_specs=[
            pl.BlockSpec((n_rows, d, LANES), lambda l, j: (0, 0, 0)),
            pl.BlockSpec((1, d, MOD_TN), lambda l, j: (l, 0, j)),
            pl.BlockSpec((1, 1, MOD_TN), lambda l, j: (l, 0, j)),
        ],
        out_specs=pl.BlockSpec((1, MOD_ROWS, MOD_TN), lambda l, j: (l, 0, j)),
        scratch_shapes=[pltpu.VMEM((n_rows, d, LANES), f32)],
        compiler_params=_cparams("arbitrary", "arbitrary"),
        name="adaln_mod",
    )(c_b, w_mod, b_mod.reshape(depth, 1, n6))


class _Rows:
    def __init__(self, n_batch, ctx_len, seq_len):
        self.n_batch = n_batch
        self.ctx_len = ctx_len
        self.seq_len = seq_len
        self.n_ctx_tiles = ctx_len // ROW_TILE
        self.n_seq_tiles = seq_len // ROW_TILE
        self.tiles_per_batch = self.n_ctx_tiles + self.n_seq_tiles
        self.n_tiles = n_batch * self.tiles_per_batch
        self.t_all = self.n_tiles * ROW_TILE

    def is_ctx(self, i):
        return i % self.tiles_per_batch < self.n_ctx_tiles

    def seg(self, i):
        return jnp.where(self.is_ctx(i), 0, 1 + i // self.tiles_per_batch)

    def ctx_tile(self, i):
        r = i % self.tiles_per_batch
        return (i // self.tiles_per_batch) * self.n_ctx_tiles + jnp.minimum(r, self.n_ctx_tiles - 1)

    def lat_tile(self, i):
        r = i % self.tiles_per_batch
        return (i // self.tiles_per_batch) * self.n_seq_tiles + jnp.maximum(r - self.n_ctx_tiles, 0)


def _qkv_kernel(xc_ref, xl_ref, mod_ref, w_ref, qg_ref, kg_ref, cos_ref, sin_ref, q_ref, k_ref, v_ref, *, rows):
    tm, d = xl_ref.shape
    kv_dim = k_ref.shape[1]
    x = jnp.where(rows.is_ctx(pl.program_id(0)), xc_ref[...], xl_ref[...])
    sh1 = mod_ref[0, 0:1, :]
    sc1 = mod_ref[0, 1:2, :]
    u = (x * (1.0 + sc1) + sh1).astype(bf16)
    qkv = jnp.dot(u, w_ref[...], preferred_element_type=f32)
    cos = cos_ref[...]
    sin = sin_ref[...]
    lane = lax.broadcasted_iota(i32, (tm, HEAD_DIM), 1)
    first = (lane % (2 * ROPE_PAIRS)) < ROPE_PAIRS

    def norm_rope(xh, gain, scale):
        ms = jnp.mean(xh * xh, axis=-1, keepdims=True)
        xn = xh * lax.rsqrt(ms + RMS_EPS) * gain
        rot = jnp.where(first, pltpu.roll(xn, HEAD_DIM - ROPE_PAIRS, 1), pltpu.roll(xn, ROPE_PAIRS, 1))
        y = xn * cos + rot * sin
        return y * scale if scale != 1.0 else y

    qg = qg_ref[...]
    kg = kg_ref[...]
    for h in range(N_HEADS):
        sl = slice(h * HEAD_DIM, (h + 1) * HEAD_DIM)
        q_ref[:, sl] = norm_rope(qkv[:, sl], qg, Q_SCALE).astype(bf16)
    ones = jnp.ones((tm, HEAD_DIM), bf16)
    for h in range(N_KV_HEADS):
        sl = slice(h * HEAD_DIM, (h + 1) * HEAD_DIM)
        k_ref[:, sl] = norm_rope(qkv[:, d + h * HEAD_DIM:d + (h + 1) * HEAD_DIM], kg, 1.0).astype(bf16)
        v0 = d + kv_dim + h * HEAD_DIM
        v_ref[:, 2 * h * HEAD_DIM:(2 * h + 1) * HEAD_DIM] = qkv[:, v0:v0 + HEAD_DIM].astype(bf16)
        v_ref[:, (2 * h + 1) * HEAD_DIM:(2 * h + 2) * HEAD_DIM] = ones


def _qkv_proj(ctx2, x2, mod_l, w_qkv_bf, q_gain, k_gain, cos_t, sin_t, rows):
    d = x2.shape[1]
    kv_dim = N_KV_HEADS * HEAD_DIM
    tpb = rows.tiles_per_batch
    return pl.pallas_call(
        functools.partial(_qkv_kernel, rows=rows),
        out_shape=(jax.ShapeDtypeStruct((rows.t_all, d), bf16),
                   jax.ShapeDtypeStruct((rows.t_all, kv_dim), bf16),
                   jax.ShapeDtypeStruct((rows.t_all, 2 * kv_dim), bf16)),
        grid=(rows.n_tiles,),
        in_specs=[
            pl.BlockSpec((ROW_TILE, d), lambda i: (rows.ctx_tile(i), 0)),
            pl.BlockSpec((ROW_TILE, d), lambda i: (rows.lat_tile(i), 0)),
            pl.BlockSpec((1, 6, d), lambda i: (rows.seg(i), 0, 0)),
            pl.BlockSpec((d, d + 2 * kv_dim), lambda i: (0, 0)),
            pl.BlockSpec((1, HEAD_DIM), lambda i: (0, 0)),
            pl.BlockSpec((1, HEAD_DIM), lambda i: (0, 0)),
            pl.BlockSpec((ROW_TILE, HEAD_DIM), lambda i: (i % tpb, 0)),
            pl.BlockSpec((ROW_TILE, HEAD_DIM), lambda i: (i % tpb, 0)),
        ],
        out_specs=(pl.BlockSpec((ROW_TILE, d), lambda i: (i, 0)),
                   pl.BlockSpec((ROW_TILE, kv_dim), lambda i: (i, 0)),
                   pl.BlockSpec((ROW_TILE, 2 * kv_dim), lambda i: (i, 0))),
        compiler_params=_cparams("arbitrary"),
        name="qkv_rope",
    )(ctx2, x2, mod_l, w_qkv_bf, q_gain, k_gain, cos_t, sin_t)


def _attn_kernel(bound_ref, q_ref, k_ref, v_ref, wg_ref, wu_ref, wd_ref, o_ref, wgo_ref, wuo_ref, wdo_ref,
                 *, ctx_len, tk, rs, n_ctx_tiles, bounded):
    tq = q_ref.shape[0]
    n_lat_chunks = (k_ref.shape[0] - ctx_len) // tk
    qi = pl.program_id(2)
    subs = [(h, r0) for h in range(Q_PER_KV) for r0 in range(0, tq, rs)]

    def attend(chunks):
        states = [None] * len(subs)
        for ci, (off, size) in enumerate(chunks):
            for src, dst in ((wg_ref, wgo_ref), (wu_ref, wuo_ref), (wd_ref, wdo_ref)):
                n = src.shape[0] // len(chunks)
                dst[ci * n:(ci + 1) * n, :] = src[ci * n:(ci + 1) * n, :].astype(bf16)
            kc = k_ref[off:off + size, :]
            vc = v_ref[off:off + size, :]
            for i, (h, r0) in enumerate(subs):
                s = _nt_dot(q_ref[r0:r0 + rs, h * HEAD_DIM:(h + 1) * HEAD_DIM], kc)
                if bounded:
                    pv = jnp.dot(jnp.exp2(s - bound_ref[0, 0]).astype(bf16), vc, preferred_element_type=f32)
                    states[i] = pv if states[i] is None else states[i] + pv
                    continue
                mx = s[:, 0:LANES]
                for j in range(1, size // LANES):
                    mx = jnp.maximum(mx, s[:, j * LANES:(j + 1) * LANES])
                m_cur = jnp.max(mx, axis=-1, keepdims=True)
                if states[i] is None:
                    m_new = jnp.broadcast_to(m_cur, (rs, LANES))
                else:
                    m_prev, acc_prev = states[i]
                    m_new = jnp.maximum(m_prev, m_cur)
                p = jnp.concatenate(
                    [jnp.exp2(s[:, j * LANES:(j + 1) * LANES] - m_new) for j in range(size // LANES)],
                    axis=1).astype(bf16)
                acc = jnp.dot(p, vc, preferred_element_type=f32)
                if states[i] is not None:
                    alpha = jnp.exp2(m_prev - m_new)
                    acc = acc_prev * jnp.concatenate([alpha, alpha], axis=1) + acc
                states[i] = (m_new, acc)
        for i, (h, r0) in enumerate(subs):
            acc = states[i] if bounded else states[i][1]
            o_ref[r0:r0 + rs, h * HEAD_DIM:(h + 1) * HEAD_DIM] = (
                acc[:, :HEAD_DIM] / acc[:, HEAD_DIM:]).astype(bf16)

    @pl.when(qi < n_ctx_tiles)
    def _():
        attend([(0, ctx_len)])

    @pl.when(qi >= n_ctx_tiles)
    def _():
        attend([(0, ctx_len + tk)] + [(ctx_len + c * tk, tk) for c in range(1, n_lat_chunks)])


def _attention(bound, q, k, v, w_gate, w_up, w_down, rows, bounded):
    t_all, d = q.shape
    tpb = rows.tiles_per_batch
    rows_per_batch = t_all // rows.n_batch
    gw = Q_PER_KV * HEAD_DIM
    tk = min(ATTN_TK, rows.seq_len)
    assert rows.seq_len % tk == 0

    n_steps = rows.n_batch * N_KV_HEADS * tpb
    wg2 = w_gate.reshape(-1, w_gate.shape[-1])
    wu2 = w_up.reshape(-1, w_up.shape[-1])
    wd2 = w_down.reshape(-1, w_down.shape[-1])
    n_slabs = 1 << (n_steps.bit_length() - 1)
    slab_g = wg2.shape[0] // n_slabs
    slab_d = wd2.shape[0] // n_slabs
    assert wg2.shape[0] == n_slabs * slab_g and wd2.shape[0] == n_slabs * slab_d and wu2.shape == wg2.shape
    assert slab_g % SUBLANES == 0 and slab_d % SUBLANES == 0

    def slab(b, g, i):
        return (jnp.minimum((b * N_KV_HEADS + g) * tpb + i, n_slabs - 1), 0)

    kern = functools.partial(_attn_kernel, ctx_len=rows.ctx_len, tk=tk, rs=ATTN_ROWS,
                             n_ctx_tiles=rows.n_ctx_tiles, bounded=bounded)
    o, wg_bf, wu_bf, wd_bf = pl.pallas_call(
        kern,
        out_shape=(jax.ShapeDtypeStruct((t_all, d), bf16),
                   jax.ShapeDtypeStruct(wg2.shape, bf16),
                   jax.ShapeDtypeStruct(wu2.shape, bf16),
                   jax.ShapeDtypeStruct(wd2.shape, bf16)),
        grid=(rows.n_batch, N_KV_HEADS, tpb),
        in_specs=[
            pl.BlockSpec((1, 1), lambda b, g, i: (0, 0), memory_space=pltpu.SMEM),
            pl.BlockSpec((ROW_TILE, gw), lambda b, g, i: (b * tpb + i, g)),
            pl.BlockSpec((rows_per_batch, HEAD_DIM), lambda b, g, i: (b, g)),
            pl.BlockSpec((rows_per_batch, 2 * HEAD_DIM), lambda b, g, i: (b, g)),
            pl.BlockSpec((slab_g, wg2.shape[1]), slab),
            pl.BlockSpec((slab_g, wu2.shape[1]), slab),
            pl.BlockSpec((slab_d, wd2.shape[1]), slab),
        ],
        out_specs=(pl.BlockSpec((ROW_TILE, gw), lambda b, g, i: (b * tpb + i, g)),
                   pl.BlockSpec((slab_g, wg2.shape[1]), slab),
                   pl.BlockSpec((slab_g, wu2.shape[1]), slab),
                   pl.BlockSpec((slab_d, wd2.shape[1]), slab)),
        compiler_params=_cparams("arbitrary", "arbitrary", "arbitrary"),
        name="gqa_attention_bounded" if bounded else "gqa_attention_online",
    )(bound, q, k, v, wg2, wu2, wd2)
    return o, wg_bf.reshape(w_gate.shape), wu_bf.reshape(w_up.shape), wd_bf.reshape(w_down.shape)


def _post_epilogue(x, y, r0, mod_ref, lng_ref, lnb_ref, wr_ref, rb_ref, x1_ref, hp_ref, e_ref, g_ref):
    tm, d = x.shape
    rsl = slice(r0, r0 + tm)
    g1 = mod_ref[0, 2:3, :]
    sh2 = mod_ref[0, 3:4, :]
    sc2 = mod_ref[0, 4:5, :]
    z = DEEPNORM_ALPHA * x + g1 * y
    mu = jnp.mean(z, axis=-1, keepdims=True)
    zc = z - mu
    var = jnp.mean(zc * zc, axis=-1, keepdims=True)
    x1 = zc * lax.rsqrt(var + LN_EPS) * lng_ref[...] + lnb_ref[...]
    h = x1 * (1.0 + sc2) + sh2
    x1_ref[rsl, :] = x1

    h_hi = h.astype(bf16)
    h_hi32 = h_hi.astype(f32)
    bits = lax.bitcast_convert_type(h_hi32, u32)
    hp_ref[rsl, :] = (bits[:, :d // 2] >> 16) | bits[:, d // 2:]

    h_lo = (h - h_hi32).astype(bf16)
    wr = wr_ref[...]
    a2 = _nt_dot(wr, h_hi)
    lt = a2[:N_EXPERTS] + a2[N_EXPERTS:] + _nt_dot(wr[:N_EXPERTS], h_lo)
    aff = jax.nn.sigmoid(lt)
    sel = aff + rb_ref[...]
    ng = N_EXPERT_GROUPS
    a = [aff[j * ng:(j + 1) * ng] for j in range(EXPERTS_PER_GROUP)]
    s = [sel[j * ng:(j + 1) * ng] for j in range(EXPERTS_PER_GROUP)]
    gs = None
    for i in range(EXPERTS_PER_GROUP):
        for j in range(i + 1, EXPERTS_PER_GROUP):
            pij = s[i] + s[j]
            gs = pij if gs is None else jnp.maximum(gs, pij)
    gio = lax.broadcasted_iota(i32, gs.shape, 0)
    gmax = jnp.max(gs, axis=0, keepdims=True)
    g_idx = jnp.min(jnp.where(gs == gmax, gio, ng), axis=0, keepdims=True)
    onehot = gio == g_idx
    sin_g = [jnp.sum(jnp.where(onehot, s[j], 0.0), axis=0, keepdims=True) for j in range(EXPERTS_PER_GROUP)]
    aff_g = [jnp.sum(jnp.where(onehot, a[j], 0.0), axis=0, keepdims=True) for j in range(EXPERTS_PER_GROUP)]
    loc = [jnp.zeros_like(g_idx), jnp.zeros_like(g_idx)]
    gat = [jnp.zeros_like(gmax), jnp.zeros_like(gmax)]
    for j in range(EXPERTS_PER_GROUP):
        rank = jnp.zeros_like(g_idx)
        for i in range(EXPERTS_PER_GROUP):
            if i == j:
                continue
            ahead = (sin_g[i] >= sin_g[j]) if i < j else (sin_g[i] > sin_g[j])
            rank = rank + ahead.astype(i32)
        for kk in range(2):
            hit = rank == kk
            loc[kk] = jnp.where(hit, j, loc[kk])
            gat[kk] = jnp.where(hit, aff_g[j], gat[kk])
    den = gat[0] + gat[1]
    e_ref[0, :, rsl] = jnp.concatenate(
        [g_idx * EXPERTS_PER_GROUP + loc[0], g_idx * EXPERTS_PER_GROUP + loc[1]], axis=0)
    g8 = jnp.concatenate([gat[0] / den, gat[1] / den, jnp.zeros((GATE_COLS - 2, tm), f32)], axis=0)
    g_ref[rsl, :] = g8.T


def _attn_post_kernel(xc_ref, xl_ref, o_ref, wo_ref, mod_ref, lng_ref, lnb_ref, wr_ref, rb_ref,
                      x1_ref, hp_ref, e_ref, g_ref, ya_sc, yb_sc, *, rows):
    s = pl.program_id(0)
    tile = jnp.maximum(s - 1, 0)

    @pl.when(s == 0)
    def _():
        yb_sc[...] = jnp.zeros(yb_sc.shape, f32)

    def body(y_write, y_read):
        y_write[...] = jnp.dot(o_ref[...], wo_ref[...], preferred_element_type=f32)
        x = jnp.where(rows.is_ctx(tile), xc_ref[...], xl_ref[...])
        _post_epilogue(x, y_read[...], 0, mod_ref, lng_ref, lnb_ref, wr_ref, rb_ref,
                       x1_ref, hp_ref, e_ref, g_ref)

    @pl.when(s % 2 == 0)
    def _():
        body(ya_sc, yb_sc)

    @pl.when(s % 2 == 1)
    def _():
        body(yb_sc, ya_sc)


def _post_out(n_rows, d, tile_of):
    shapes = (jax.ShapeDtypeStruct((n_rows, d), f32),
              jax.ShapeDtypeStruct((n_rows, d // 2), u32),
              jax.ShapeDtypeStruct((n_rows // ROW_TILE, 2, ROW_TILE), i32),
              jax.ShapeDtypeStruct((n_rows, GATE_COLS), f32))
    specs = (pl.BlockSpec((ROW_TILE, d), lambda i: (tile_of(i), 0)),
             pl.BlockSpec((ROW_TILE, d // 2), lambda i: (tile_of(i), 0)),
             pl.BlockSpec((1, 2, ROW_TILE), lambda i: (tile_of(i), 0, 0)),
             pl.BlockSpec((ROW_TILE, GATE_COLS), lambda i: (tile_of(i), 0)))
    return shapes, specs


def _router_specs(d):
    return [pl.BlockSpec((2 * N_EXPERTS, d), lambda i: (0, 0)),
            pl.BlockSpec((N_EXPERTS, 1), lambda i: (0, 0))]


def _attn_post(ctx2, x2, o, w_o_bf, mod_l, ln_g, ln_b, wr_hl, rb_p, rows):
    d = x2.shape[1]
    n = rows.n_tiles

    def prev(i):
        return jnp.maximum(i - 1, 0)

    shapes, specs = _post_out(rows.t_all, d, prev)
    return pl.pallas_call(
        functools.partial(_attn_post_kernel, rows=rows),
        out_shape=shapes,
        grid=(n + 1,),
        in_specs=[
            pl.BlockSpec((ROW_TILE, d), lambda i: (rows.ctx_tile(prev(i)), 0)),
            pl.BlockSpec((ROW_TILE, d), lambda i: (rows.lat_tile(prev(i)), 0)),
            pl.BlockSpec((ROW_TILE, d), lambda i: (jnp.minimum(i, n - 1), 0)),
            pl.BlockSpec((d, d), lambda i: (0, 0)),
            pl.BlockSpec((1, 6, d), lambda i: (rows.seg(prev(i)), 0, 0)),
            pl.BlockSpec((1, d), lambda i: (0, 0)),
            pl.BlockSpec((1, d), lambda i: (0, 0)),
        ] + _router_specs(d),
        out_specs=specs,
        scratch_shapes=[pltpu.VMEM((ROW_TILE, d), f32), pltpu.VMEM((ROW_TILE, d), f32)],
        compiler_params=_cparams("arbitrary"),
        name="attn_out_norm_route",
    )(ctx2, x2, o, w_o_bf, mod_l, ln_g, ln_b, wr_hl, rb_p)


def _pool_post_kernel(x_ref, xp_ref, xn_ref, wp_ref, ps_ref, mod_ref, lng_ref, lnb_ref, wr_ref, rb_ref,
                      x1_ref, hp_ref, e_ref, g_ref, *, seq_len, tiles_per_seq):
    tm, d = x_ref.shape
    gd = d // len(POOL_WINDOWS)
    r = pl.program_id(0) % tiles_per_seq
    sh1 = mod_ref[0, 0:1, :]
    sc1 = mod_ref[0, 1:2, :]
    x = x_ref[...]
    ext = jnp.concatenate([xp_ref[...], x, xn_ref[...]], axis=0)
    n_ext = tm + 2 * POOL_HALO
    pos = r * tm - POOL_HALO + lax.broadcasted_iota(i32, (n_ext, 1), 0)
    inside = (pos >= 0) & (pos < seq_len)
    u_ext = jnp.where(inside, ext * (1.0 + sc1) + sh1, 0.0)
    t = pos[POOL_HALO:POOL_HALO + tm]
    dlts = []
    for g, w in enumerate(POOL_WINDOWS):
        e = u_ext[:, g * gd:(g + 1) * gd]
        p = e + pltpu.roll(e, 1, 0)
        step = 1
        while 2 * step < w:
            p = pltpu.roll(p, step, 0) + pltpu.roll(p, n_ext - step, 0)
            step *= 2
        cnt = (jnp.clip(t + w // 2, 0, seq_len) - jnp.clip(t - w // 2, 0, seq_len)).astype(f32)
        mean = p[POOL_HALO:POOL_HALO + tm] / cnt
        dlts.append((mean - e[POOL_HALO:POOL_HALO + tm]).astype(bf16))
    for r0 in range(0, tm, POST_ROWS):
        rsl = slice(r0, r0 + POST_ROWS)
        y = jnp.concatenate(
            [jnp.dot(dlts[g][rsl, :], wp_ref[g], preferred_element_type=f32) for g in range(len(POOL_WINDOWS))],
            axis=-1) * ps_ref[...]
        _post_epilogue(x[rsl, :], y, r0, mod_ref, lng_ref, lnb_ref, wr_ref, rb_ref, x1_ref, hp_ref, e_ref, g_ref)


def _pool_post(x_all, w_pool_bf, pool_scale, mod_l, ln_g, ln_b, wr_hl, rb_p, rows):
    t_all, d = x_all.shape
    tiles_per_seq = rows.n_seq_tiles
    n_rows = rows.n_batch * rows.seq_len
    halo_per_tile = ROW_TILE // POOL_HALO
    last_halo = t_all // POOL_HALO - 1
    ng = len(POOL_WINDOWS)

    def src_tile(i):
        return (i // tiles_per_seq) * rows.tiles_per_batch + rows.n_ctx_tiles + i % tiles_per_seq

    shapes, specs = _post_out(n_rows, d, lambda i: i)
    kern = functools.partial(_pool_post_kernel, seq_len=rows.seq_len, tiles_per_seq=tiles_per_seq)
    return pl.pallas_call(
        kern,
        out_shape=shapes,
        grid=(n_rows // ROW_TILE,),
        in_specs=[
            pl.BlockSpec((ROW_TILE, d), lambda i: (src_tile(i), 0)),
            pl.BlockSpec((POOL_HALO, d), lambda i: (jnp.maximum(src_tile(i) * halo_per_tile - 1, 0), 0)),
            pl.BlockSpec((POOL_HALO, d), lambda i: (jnp.minimum((src_tile(i) + 1) * halo_per_tile, last_halo), 0)),
            pl.BlockSpec((ng, d // ng, d // ng), lambda i: (0, 0, 0)),
            pl.BlockSpec((1, d), lambda i: (0, 0)),
            pl.BlockSpec((1, 6, d), lambda i: (1 + i // tiles_per_seq, 0, 0)),
            pl.BlockSpec((1, d), lambda i: (0, 0)),
            pl.BlockSpec((1, d), lambda i: (0, 0)),
        ] + _router_specs(d),
        out_specs=specs,
        compiler_params=_cparams("arbitrary"),
        name="pool_norm_route",
    )(x_all, x_all, x_all, w_pool_bf, pool_scale, mod_l, ln_g, ln_b, wr_hl, rb_p)


def _plan_kernel(e_ref, dest_ref, be_ref, nu_ref, carry_sc, start_sc):
    phase = pl.program_id(0)
    i = pl.program_id(1)
    tm = e_ref.shape[2]
    sub = lax.broadcasted_iota(i32, (N_EXPERTS, tm), 0)
    shift = MOE_BLOCK.bit_length() - 1

    @pl.when((phase == 0) & (i == 0))
    def _():
        carry_sc[...] = jnp.zeros(carry_sc.shape, f32)

    @pl.when(phase == 0)
    def _():
        carry = carry_sc[...]
        for j in range(e_ref.shape[0]):
            for kk in range(2):
                hit = sub == e_ref[j, kk:kk + 1, :]
                carry = carry + jnp.sum(jnp.where(hit, 1.0, 0.0), axis=1, keepdims=True)
        carry_sc[...] = carry

    @pl.when((phase == 1) & (i == 0))
    def _():
        counts = carry_sc[...].astype(i32)
        padded = ((counts + (MOE_BLOCK - 1)) >> shift) << shift
        esub = lax.broadcasted_iota(i32, padded.shape, 0)
        used = jnp.where(padded > 0, 1, 0)
        start = jnp.zeros(padded.shape, i32)
        order = jnp.zeros(padded.shape, i32)
        for e in range(N_EXPERTS - 1):
            start = start + jnp.where(esub > e, padded[e:e + 1, :], 0)
            order = order + jnp.where(esub > e, used[e:e + 1, :], 0)
        ends = start + padded
        nxt = jnp.full(padded.shape, NO_EXPERT, i32)
        cur = jnp.full((1, LANES), NO_EXPERT, i32)
        for e in range(N_EXPERTS - 1, -1, -1):
            nxt = jnp.where(esub == e, cur, nxt)
            cur = jnp.where(used[e:e + 1, :] > 0, e, cur)
        nb_pad = be_ref.shape[1]
        rep = nb_pad // LANES
        lane = lax.broadcasted_iota(i32, (1, nb_pad), 1)
        be = jnp.zeros((1, nb_pad), i32)
        for e in range(N_EXPERTS):
            be = be + jnp.where(jnp.concatenate([ends[e:e + 1, :]] * rep, axis=1) <= lane * MOE_BLOCK, 1, 0)
        be = jnp.minimum(be, N_EXPERTS - 1)
        par_b = jnp.zeros((1, nb_pad), i32)
        nxt_b = jnp.zeros((1, nb_pad), i32)
        for e in range(N_EXPERTS):
            hit = be == e
            par_b = jnp.where(hit, jnp.concatenate([order[e:e + 1, :]] * rep, axis=1) & 1, par_b)
            nxt_b = jnp.where(hit, jnp.concatenate([nxt[e:e + 1, :]] * rep, axis=1), nxt_b)
        first_b = jnp.where((be != pltpu.roll(be, 1, 1)) | (lane == 0), 1, 0)
        be_ref[...] = be | (first_b << INFO_FIRST) | (par_b << INFO_SLOT) | (nxt_b << INFO_NEXT)
        nu_ref[...] = ends[N_EXPERTS - 1:N_EXPERTS, :] >> shift
        start_sc[...] = start.astype(f32)
        carry_sc[...] = jnp.zeros(carry_sc.shape, f32)

    @pl.when(phase == 1)
    def _():
        before = lax.broadcasted_iota(i32, (tm, tm), 0) < lax.broadcasted_iota(i32, (tm, tm), 1)
        tri = jnp.where(before, 1.0, 0.0).astype(bf16)
        carry = carry_sc[...]
        start = start_sc[...]
        for j in range(e_ref.shape[0]):
            out = []
            for kk in range(2):
                hit = sub == e_ref[j, kk:kk + 1, :]
                onehot = jnp.where(hit, 1.0, 0.0)
                cum = jnp.dot(onehot.astype(bf16), tri, preferred_element_type=f32)
                base = (carry + start)[:, 0:1]
                out.append(jnp.sum(jnp.where(hit, cum + base, 0.0), axis=0, keepdims=True))
                carry = carry + jnp.sum(onehot, axis=1, keepdims=True)
            dest_ref[j] = jnp.concatenate(out, axis=0).astype(i32)
        carry_sc[...] = carry


def _dispatch_plan(e3):
    n_tiles, _, tm = e3.shape
    n_blocks = (2 * n_tiles * tm + N_EXPERTS * (MOE_BLOCK - 1) + MOE_BLOCK - 1) // MOE_BLOCK
    nb_pad = (n_blocks + LANES - 1) // LANES * LANES
    tps = max(c for c in PLAN_TILES_PER_STEP if n_tiles % c == 0)
    dest3, be, nu = pl.pallas_call(
        _plan_kernel,
        out_shape=(jax.ShapeDtypeStruct(e3.shape, i32),
                   jax.ShapeDtypeStruct((1, nb_pad), i32),
                   jax.ShapeDtypeStruct((1, LANES), i32)),
        grid=(2, n_tiles // tps),
        in_specs=[pl.BlockSpec((tps, 2, tm), lambda p, i: (i, 0, 0))],
        out_specs=(pl.BlockSpec((tps, 2, tm), lambda p, i: (p * i, 0, 0)),
                   pl.BlockSpec((1, nb_pad), lambda p, i: (0, 0)),
                   pl.BlockSpec((1, LANES), lambda p, i: (0, 0))),
        scratch_shapes=[pltpu.VMEM((N_EXPERTS, LANES), f32), pltpu.VMEM((N_EXPERTS, LANES), f32)],
        compiler_params=_cparams("arbitrary", "arbitrary"),
        name="route_plan",
    )(e3)
    return be[0, :n_blocks], nu[0, :1], dest3, n_blocks


def _dispatch_kernel(info_ref, nu_ref, dest_ref, hp_ref, xs_hbm, buf, zbuf, sem, zsem):
    i = pl.program_id(0)
    n = pl.num_programs(0)
    tm = hp_ref.shape[0]
    n_blocks = xs_hbm.shape[0] // MOE_BLOCK

    @pl.when(i == 0)
    def _():
        zbuf[...] = jnp.zeros(zbuf.shape, u32)

        def fill(b):
            return pltpu.make_async_copy(
                zbuf, xs_hbm.at[pl.ds(pl.multiple_of(b * MOE_BLOCK, MOE_BLOCK), MOE_BLOCK), :], zsem.at[0])

        def needs_fill(b):
            next_first = (info_ref[jnp.minimum(b + 1, n_blocks - 1)] >> INFO_FIRST) & 1
            return (b >= nu_ref[0] - 1) | (next_first == 1)

        def start(b, c):
            @pl.when(needs_fill(b))
            def _():
                fill(b).start()
            return c

        def wait(b, c):
            @pl.when(needs_fill(b))
            def _():
                fill(b).wait()
            return c

        lax.fori_loop(0, n_blocks, start, 0)
        lax.fori_loop(0, n_blocks, wait, 0)

    def issue(s):
        buf[s] = hp_ref[...]
        for kk in range(2):
            for r in range(tm):
                pltpu.make_async_copy(
                    buf.at[s, pl.ds(r, 1), :], xs_hbm.at[pl.ds(dest_ref[0, kk, r], 1), :], sem.at[s]).start()

    def drain(s):
        for _ in range(2 * tm):
            pltpu.make_async_copy(buf.at[s, pl.ds(0, 1), :], xs_hbm.at[pl.ds(0, 1), :], sem.at[s]).wait()

    for s in range(2):
        @pl.when(i % 2 == s)
        def _():
            issue(s)

            @pl.when(i > 0)
            def _():
                drain(1 - s)

            @pl.when(i == n - 1)
            def _():
                drain(s)


def _dispatch(hp, dest3, blk_info, n_used):
    t, dw = hp.shape
    n_tiles = t // ROW_TILE
    slots = blk_info.shape[0] * MOE_BLOCK
    grid_spec = pltpu.PrefetchScalarGridSpec(
        num_scalar_prefetch=2,
        grid=(n_tiles,),
        in_specs=[
            pl.BlockSpec((1, 2, ROW_TILE), lambda i, info, nu: (i, 0, 0), memory_space=pltpu.SMEM),
            pl.BlockSpec((ROW_TILE, dw), lambda i, info, nu: (i, 0)),
        ],
        out_specs=pl.BlockSpec(memory_space=pl.ANY),
        scratch_shapes=[pltpu.VMEM((2, ROW_TILE, dw), u32), pltpu.VMEM((MOE_BLOCK, dw), u32),
                        pltpu.SemaphoreType.DMA((2,)), pltpu.SemaphoreType.DMA((1,))],
    )
    return pl.pallas_call(
        _dispatch_kernel,
        out_shape=jax.ShapeDtypeStruct((slots, dw), u32),
        grid_spec=grid_spec,
        compiler_params=_cparams("arbitrary"),
        name="moe_dispatch",
    )(blk_info, n_used, dest3, hp)


def _ffn_kernel(info_ref, nu_ref, xs_ref, wg_hbm, wu_hbm, wd_hbm, y_ref, wg_sc, wu_sc, wd_sc, sem, *, layer):
    b = pl.program_id(0)
    info = info_ref[b]
    expert = info & ((1 << INFO_FIRST) - 1)
    first = (info >> INFO_FIRST) & 1
    slot = (info >> INFO_SLOT) & 1
    nxt = info >> INFO_NEXT
    live = b < nu_ref[0]

    def weight_copies(e, s):
        return [pltpu.make_async_copy(src.at[layer, e], dst.at[s], sem.at[s, j])
                for j, (src, dst) in enumerate(((wg_hbm, wg_sc), (wu_hbm, wu_sc), (wd_hbm, wd_sc)))]

    @pl.when(b == 0)
    def _():
        for c in weight_copies(expert, 0):
            c.start()

    @pl.when(live & (first == 1) & (nxt != NO_EXPERT))
    def _():
        for c in weight_copies(nxt, 1 - slot):
            c.start()

    @pl.when(live & (first == 1))
    def _():
        for c in weight_copies(expert, slot):
            c.wait()

    def compute(s):
        w = xs_ref[...]
        lo = lax.bitcast_convert_type(w << 16, f32)
        hi = lax.bitcast_convert_type(w & jnp.uint32(0xFFFF0000), f32)
        xb = jnp.concatenate([lo, hi], axis=1).astype(bf16)
        a = jnp.dot(xb, wg_sc[s], preferred_element_type=f32)
        u = jnp.dot(xb, wu_sc[s], preferred_element_type=f32)
        z = (a * jax.nn.sigmoid(a) * u).astype(bf16)
        y_ref[...] = jnp.dot(z, wd_sc[s], preferred_element_type=f32)

    for s in range(2):
        @pl.when(live & (slot == s))
        def _():
            compute(s)

    @pl.when(jnp.logical_not(live))
    def _():
        y_ref[...] = jnp.zeros(y_ref.shape, f32)


def _expert_ffn(xs, blk_expert, n_used, layer, wg, wu, wd):
    slots, dw = xs.shape
    d = 2 * dw
    de = wg.shape[3]
    n_blocks = slots // MOE_BLOCK
    grid_spec = pltpu.PrefetchScalarGridSpec(
        num_scalar_prefetch=2,
        grid=(n_blocks,),
        in_specs=[
            pl.BlockSpec((MOE_BLOCK, dw), lambda b, be, nu: (b, 0)),
            pl.BlockSpec(memory_space=pl.ANY),
            pl.BlockSpec(memory_space=pl.ANY),
            pl.BlockSpec(memory_space=pl.ANY),
        ],
        out_specs=pl.BlockSpec((MOE_BLOCK, d), lambda b, be, nu: (b, 0)),
        scratch_shapes=[pltpu.VMEM((2, d, de), bf16), pltpu.VMEM((2, d, de), bf16), pltpu.VMEM((2, de, d), bf16),
                        pltpu.SemaphoreType.DMA((2, 3))],
    )
    return pl.pallas_call(
        functools.partial(_ffn_kernel, layer=layer),
        out_shape=jax.ShapeDtypeStruct((slots, d), f32),
        grid_spec=grid_spec,
        compiler_params=_cparams("arbitrary"),
        name="expert_ffn",
    )(blk_expert, n_used, xs, wg, wu, wd)


def _fin_kernel(pos_cur, pos_nxt, x1_ref, gate_ref, mod_ref, lng_ref, lnb_ref, y_hbm, o_ref, ybuf, sem):
    i = pl.program_id(0)
    n = pl.num_programs(0)
    tm = x1_ref.shape[0]
    slot = i % 2

    def copy(pos_ref, kk, r, s):
        return pltpu.make_async_copy(
            y_hbm.at[pl.ds(pos_ref[0, kk, r], 1), :], ybuf.at[s, kk, pl.ds(r, 1), :], sem.at[s])

    def start_gather(pos_ref, s):
        for kk in range(2):
            for r in range(tm):
                copy(pos_ref, kk, r, s).start()

    @pl.when(i == 0)
    def _():
        start_gather(pos_cur, 0)

    @pl.when(i + 1 < n)
    def _():
        start_gather(pos_nxt, 1 - slot)

    for kk in range(2):
        for r in range(tm):
            copy(pos_cur, kk, r, slot).wait()
    g2 = mod_ref[0, 5:6, :]
    gate = gate_ref[...]
    f = gate[:, 0:1] * ybuf[slot, 0] + gate[:, 1:2] * ybuf[slot, 1]
    z = DEEPNORM_ALPHA * x1_ref[...] + g2 * f
    mu = jnp.mean(z, axis=-1, keepdims=True)
    zc = z - mu
    var = jnp.mean(zc * zc, axis=-1, keepdims=True)
    o_ref[...] = zc * lax.rsqrt(var + LN_EPS) * lng_ref[...] + lnb_ref[...]


def _combine_norm(x1, gate, pos3, y_sorted, mod_l, ln_g, ln_b, seg_of_tile):
    t, d = x1.shape
    n_tiles = t // ROW_TILE
    return pl.pallas_call(
        _fin_kernel,
        out_shape=jax.ShapeDtypeStruct((t, d), f32),
        grid=(n_tiles,),
        in_specs=[
            pl.BlockSpec((1, 2, ROW_TILE), lambda i: (i, 0, 0), memory_space=pltpu.SMEM),
            pl.BlockSpec((1, 2, ROW_TILE), lambda i: (jnp.minimum(i + 1, n_tiles - 1), 0, 0),
                         memory_space=pltpu.SMEM),
            pl.BlockSpec((ROW_TILE, d), lambda i: (i, 0)),
            pl.BlockSpec((ROW_TILE, GATE_COLS), lambda i: (i, 0)),
            pl.BlockSpec((1, 6, d), lambda i: (seg_of_tile(i), 0, 0)),
            pl.BlockSpec((1, d), lambda i: (0, 0)),
            pl.BlockSpec((1, d), lambda i: (0, 0)),
            pl.BlockSpec(memory_space=pl.ANY),
        ],
        out_specs=pl.BlockSpec((ROW_TILE, d), lambda i: (i, 0)),
        scratch_shapes=[pltpu.VMEM((2, 2, ROW_TILE, d), f32), pltpu.SemaphoreType.DMA((2,))],
        compiler_params=_cparams("arbitrary"),
        name="combine_norm",
    )(pos3, pos3, x1, gate, mod_l, ln_g, ln_b, y_sorted)


def _moe_combine(x1, hp, e_idx, gate, layer, wg_bf, wu_bf, wd_bf, mod_l, ln_g, ln_b, seg_of_tile):
    blk_expert, n_used, dest3, n_blocks = _dispatch_plan(e_idx)
    xs = _dispatch(hp, dest3, blk_expert, n_used)
    y_sorted = _expert_ffn(xs, blk_expert, n_used, layer, wg_bf, wu_bf, wd_bf)
    return _combine_norm(x1, gate, dest3, y_sorted, mod_l, ln_g, ln_b, seg_of_tile)


def _rope_tables(ctx_len, seq_len):
    t = jnp.arange(seq_len)
    row = (t // GRID_W).astype(f32)
    col = (t % GRID_W).astype(f32)
    inv_freq = ROPE_THETA ** (-jnp.arange(ROPE_PAIRS, dtype=f32) / ROPE_PAIRS)
    ang_r = row[:, None] * inv_freq
    ang_c = col[:, None] * inv_freq
    ang = jnp.concatenate([ang_r, ang_r, ang_c, ang_c], axis=-1)
    cos = jnp.cos(ang)
    sin = jnp.sin(ang)
    lane = jnp.arange(HEAD_DIM)
    sign = jnp.where((lane % (2 * ROPE_PAIRS)) < ROPE_PAIRS, -1.0, 1.0).astype(f32)
    cos_t = jnp.concatenate([jnp.ones((ctx_len, HEAD_DIM), f32), cos], axis=0)
    sin_t = jnp.concatenate([jnp.zeros((ctx_len, HEAD_DIM), f32), sin * sign], axis=0)
    return cos_t, sin_t


def kernel(x, c, ctx, c_ctx, w_mod, b_mod, ln_g, ln_b, w_qkv, q_gain, k_gain, w_o, w_pool, pool_scale,
           w_router, router_bias, w_gate, w_up, w_down):
    n_batch, seq_len, d = x.shape
    ctx_len = ctx.shape[1]
    assert w_mod.shape[0] == DEPTH and d == N_HEADS * HEAD_DIM
    assert ctx_len % ROW_TILE == 0 and seq_len % ROW_TILE == 0 and 1 + n_batch <= MOD_ROWS
    rows = _Rows(n_batch, ctx_len, seq_len)

    cond = jnp.concatenate([c_ctx[None, :], c], axis=0)
    mod = _modulation(cond, w_mod, b_mod).reshape(DEPTH, MOD_ROWS, 6, d)

    wr_t = w_router.reshape(d, N_EXPERT_GROUPS, EXPERTS_PER_GROUP).transpose(2, 1, 0).reshape(N_EXPERTS, d)
    wr_hi = wr_t.astype(bf16)
    wr_hl = jnp.concatenate([wr_hi, (wr_t - wr_hi.astype(f32)).astype(bf16)], axis=0)
    rb_p = router_bias.reshape(N_EXPERT_GROUPS, EXPERTS_PER_GROUP).T.reshape(N_EXPERTS, 1)
    cos_t, sin_t = _rope_tables(ctx_len, seq_len)

    ctx2 = ctx.reshape(n_batch * ctx_len, d)
    x2 = x.reshape(n_batch * seq_len, d)
    q, k, v = _qkv_proj(ctx2, x2, mod[0], w_qkv[0].astype(bf16), q_gain[0:1], k_gain[0:1], cos_t, sin_t, rows)
    bound = (Q_SCALE * HEAD_DIM * SCORE_BOUND_SLACK) * jnp.max(jnp.abs(q_gain[0])) * jnp.max(jnp.abs(k_gain[0]))
    attn_args = (bound.reshape(1, 1), q, k, v, w_gate, w_up, w_down)
    o, wg_bf, wu_bf, wd_bf = lax.cond(
        bound <= SCORE_BOUND_MAX,
        lambda a: _attention(*a, rows, True),
        lambda a: _attention(*a, rows, False),
        attn_args)
    x1, hp, e_idx, gate = _attn_post(ctx2, x2, o, w_o[0].astype(bf16), mod[0], ln_g[0, 0:1], ln_b[0, 0:1],
                                     wr_hl, rb_p, rows)
    x_all = _moe_combine(x1, hp, e_idx, gate, 0, wg_bf, wu_bf, wd_bf, mod[0], ln_g[0, 1:2], ln_b[0, 1:2],
                         rows.seg)

    x1, hp, e_idx, gate = _pool_post(x_all, w_pool[0].astype(bf16), pool_scale[0:1], mod[1],
                                     ln_g[1, 0:1], ln_b[1, 0:1], wr_hl, rb_p, rows)
    out = _moe_combine(x1, hp, e_idx, gate, 1, wg_bf, wu_bf, wd_bf, mod[1], ln_g[1, 1:2], ln_b[1, 1:2],
                       lambda i: 1 + i // rows.n_seq_tiles)
    return out.reshape(n_batch, seq_len, d)
```

```python
import functools

import jax
import jax.numpy as jnp
from jax import lax
from jax.experimental import pallas as pl
from jax.experimental.pallas import tpu as pltpu

N_HEADS = 16
N_KV_HEADS = 4
HEAD_DIM = 128
Q_PER_KV = N_HEADS // N_KV_HEADS
GRID_W = 64
ROPE_THETA = 10000.0
ROPE_PAIRS = HEAD_DIM // 4
ATTN_SCALE = HEAD_DIM ** -0.5
Q_SCALE = ATTN_SCALE * 1.4426950408889634
POOL_WINDOWS = (2, 4, 8, 16)
POOL_HALO = 8
N_EXPERTS = 32
N_EXPERT_GROUPS = 8
EXPERTS_PER_GROUP = N_EXPERTS // N_EXPERT_GROUPS
DEPTH = 2
DEEPNORM_ALPHA = (2 * DEPTH) ** 0.25
LN_EPS = 1e-6
RMS_EPS = 1e-6

LANES = 128
SUBLANES = 8
ROW_TILE = 256
ATTN_TK = 2048
ATTN_ROWS = 256
SCORE_BOUND_SLACK = 1.02
SCORE_BOUND_MAX = 56.0
POST_ROWS = 128
MOE_BLOCK = 256
PLAN_TILES_PER_STEP = (11, 8, 6, 4, 3, 2, 1)
NO_EXPERT = N_EXPERTS
INFO_FIRST, INFO_SLOT, INFO_NEXT = 8, 9, 10
GATE_COLS = 8
MOD_ROWS = 8
MOD_TN = 1024
VMEM_LIMIT = 56 * 1024 * 1024

f32 = jnp.float32
bf16 = jnp.bfloat16
u32 = jnp.uint32
i32 = jnp.int32


def _cparams(*sem):
    return pltpu.CompilerParams(dimension_semantics=sem, vmem_limit_bytes=VMEM_LIMIT)


def _nt_dot(a, b):
    return lax.dot_general(a, b, (((1,), (1,)), ((), ())), preferred_element_type=f32)


def _mod_kernel(c_ref, w_ref, b_ref, o_ref, s_sc, *, n_rows):
    d = w_ref.shape[1]
    tn = w_ref.shape[2]
    nj = tn // LANES

    @pl.when((pl.program_id(0) == 0) & (pl.program_id(1) == 0))
    def _():
        c = c_ref[...]
        s_sc[...] = c * jax.nn.sigmoid(c)

    def body(k, accs):
        r0 = pl.multiple_of(k * SUBLANES, SUBLANES)
        w8 = w_ref[0, pl.ds(r0, SUBLANES), :]
        out = []
        for r in range(n_rows):
            s8 = s_sc[r, pl.ds(r0, SUBLANES), :]
            for j in range(nj):
                out.append(accs[r * nj + j] + s8 * w8[:, j * LANES:(j + 1) * LANES])
        return tuple(out)

    init = tuple(jnp.zeros((SUBLANES, LANES), f32) for _ in range(n_rows * nj))
    accs = lax.fori_loop(0, d // SUBLANES, body, init, unroll=4)
    rows = []
    for r in range(n_rows):
        row = jnp.concatenate(
            [jnp.sum(accs[r * nj + j], axis=0, keepdims=True) for j in range(nj)], axis=1)
        rows.append(row + b_ref[0])
    rows.append(jnp.zeros((MOD_ROWS - n_rows, tn), f32))
    o_ref[0] = jnp.concatenate(rows, axis=0)


def _modulation(cond, w_mod, b_mod):
    n_rows, d = cond.shape
    depth, _, n6 = w_mod.shape
    c_b = jnp.broadcast_to(cond[:, :, None], (n_rows, d, LANES))
    return pl.pallas_call(
        functools.partial(_mod_kernel, n_rows=n_rows),
        out_shape=jax.ShapeDtypeStruct((depth, MOD_ROWS, n6), f32),
        grid=(depth, n6 // MOD_TN),
        in_specs=[
            pl.BlockSpec((n_rows, d, LANES), lambda l, j: (0, 0, 0)),
            pl.BlockSpec((1, d, MOD_TN), lambda l, j: (l, 0, j)),
            pl.BlockSpec((1, 1, MOD_TN), lambda l, j: (l, 0, j)),
        ],
        out_specs=pl.BlockSpec((1, MOD_ROWS, MOD_TN), lambda l, j: (l, 0, j)),
        scratch_shapes=[pltpu.VMEM((n_rows, d, LANES), f32)],
        compiler_params=_cparams("arbitrary", "arbitrary"),
        name="adaln_mod",
    )(c_b, w_mod, b_mod.reshape(depth, 1, n6))


class _Rows:
    def __init__(self, n_batch, ctx_len, seq_len):
        self.n_batch = n_batch
        self.ctx_len = ctx_len
        self.seq_len = seq_len
        self.n_ctx_tiles = ctx_len // ROW_TILE
        self.n_seq_tiles = seq_len // ROW_TILE
        self.tiles_per_batch = self.n_ctx_tiles + self.n_seq_tiles
        self.n_tiles = n_batch * self.tiles_per_batch
        self.t_all = self.n_tiles * ROW_TILE

    def is_ctx(self, i):
        return i % self.tiles_per_batch < self.n_ctx_tiles

    def seg(self, i):
        return jnp.where(self.is_ctx(i), 0, 1 + i // self.tiles_per_batch)

    def ctx_tile(self, i):
        r = i % self.tiles_per_batch
        return (i // self.tiles_per_batch) * self.n_ctx_tiles + jnp.minimum(r, self.n_ctx_tiles - 1)

    def lat_tile(self, i):
        r = i % self.tiles_per_batch
        return (i // self.tiles_per_batch) * self.n_seq_tiles + jnp.maximum(r - self.n_ctx_tiles, 0)


def _qkv_kernel(xc_ref, xl_ref, mod_ref, w_ref, qg_ref, kg_ref, cos_ref, sin_ref, q_ref, k_ref, v_ref, *, rows):
    tm, d = xl_ref.shape
    kv_dim = k_ref.shape[1]
    x = jnp.where(rows.is_ctx(pl.program_id(0)), xc_ref[...], xl_ref[...])
    sh1 = mod_ref[0, 0:1, :]
    sc1 = mod_ref[0, 1:2, :]
    u = (x * (1.0 + sc1) + sh1).astype(bf16)
    qkv = jnp.dot(u, w_ref[...], preferred_element_type=f32)
    cos = cos_ref[...]
    sin = sin_ref[...]
    lane = lax.broadcasted_iota(i32, (tm, HEAD_DIM), 1)
    first = (lane % (2 * ROPE_PAIRS)) < ROPE_PAIRS

    def norm_rope(xh, gain, scale):
        ms = jnp.mean(xh * xh, axis=-1, keepdims=True)
        xn = xh * lax.rsqrt(ms + RMS_EPS) * gain
        rot = jnp.where(first, pltpu.roll(xn, HEAD_DIM - ROPE_PAIRS, 1), pltpu.roll(xn, ROPE_PAIRS, 1))
        y = xn * cos + rot * sin
        return y * scale if scale != 1.0 else y

    qg = qg_ref[...]
    kg = kg_ref[...]
    for h in range(N_HEADS):
        sl = slice(h * HEAD_DIM, (h + 1) * HEAD_DIM)
        q_ref[:, sl] = norm_rope(qkv[:, sl], qg, Q_SCALE).astype(bf16)
    ones = jnp.ones((tm, HEAD_DIM), bf16)
    for h in range(N_KV_HEADS):
        sl = slice(h * HEAD_DIM, (h + 1) * HEAD_DIM)
        k_ref[:, sl] = norm_rope(qkv[:, d + h * HEAD_DIM:d + (h + 1) * HEAD_DIM], kg, 1.0).astype(bf16)
        v0 = d + kv_dim + h * HEAD_DIM
        v_ref[:, 2 * h * HEAD_DIM:(2 * h + 1) * HEAD_DIM] = qkv[:, v0:v0 + HEAD_DIM].astype(bf16)
        v_ref[:, (2 * h + 1) * HEAD_DIM:(2 * h + 2) * HEAD_DIM] = ones


def _qkv_proj(ctx2, x2, mod_l, w_qkv_bf, q_gain, k_gain, cos_t, sin_t, rows):
    d = x2.shape[1]
    kv_dim = N_KV_HEADS * HEAD_DIM
    tpb = rows.tiles_per_batch
    return pl.pallas_call(
        functools.partial(_qkv_kernel, rows=rows),
        out_shape=(jax.ShapeDtypeStruct((rows.t_all, d), bf16),
                   jax.ShapeDtypeStruct((rows.t_all, kv_dim), bf16),
                   jax.ShapeDtypeStruct((rows.t_all, 2 * kv_dim), bf16)),
        grid=(rows.n_tiles,),
        in_specs=[
            pl.BlockSpec((ROW_TILE, d), lambda i: (rows.ctx_tile(i), 0)),
            pl.BlockSpec((ROW_TILE, d), lambda i: (rows.lat_tile(i), 0)),
            pl.BlockSpec((1, 6, d), lambda i: (rows.seg(i), 0, 0)),
            pl.BlockSpec((d, d + 2 * kv_dim), lambda i: (0, 0)),
            pl.BlockSpec((1, HEAD_DIM), lambda i: (0, 0)),
            pl.BlockSpec((1, HEAD_DIM), lambda i: (0, 0)),
            pl.BlockSpec((ROW_TILE, HEAD_DIM), lambda i: (i % tpb, 0)),
            pl.BlockSpec((ROW_TILE, HEAD_DIM), lambda i: (i % tpb, 0)),
        ],
        out_specs=(pl.BlockSpec((ROW_TILE, d), lambda i: (i, 0)),
                   pl.BlockSpec((ROW_TILE, kv_dim), lambda i: (i, 0)),
                   pl.BlockSpec((ROW_TILE, 2 * kv_dim), lambda i: (i, 0))),
        compiler_params=_cparams("arbitrary"),
        name="qkv_rope",
    )(ctx2, x2, mod_l, w_qkv_bf, q_gain, k_gain, cos_t, sin_t)


def _attn_kernel(bound_ref, q_ref, k_ref, v_ref, wg_ref, wu_ref, wd_ref, o_ref, wgo_ref, wuo_ref, wdo_ref,
                 *, ctx_len, tk, rs, n_ctx_tiles, bounded):
    tq = q_ref.shape[0]
    n_lat_chunks = (k_ref.shape[0] - ctx_len) // tk
    qi = pl.program_id(2)
    subs = [(h, r0) for h in range(Q_PER_KV) for r0 in range(0, tq, rs)]

    def attend(chunks):
        states = [None] * len(subs)
        for ci, (off, size) in enumerate(chunks):
            for src, dst in ((wg_ref, wgo_ref), (wu_ref, wuo_ref), (wd_ref, wdo_ref)):
                n = src.shape[0] // len(chunks)
                dst[ci * n:(ci + 1) * n, :] = src[ci * n:(ci + 1) * n, :].astype(bf16)
            kc = k_ref[off:off + size, :]
            vc = v_ref[off:off + size, :]
            for i, (h, r0) in enumerate(subs):
                s = _nt_dot(q_ref[r0:r0 + rs, h * HEAD_DIM:(h + 1) * HEAD_DIM], kc)
                if bounded:
                    pv = jnp.dot(jnp.exp2(s - bound_ref[0, 0]).astype(bf16), vc, preferred_element_type=f32)
                    states[i] = pv if states[i] is None else states[i] + pv
                    continue
                mx = s[:, 0:LANES]
                for j in range(1, size // LANES):
                    mx = jnp.maximum(mx, s[:, j * LANES:(j + 1) * LANES])
                m_cur = jnp.max(mx, axis=-1, keepdims=True)
                if states[i] is None:
                    m_new = jnp.broadcast_to(m_cur, (rs, LANES))
                else:
                    m_prev, acc_prev = states[i]
                    m_new = jnp.maximum(m_prev, m_cur)
                p = jnp.concatenate(
                    [jnp.exp2(s[:, j * LANES:(j + 1) * LANES] - m_new) for j in range(size // LANES)],
                    axis=1).astype(bf16)
                acc = jnp.dot(p, vc, preferred_element_type=f32)
                if states[i] is not None:
                    alpha = jnp.exp2(m_prev - m_new)
                    acc = acc_prev * jnp.concatenate([alpha, alpha], axis=1) + acc
                states[i] = (m_new, acc)
        for i, (h, r0) in enumerate(subs):
            acc = states[i] if bounded else states[i][1]
            o_ref[r0:r0 + rs, h * HEAD_DIM:(h + 1) * HEAD_DIM] = (
                acc[:, :HEAD_DIM] / acc[:, HEAD_DIM:]).astype(bf16)

    @pl.when(qi < n_ctx_tiles)
    def _():
        attend([(0, ctx_len)])

    @pl.when(qi >= n_ctx_tiles)
    def _():
        attend([(0, ctx_len + tk)] + [(ctx_len + c * tk, tk) for c in range(1, n_lat_chunks)])


def _attention(bound, q, k, v, w_gate, w_up, w_down, rows, bounded):
    t_all, d = q.shape
    tpb = rows.tiles_per_batch
    rows_per_batch = t_all // rows.n_batch
    gw = Q_PER_KV * HEAD_DIM
    tk = min(ATTN_TK, rows.seq_len)
    assert rows.seq_len % tk == 0

    n_steps = rows.n_batch * N_KV_HEADS * tpb
    wg2 = w_gate.reshape(-1, w_gate.shape[-1])
    wu2 = w_up.reshape(-1, w_up.shape[-1])
    wd2 = w_down.reshape(-1, w_down.shape[-1])
    n_slabs = 1 << (n_steps.bit_length() - 1)
    slab_g = wg2.shape[0] // n_slabs
    slab_d = wd2.shape[0] // n_slabs
    assert wg2.shape[0] == n_slabs * slab_g and wd2.shape[0] == n_slabs * slab_d and wu2.shape == wg2.shape
    assert slab_g % SUBLANES == 0 and slab_d % SUBLANES == 0

    def slab(b, g, i):
        return (jnp.minimum((b * N_KV_HEADS + g) * tpb + i, n_slabs - 1), 0)

    kern = functools.partial(_attn_kernel, ctx_len=rows.ctx_len, tk=tk, rs=ATTN_ROWS,
                             n_ctx_tiles=rows.n_ctx_tiles, bounded=bounded)
    o, wg_bf, wu_bf, wd_bf = pl.pallas_call(
        kern,
        out_shape=(jax.ShapeDtypeStruct((t_all, d), bf16),
                   jax.ShapeDtypeStruct(wg2.shape, bf16),
                   jax.ShapeDtypeStruct(wu2.shape, bf16),
                   jax.ShapeDtypeStruct(wd2.shape, bf16)),
        grid=(rows.n_batch, N_KV_HEADS, tpb),
        in_specs=[
            pl.BlockSpec((1, 1), lambda b, g, i: (0, 0), memory_space=pltpu.SMEM),
            pl.BlockSpec((ROW_TILE, gw), lambda b, g, i: (b * tpb + i, g)),
            pl.BlockSpec((rows_per_batch, HEAD_DIM), lambda b, g, i: (b, g)),
            pl.BlockSpec((rows_per_batch, 2 * HEAD_DIM), lambda b, g, i: (b, g)),
            pl.BlockSpec((slab_g, wg2.shape[1]), slab),
            pl.BlockSpec((slab_g, wu2.shape[1]), slab),
            pl.BlockSpec((slab_d, wd2.shape[1]), slab),
        ],
        out_specs=(pl.BlockSpec((ROW_TILE, gw), lambda b, g, i: (b * tpb + i, g)),
                   pl.BlockSpec((slab_g, wg2.shape[1]), slab),
                   pl.BlockSpec((slab_g, wu2.shape[1]), slab),
                   pl.BlockSpec((slab_d, wd2.shape[1]), slab)),
        compiler_params=_cparams("arbitrary", "arbitrary", "arbitrary"),
        name="gqa_attention_bounded" if bounded else "gqa_attention_online",
    )(bound, q, k, v, wg2, wu2, wd2)
    return o, wg_bf.reshape(w_gate.shape), wu_bf.reshape(w_up.shape), wd_bf.reshape(w_down.shape)


def _post_epilogue(x, y, r0, mod_ref, lng_ref, lnb_ref, wr_ref, rb_ref, x1_ref, hp_ref, e_ref, g_ref):
    tm, d = x.shape
    rsl = slice(r0, r0 + tm)
    g1 = mod_ref[0, 2:3, :]
    sh2 = mod_ref[0, 3:4, :]
    sc2 = mod_ref[0, 4:5, :]
    z = DEEPNORM_ALPHA * x + g1 * y
    mu = jnp.mean(z, axis=-1, keepdims=True)
    zc = z - mu
    var = jnp.mean(zc * zc, axis=-1, keepdims=True)
    x1 = zc * lax.rsqrt(var + LN_EPS) * lng_ref[...] + lnb_ref[...]
    h = x1 * (1.0 + sc2) + sh2
    x1_ref[rsl, :] = x1

    h_hi = h.astype(bf16)
    h_hi32 = h_hi.astype(f32)
    bits = lax.bitcast_convert_type(h_hi32, u32)
    hp_ref[rsl, :] = (bits[:, :d // 2] >> 16) | bits[:, d // 2:]

    h_lo = (h - h_hi32).astype(bf16)
    wr = wr_ref[...]
    a2 = _nt_dot(wr, h_hi)
    lt = a2[:N_EXPERTS] + a2[N_EXPERTS:] + _nt_dot(wr[:N_EXPERTS], h_lo)
    aff = jax.nn.sigmoid(lt)
    sel = aff + rb_ref[...]
    ng = N_EXPERT_GROUPS
    a = [aff[j * ng:(j + 1) * ng] for j in range(EXPERTS_PER_GROUP)]
    s = [sel[j * ng:(j + 1) * ng] for j in range(EXPERTS_PER_GROUP)]
    gs = None
    for i in range(EXPERTS_PER_GROUP):
        for j in range(i + 1, EXPERTS_PER_GROUP):
            pij = s[i] + s[j]
            gs = pij if gs is None else jnp.maximum(gs, pij)
    gio = lax.broadcasted_iota(i32, gs.shape, 0)
    gmax = jnp.max(gs, axis=0, keepdims=True)
    g_idx = jnp.min(jnp.where(gs == gmax, gio, ng), axis=0, keepdims=True)
    onehot = gio == g_idx
    sin_g = [jnp.sum(jnp.where(onehot, s[j], 0.0), axis=0, keepdims=True) for j in range(EXPERTS_PER_GROUP)]
    aff_g = [jnp.sum(jnp.where(onehot, a[j], 0.0), axis=0, keepdims=True) for j in range(EXPERTS_PER_GROUP)]
    loc = [jnp.zeros_like(g_idx), jnp.zeros_like(g_idx)]
    gat = [jnp.zeros_like(gmax), jnp.zeros_like(gmax)]
    for j in range(EXPERTS_PER_GROUP):
        rank = jnp.zeros_like(g_idx)
        for i in range(EXPERTS_PER_GROUP):
            if i == j:
                continue
            ahead = (sin_g[i] >= sin_g[j]) if i < j else (sin_g[i] > sin_g[j])
            rank = rank + ahead.astype(i32)
        for kk in range(2):
            hit = rank == kk
            loc[kk] = jnp.where(hit, j, loc[kk])
            gat[kk] = jnp.where(hit, aff_g[j], gat[kk])
    den = gat[0] + gat[1]
    e_ref[0, :, rsl] = jnp.concatenate(
        [g_idx * EXPERTS_PER_GROUP + loc[0], g_idx * EXPERTS_PER_GROUP + loc[1]], axis=0)
    g8 = jnp.concatenate([gat[0] / den, gat[1] / den, jnp.zeros((GATE_COLS - 2, tm), f32)], axis=0)
    g_ref[rsl, :] = g8.T


def _attn_post_kernel(xc_ref, xl_ref, o_ref, wo_ref, mod_ref, lng_ref, lnb_ref, wr_ref, rb_ref,
                      x1_ref, hp_ref, e_ref, g_ref, ya_sc, yb_sc, *, rows):
    s = pl.program_id(0)
    tile = jnp.maximum(s - 1, 0)

    @pl.when(s == 0)
    def _():
        yb_sc[...] = jnp.zeros(yb_sc.shape, f32)

    def body(y_write, y_read):
        y_write[...] = jnp.dot(o_ref[...], wo_ref[...], preferred_element_type=f32)
        x = jnp.where(rows.is_ctx(tile), xc_ref[...], xl_ref[...])
        _post_epilogue(x, y_read[...], 0, mod_ref, lng_ref, lnb_ref, wr_ref, rb_ref,
                       x1_ref, hp_ref, e_ref, g_ref)

    @pl.when(s % 2 == 0)
    def _():
        body(ya_sc, yb_sc)

    @pl.when(s % 2 == 1)
    def _():
        body(yb_sc, ya_sc)


def _post_out(n_rows, d, tile_of):
    shapes = (jax.ShapeDtypeStruct((n_rows, d), f32),
              jax.ShapeDtypeStruct((n_rows, d // 2), u32),
              jax.ShapeDtypeStruct((n_rows // ROW_TILE, 2, ROW_TILE), i32),
              jax.ShapeDtypeStruct((n_rows, GATE_COLS), f32))
    specs = (pl.BlockSpec((ROW_TILE, d), lambda i: (tile_of(i), 0)),
             pl.BlockSpec((ROW_TILE, d // 2), lambda i: (tile_of(i), 0)),
             pl.BlockSpec((1, 2, ROW_TILE), lambda i: (tile_of(i), 0, 0)),
             pl.BlockSpec((ROW_TILE, GATE_COLS), lambda i: (tile_of(i), 0)))
    return shapes, specs


def _router_specs(d):
    return [pl.BlockSpec((2 * N_EXPERTS, d), lambda i: (0, 0)),
            pl.BlockSpec((N_EXPERTS, 1), lambda i: (0, 0))]


def _attn_post(ctx2, x2, o, w_o_bf, mod_l, ln_g, ln_b, wr_hl, rb_p, rows):
    d = x2.shape[1]
    n = rows.n_tiles

    def prev(i):
        return jnp.maximum(i - 1, 0)

    shapes, specs = _post_out(rows.t_all, d, prev)
    return pl.pallas_call(
        functools.partial(_attn_post_kernel, rows=rows),
        out_shape=shapes,
        grid=(n + 1,),
        in_specs=[
            pl.BlockSpec((ROW_TILE, d), lambda i: (rows.ctx_tile(prev(i)), 0)),
            pl.BlockSpec((ROW_TILE, d), lambda i: (rows.lat_tile(prev(i)), 0)),
            pl.BlockSpec((ROW_TILE, d), lambda i: (jnp.minimum(i, n - 1), 0)),
            pl.BlockSpec((d, d), lambda i: (0, 0)),
            pl.BlockSpec((1, 6, d), lambda i: (rows.seg(prev(i)), 0, 0)),
            pl.BlockSpec((1, d), lambda i: (0, 0)),
            pl.BlockSpec((1, d), lambda i: (0, 0)),
        ] + _router_specs(d),
        out_specs=specs,
        scratch_shapes=[pltpu.VMEM((ROW_TILE, d), f32), pltpu.VMEM((ROW_TILE, d), f32)],
        compiler_params=_cparams("arbitrary"),
        name="attn_out_norm_route",
    )(ctx2, x2, o, w_o_bf, mod_l, ln_g, ln_b, wr_hl, rb_p)


def _pool_post_kernel(x_ref, xp_ref, xn_ref, wp_ref, ps_ref, mod_ref, lng_ref, lnb_ref, wr_ref, rb_ref,
                      x1_ref, hp_ref, e_ref, g_ref, *, seq_len, tiles_per_seq):
    tm, d = x_ref.shape
    gd = d // len(POOL_WINDOWS)
    r = pl.program_id(0) % tiles_per_seq
    sh1 = mod_ref[0, 0:1, :]
    sc1 = mod_ref[0, 1:2, :]
    x = x_ref[...]
    ext = jnp.concatenate([xp_ref[...], x, xn_ref[...]], axis=0)
    n_ext = tm + 2 * POOL_HALO
    pos = r * tm - POOL_HALO + lax.broadcasted_iota(i32, (n_ext, 1), 0)
    inside = (pos >= 0) & (pos < seq_len)
    u_ext = jnp.where(inside, ext * (1.0 + sc1) + sh1, 0.0)
    t = pos[POOL_HALO:POOL_HALO + tm]
    dlts = []
    for g, w in enumerate(POOL_WINDOWS):
        e = u_ext[:, g * gd:(g + 1) * gd]
        p = e + pltpu.roll(e, 1, 0)
        step = 1
        while 2 * step < w:
            p = pltpu.roll(p, step, 0) + pltpu.roll(p, n_ext - step, 0)
            step *= 2
        cnt = (jnp.clip(t + w // 2, 0, seq_len) - jnp.clip(t - w // 2, 0, seq_len)).astype(f32)
        mean = p[POOL_HALO:POOL_HALO + tm] / cnt
        dlts.append((mean - e[POOL_HALO:POOL_HALO + tm]).astype(bf16))
    for r0 in range(0, tm, POST_ROWS):
        rsl = slice(r0, r0 + POST_ROWS)
        y = jnp.concatenate(
            [jnp.dot(dlts[g][rsl, :], wp_ref[g], preferred_element_type=f32) for g in range(len(POOL_WINDOWS))],
            axis=-1) * ps_ref[...]
        _post_epilogue(x[rsl, :], y, r0, mod_ref, lng_ref, lnb_ref, wr_ref, rb_ref, x1_ref, hp_ref, e_ref, g_ref)


def _pool_post(x_all, w_pool_bf, pool_scale, mod_l, ln_g, ln_b, wr_hl, rb_p, rows):
    t_all, d = x_all.shape
    tiles_per_seq = rows.n_seq_tiles
    n_rows = rows.n_batch * rows.seq_len
    halo_per_tile = ROW_TILE // POOL_HALO
    last_halo = t_all // POOL_HALO - 1
    ng = len(POOL_WINDOWS)

    def src_tile(i):
        return (i // tiles_per_seq) * rows.tiles_per_batch + rows.n_ctx_tiles + i % tiles_per_seq

    shapes, specs = _post_out(n_rows, d, lambda i: i)
    kern = functools.partial(_pool_post_kernel, seq_len=rows.seq_len, tiles_per_seq=tiles_per_seq)
    return pl.pallas_call(
        kern,
        out_shape=shapes,
        grid=(n_rows // ROW_TILE,),
        in_specs=[
            pl.BlockSpec((ROW_TILE, d), lambda i: (src_tile(i), 0)),
            pl.BlockSpec((POOL_HALO, d), lambda i: (jnp.maximum(src_tile(i) * halo_per_tile - 1, 0), 0)),
            pl.BlockSpec((POOL_HALO, d), lambda i: (jnp.minimum((src_tile(i) + 1) * halo_per_tile, last_halo), 0)),
            pl.BlockSpec((ng, d // ng, d // ng), lambda i: (0, 0, 0)),
            pl.BlockSpec((1, d), lambda i: (0, 0)),
            pl.BlockSpec((1, 6, d), lambda i: (1 + i // tiles_per_seq, 0, 0)),
            pl.BlockSpec((1, d), lambda i: (0, 0)),
            pl.BlockSpec((1, d), lambda i: (0, 0)),
        ] + _router_specs(d),
        out_specs=specs,
        compiler_params=_cparams("arbitrary"),
        name="pool_norm_route",
    )(x_all, x_all, x_all, w_pool_bf, pool_scale, mod_l, ln_g, ln_b, wr_hl, rb_p)


def _plan_kernel(e_ref, dest_ref, be_ref, nu_ref, carry_sc, start_sc):
    phase = pl.program_id(0)
    i = pl.program_id(1)
    tm = e_ref.shape[2]
    sub = lax.broadcasted_iota(i32, (N_EXPERTS, tm), 0)
    shift = MOE_BLOCK.bit_length() - 1

    @pl.when((phase == 0) & (i == 0))
    def _():
        carry_sc[...] = jnp.zeros(carry_sc.shape, f32)

    @pl.when(phase == 0)
    def _():
        carry = carry_sc[...]
        for j in range(e_ref.shape[0]):
            for kk in range(2):
                hit = sub == e_ref[j, kk:kk + 1, :]
                carry = carry + jnp.sum(jnp.where(hit, 1.0, 0.0), axis=1, keepdims=True)
        carry_sc[...] = carry

    @pl.when((phase == 1) & (i == 0))
    def _():
        counts = carry_sc[...].astype(i32)
        padded = ((counts + (MOE_BLOCK - 1)) >> shift) << shift
        esub = lax.broadcasted_iota(i32, padded.shape, 0)
        used = jnp.where(padded > 0, 1, 0)
        start = jnp.zeros(padded.shape, i32)
        order = jnp.zeros(padded.shape, i32)
        for e in range(N_EXPERTS - 1):
            start = start + jnp.where(esub > e, padded[e:e + 1, :], 0)
            order = order + jnp.where(esub > e, used[e:e + 1, :], 0)
        ends = start + padded
        nxt = jnp.full(padded.shape, NO_EXPERT, i32)
        cur = jnp.full((1, LANES), NO_EXPERT, i32)
        for e in range(N_EXPERTS - 1, -1, -1):
            nxt = jnp.where(esub == e, cur, nxt)
            cur = jnp.where(used[e:e + 1, :] > 0, e, cur)
        nb_pad = be_ref.shape[1]
        rep = nb_pad // LANES
        lane = lax.broadcasted_iota(i32, (1, nb_pad), 1)
        be = jnp.zeros((1, nb_pad), i32)
        for e in range(N_EXPERTS):
            be = be + jnp.where(jnp.concatenate([ends[e:e + 1, :]] * rep, axis=1) <= lane * MOE_BLOCK, 1, 0)
        be = jnp.minimum(be, N_EXPERTS - 1)
        par_b = jnp.zeros((1, nb_pad), i32)
        nxt_b = jnp.zeros((1, nb_pad), i32)
        for e in range(N_EXPERTS):
            hit = be == e
            par_b = jnp.where(hit, jnp.concatenate([order[e:e + 1, :]] * rep, axis=1) & 1, par_b)
            nxt_b = jnp.where(hit, jnp.concatenate([nxt[e:e + 1, :]] * rep, axis=1), nxt_b)
        first_b = jnp.where((be != pltpu.roll(be, 1, 1)) | (lane == 0), 1, 0)
        be_ref[...] = be | (first_b << INFO_FIRST) | (par_b << INFO_SLOT) | (nxt_b << INFO_NEXT)
        nu_ref[...] = ends[N_EXPERTS - 1:N_EXPERTS, :] >> shift
        start_sc[...] = start.astype(f32)
        carry_sc[...] = jnp.zeros(carry_sc.shape, f32)

    @pl.when(phase == 1)
    def _():
        before = lax.broadcasted_iota(i32, (tm, tm), 0) < lax.broadcasted_iota(i32, (tm, tm), 1)
        tri = jnp.where(before, 1.0, 0.0).astype(bf16)
        carry = carry_sc[...]
        start = start_sc[...]
        for j in range(e_ref.shape[0]):
            out = []
            for kk in range(2):
                hit = sub == e_ref[j, kk:kk + 1, :]
                onehot = jnp.where(hit, 1.0, 0.0)
                cum = jnp.dot(onehot.astype(bf16), tri, preferred_element_type=f32)
                base = (carry + start)[:, 0:1]
                out.append(jnp.sum(jnp.where(hit, cum + base, 0.0), axis=0, keepdims=True))
                carry = carry + jnp.sum(onehot, axis=1, keepdims=True)
            dest_ref[j] = jnp.concatenate(out, axis=0).astype(i32)
        carry_sc[...] = carry


def _dispatch_plan(e3):
    n_tiles, _, tm = e3.shape
    n_blocks = (2 * n_tiles * tm + N_EXPERTS * (MOE_BLOCK - 1) + MOE_BLOCK - 1) // MOE_BLOCK
    nb_pad = (n_blocks + LANES - 1) // LANES * LANES
    tps = max(c for c in PLAN_TILES_PER_STEP if n_tiles % c == 0)
    dest3, be, nu = pl.pallas_call(
        _plan_kernel,
        out_shape=(jax.ShapeDtypeStruct(e3.shape, i32),
                   jax.ShapeDtypeStruct((1, nb_pad), i32),
                   jax.ShapeDtypeStruct((1, LANES), i32)),
        grid=(2, n_tiles // tps),
        in_specs=[pl.BlockSpec((tps, 2, tm), lambda p, i: (i, 0, 0))],
        out_specs=(pl.BlockSpec((tps, 2, tm), lambda p, i: (p * i, 0, 0)),
                   pl.BlockSpec((1, nb_pad), lambda p, i: (0, 0)),
                   pl.BlockSpec((1, LANES), lambda p, i: (0, 0))),
        scratch_shapes=[pltpu.VMEM((N_EXPERTS, LANES), f32), pltpu.VMEM((N_EXPERTS, LANES), f32)],
        compiler_params=_cparams("arbitrary", "arbitrary"),
        name="route_plan",
    )(e3)
    return be[0, :n_blocks], nu[0, :1], dest3, n_blocks


def _dispatch_kernel(info_ref, nu_ref, dest_ref, hp_ref, xs_hbm, buf, zbuf, sem, zsem):
    i = pl.program_id(0)
    n = pl.num_programs(0)
    tm = hp_ref.shape[0]
    n_blocks = xs_hbm.shape[0] // MOE_BLOCK

    @pl.when(i == 0)
    def _():
        zbuf[...] = jnp.zeros(zbuf.shape, u32)

        def fill(b):
            return pltpu.make_async_copy(
                zbuf, xs_hbm.at[pl.ds(pl.multiple_of(b * MOE_BLOCK, MOE_BLOCK), MOE_BLOCK), :], zsem.at[0])

        def needs_fill(b):
            next_first = (info_ref[jnp.minimum(b + 1, n_blocks - 1)] >> INFO_FIRST) & 1
            return (b >= nu_ref[0] - 1) | (next_first == 1)

        def start(b, c):
            @pl.when(needs_fill(b))
            def _():
                fill(b).start()
            return c

        def wait(b, c):
            @pl.when(needs_fill(b))
            def _():
                fill(b).wait()
            return c

        lax.fori_loop(0, n_blocks, start, 0)
        lax.fori_loop(0, n_blocks, wait, 0)

    def issue(s):
        buf[s] = hp_ref[...]
        for kk in range(2):
            for r in range(tm):
                pltpu.make_async_copy(
                    buf.at[s, pl.ds(r, 1), :], xs_hbm.at[pl.ds(dest_ref[0, kk, r], 1), :], sem.at[s]).start()

    def drain(s):
        for _ in range(2 * tm):
            pltpu.make_async_copy(buf.at[s, pl.ds(0, 1), :], xs_hbm.at[pl.ds(0, 1), :], sem.at[s]).wait()

    for s in range(2):
        @pl.when(i % 2 == s)
        def _():
            issue(s)

            @pl.when(i > 0)
            def _():
                drain(1 - s)

            @pl.when(i == n - 1)
            def _():
                drain(s)


def _dispatch(hp, dest3, blk_info, n_used):
    t, dw = hp.shape
    n_tiles = t // ROW_TILE
    slots = blk_info.shape[0] * MOE_BLOCK
    grid_spec = pltpu.PrefetchScalarGridSpec(
        num_scalar_prefetch=2,
        grid=(n_tiles,),
        in_specs=[
            pl.BlockSpec((1, 2, ROW_TILE), lambda i, info, nu: (i, 0, 0), memory_space=pltpu.SMEM),
            pl.BlockSpec((ROW_TILE, dw), lambda i, info, nu: (i, 0)),
        ],
        out_specs=pl.BlockSpec(memory_space=pl.ANY),
        scratch_shapes=[pltpu.VMEM((2, ROW_TILE, dw), u32), pltpu.VMEM((MOE_BLOCK, dw), u32),
                        pltpu.SemaphoreType.DMA((2,)), pltpu.SemaphoreType.DMA((1,))],
    )
    return pl.pallas_call(
        _dispatch_kernel,
        out_shape=jax.ShapeDtypeStruct((slots, dw), u32),
        grid_spec=grid_spec,
        compiler_params=_cparams("arbitrary"),
        name="moe_dispatch",
    )(blk_info, n_used, dest3, hp)


def _ffn_kernel(info_ref, nu_ref, xs_ref, wg_hbm, wu_hbm, wd_hbm, y_ref, wg_sc, wu_sc, wd_sc, sem, *, layer):
    b = pl.program_id(0)
    info = info_ref[b]
    expert = info & ((1 << INFO_FIRST) - 1)
    first = (info >> INFO_FIRST) & 1
    slot = (info >> INFO_SLOT) & 1
    nxt = info >> INFO_NEXT
    live = b < nu_ref[0]

    def weight_copies(e, s):
        return [pltpu.make_async_copy(src.at[layer, e], dst.at[s], sem.at[s, j])
                for j, (src, dst) in enumerate(((wg_hbm, wg_sc), (wu_hbm, wu_sc), (wd_hbm, wd_sc)))]

    @pl.when(b == 0)
    def _():
        for c in weight_copies(expert, 0):
            c.start()

    @pl.when(live & (first == 1) & (nxt != NO_EXPERT))
    def _():
        for c in weight_copies(nxt, 1 - slot):
            c.start()

    @pl.when(live & (first == 1))
    def _():
        for c in weight_copies(expert, slot):
            c.wait()

    def compute(s):
        w = xs_ref[...]
        lo = lax.bitcast_convert_type(w << 16, f32)
        hi = lax.bitcast_convert_type(w & jnp.uint32(0xFFFF0000), f32)
        xb = jnp.concatenate([lo, hi], axis=1).astype(bf16)
        a = jnp.dot(xb, wg_sc[s], preferred_element_type=f32)
        u = jnp.dot(xb, wu_sc[s], preferred_element_type=f32)
        z = (a * jax.nn.sigmoid(a) * u).astype(bf16)
        y_ref[...] = jnp.dot(z, wd_sc[s], preferred_element_type=f32)

    for s in range(2):
        @pl.when(live & (slot == s))
        def _():
            compute(s)

    @pl.when(jnp.logical_not(live))
    def _():
        y_ref[...] = jnp.zeros(y_ref.shape, f32)


def _expert_ffn(xs, blk_expert, n_used, layer, wg, wu, wd):
    slots, dw = xs.shape
    d = 2 * dw
    de = wg.shape[3]
    n_blocks = slots // MOE_BLOCK
    grid_spec = pltpu.PrefetchScalarGridSpec(
        num_scalar_prefetch=2,
        grid=(n_blocks,),
        in_specs=[
            pl.BlockSpec((MOE_BLOCK, dw), lambda b, be, nu: (b, 0)),
            pl.BlockSpec(memory_space=pl.ANY),
            pl.BlockSpec(memory_space=pl.ANY),
            pl.BlockSpec(memory_space=pl.ANY),
        ],
        out_specs=pl.BlockSpec((MOE_BLOCK, d), lambda b, be, nu: (b, 0)),
        scratch_shapes=[pltpu.VMEM((2, d, de), bf16), pltpu.VMEM((2, d, de), bf16), pltpu.VMEM((2, de, d), bf16),
                        pltpu.SemaphoreType.DMA((2, 3))],
    )
    return pl.pallas_call(
        functools.partial(_ffn_kernel, layer=layer),
        out_shape=jax.ShapeDtypeStruct((slots, d), f32),
        grid_spec=grid_spec,
        compiler_params=_cparams("arbitrary"),
        name="expert_ffn",
    )(blk_expert, n_used, xs, wg, wu, wd)


def _fin_kernel(pos_cur, pos_nxt, x1_ref, gate_ref, mod_ref, lng_ref, lnb_ref, y_hbm, o_ref, ybuf, sem):
    i = pl.program_id(0)
    n = pl.num_programs(0)
    tm = x1_ref.shape[0]
    slot = i % 2

    def copy(pos_ref, kk, r, s):
        return pltpu.make_async_copy(
            y_hbm.at[pl.ds(pos_ref[0, kk, r], 1), :], ybuf.at[s, kk, pl.ds(r, 1), :], sem.at[s])

    def start_gather(pos_ref, s):
        for kk in range(2):
            for r in range(tm):
                copy(pos_ref, kk, r, s).start()

    @pl.when(i == 0)
    def _():
        start_gather(pos_cur, 0)

    @pl.when(i + 1 < n)
    def _():
        start_gather(pos_nxt, 1 - slot)

    for kk in range(2):
        for r in range(tm):
            copy(pos_cur, kk, r, slot).wait()
    g2 = mod_ref[0, 5:6, :]
    gate = gate_ref[...]
    f = gate[:, 0:1] * ybuf[slot, 0] + gate[:, 1:2] * ybuf[slot, 1]
    z = DEEPNORM_ALPHA * x1_ref[...] + g2 * f
    mu = jnp.mean(z, axis=-1, keepdims=True)
    zc = z - mu
    var = jnp.mean(zc * zc, axis=-1, keepdims=True)
    o_ref[...] = zc * lax.rsqrt(var + LN_EPS) * lng_ref[...] + lnb_ref[...]


def _combine_norm(x1, gate, pos3, y_sorted, mod_l, ln_g, ln_b, seg_of_tile):
    t, d = x1.shape
    n_tiles = t // ROW_TILE
    return pl.pallas_call(
        _fin_kernel,
        out_shape=jax.ShapeDtypeStruct((t, d), f32),
        grid=(n_tiles,),
        in_specs=[
            pl.BlockSpec((1, 2, ROW_TILE), lambda i: (i, 0, 0), memory_space=pltpu.SMEM),
            pl.BlockSpec((1, 2, ROW_TILE), lambda i: (jnp.minimum(i + 1, n_tiles - 1), 0, 0),
                         memory_space=pltpu.SMEM),
            pl.BlockSpec((ROW_TILE, d), lambda i: (i, 0)),
            pl.BlockSpec((ROW_TILE, GATE_COLS), lambda i: (i, 0)),
            pl.BlockSpec((1, 6, d), lambda i: (seg_of_tile(i), 0, 0)),
            pl.BlockSpec((1, d), lambda i: (0, 0)),
            pl.BlockSpec((1, d), lambda i: (0, 0)),
            pl.BlockSpec(memory_space=pl.ANY),
        ],
        out_specs=pl.BlockSpec((ROW_TILE, d), lambda i: (i, 0)),
        scratch_shapes=[pltpu.VMEM((2, 2, ROW_TILE, d), f32), pltpu.SemaphoreType.DMA((2,))],
        compiler_params=_cparams("arbitrary"),
        name="combine_norm",
    )(pos3, pos3, x1, gate, mod_l, ln_g, ln_b, y_sorted)


def _moe_combine(x1, hp, e_idx, gate, layer, wg_bf, wu_bf, wd_bf, mod_l, ln_g, ln_b, seg_of_tile):
    blk_expert, n_used, dest3, n_blocks = _dispatch_plan(e_idx)
    xs = _dispatch(hp, dest3, blk_expert, n_used)
    y_sorted = _expert_ffn(xs, blk_expert, n_used, layer, wg_bf, wu_bf, wd_bf)
    return _combine_norm(x1, gate, dest3, y_sorted, mod_l, ln_g, ln_b, seg_of_tile)


def _rope_tables(ctx_len, seq_len):
    t = jnp.arange(seq_len)
    row = (t // GRID_W).astype(f32)
    col = (t % GRID_W).astype(f32)
    inv_freq = ROPE_THETA ** (-jnp.arange(ROPE_PAIRS, dtype=f32) / ROPE_PAIRS)
    ang_r = row[:, None] * inv_freq
    ang_c = col[:, None] * inv_freq
    ang = jnp.concatenate([ang_r, ang_r, ang_c, ang_c], axis=-1)
    cos = jnp.cos(ang)
    sin = jnp.sin(ang)
    lane = jnp.arange(HEAD_DIM)
    sign = jnp.where((lane % (2 * ROPE_PAIRS)) < ROPE_PAIRS, -1.0, 1.0).astype(f32)
    cos_t = jnp.concatenate([jnp.ones((ctx_len, HEAD_DIM), f32), cos], axis=0)
    sin_t = jnp.concatenate([jnp.zeros((ctx_len, HEAD_DIM), f32), sin * sign], axis=0)
    return cos_t, sin_t


def kernel(x, c, ctx, c_ctx, w_mod, b_mod, ln_g, ln_b, w_qkv, q_gain, k_gain, w_o, w_pool, pool_scale,
           w_router, router_bias, w_gate, w_up, w_down):
    n_batch, seq_len, d = x.shape
    ctx_len = ctx.shape[1]
    assert w_mod.shape[0] == DEPTH and d == N_HEADS * HEAD_DIM
    assert ctx_len % ROW_TILE == 0 and seq_len % ROW_TILE == 0 and 1 + n_batch <= MOD_ROWS
    rows = _Rows(n_batch, ctx_len, seq_len)

    cond = jnp.concatenate([c_ctx[None, :], c], axis=0)
    mod = _modulation(cond, w_mod, b_mod).reshape(DEPTH, MOD_ROWS, 6, d)

    wr_t = w_router.reshape(d, N_EXPERT_GROUPS, EXPERTS_PER_GROUP).transpose(2, 1, 0).reshape(N_EXPERTS, d)
    wr_hi = wr_t.astype(bf16)
    wr_hl = jnp.concatenate([wr_hi, (wr_t - wr_hi.astype(f32)).astype(bf16)], axis=0)
    rb_p = router_bias.reshape(N_EXPERT_GROUPS, EXPERTS_PER_GROUP).T.reshape(N_EXPERTS, 1)
    cos_t, sin_t = _rope_tables(ctx_len, seq_len)

    ctx2 = ctx.reshape(n_batch * ctx_len, d)
    x2 = x.reshape(n_batch * seq_len, d)
    q, k, v = _qkv_proj(ctx2, x2, mod[0], w_qkv[0].astype(bf16), q_gain[0:1], k_gain[0:1], cos_t, sin_t, rows)
    bound = (Q_SCALE * HEAD_DIM * SCORE_BOUND_SLACK) * jnp.max(jnp.abs(q_gain[0])) * jnp.max(jnp.abs(k_gain[0]))
    attn_args = (bound.reshape(1, 1), q, k, v, w_gate, w_up, w_down)
    o, wg_bf, wu_bf, wd_bf = lax.cond(
        bound <= SCORE_BOUND_MAX,
        lambda a: _attention(*a, rows, True),
        lambda a: _attention(*a, rows, False),
        attn_args)
    x1, hp, e_idx, gate = _attn_post(ctx2, x2, o, w_o[0].astype(bf16), mod[0], ln_g[0, 0:1], ln_b[0, 0:1],
                                     wr_hl, rb_p, rows)
    x_all = _moe_combine(x1, hp, e_idx, gate, 0, wg_bf, wu_bf, wd_bf, mod[0], ln_g[0, 1:2], ln_b[0, 1:2],
                         rows.seg)

    x1, hp, e_idx, gate = _pool_post(x_all, w_pool[0].astype(bf16), pool_scale[0:1], mod[1],
                                     ln_g[1, 0:1], ln_b[1, 0:1], wr_hl, rb_p, rows)
    out = _moe_combine(x1, hp, e_idx, gate, 1, wg_bf, wu_bf, wd_bf, mod[1], ln_g[1, 1:2], ln_b[1, 1:2],
                       lambda i: 1 + i // rows.n_seq_tiles)
    return out.reshape(n_batch, seq_len, d)
```

```python
import functools

import jax
import jax.numpy as jnp
from jax import lax
from jax.experimental import pallas as pl
from jax.experimental.pallas import tpu as pltpu

N_HEADS = 16
N_KV_HEADS = 4
HEAD_DIM = 128
Q_PER_KV = N_HEADS // N_KV_HEADS
GRID_W = 64
ROPE_THETA = 10000.0
ROPE_PAIRS = HEAD_DIM // 4
ATTN_SCALE = HEAD_DIM ** -0.5
Q_SCALE = ATTN_SCALE * 1.4426950408889634
POOL_WINDOWS = (2, 4, 8, 16)
POOL_HALO = 8
N_EXPERTS = 32
N_EXPERT_GROUPS = 8
EXPERTS_PER_GROUP = N_EXPERTS // N_EXPERT_GROUPS
DEPTH = 2
DEEPNORM_ALPHA = (2 * DEPTH) ** 0.25
LN_EPS = 1e-6
RMS_EPS = 1e-6

LANES = 128
SUBLANES = 8
ROW_TILE = 256
ATTN_TK = 2048
ATTN_ROWS = 256
SCORE_BOUND_SLACK = 1.02
SCORE_BOUND_MAX = 56.0
POST_ROWS = 128
MOE_BLOCK = 256
PLAN_TILES_PER_STEP = (11, 8, 6, 4, 3, 2, 1)
NO_EXPERT = N_EXPERTS
INFO_FIRST, INFO_SLOT, INFO_NEXT = 8, 9, 10
GATE_COLS = 8
MOD_ROWS = 8
MOD_TN = 1024
VMEM_LIMIT = 56 * 1024 * 1024

f32 = jnp.float32
bf16 = jnp.bfloat16
u32 = jnp.uint32
i32 = jnp.int32


def _cparams(*sem):
    return pltpu.CompilerParams(dimension_semantics=sem, vmem_limit_bytes=VMEM_LIMIT)


def _nt_dot(a, b):
    return lax.dot_general(a, b, (((1,), (1,)), ((), ())), preferred_element_type=f32)


def _mod_kernel(c_ref, w_ref, b_ref, o_ref, s_sc, *, n_rows):
    d = w_ref.shape[1]
    tn = w_ref.shape[2]
    nj = tn // LANES

    @pl.when((pl.program_id(0) == 0) & (pl.program_id(1) == 0))
    def _():
        c = c_ref[...]
        s_sc[...] = c * jax.nn.sigmoid(c)

    def body(k, accs):
        r0 = pl.multiple_of(k * SUBLANES, SUBLANES)
        w8 = w_ref[0, pl.ds(r0, SUBLANES), :]
        out = []
        for r in range(n_rows):
            s8 = s_sc[r, pl.ds(r0, SUBLANES), :]
            for j in range(nj):
                out.append(accs[r * nj + j] + s8 * w8[:, j * LANES:(j + 1) * LANES])
        return tuple(out)

    init = tuple(jnp.zeros((SUBLANES, LANES), f32) for _ in range(n_rows * nj))
    accs = lax.fori_loop(0, d // SUBLANES, body, init, unroll=4)
    rows = []
    for r in range(n_rows):
        row = jnp.concatenate(
            [jnp.sum(accs[r * nj + j], axis=0, keepdims=True) for j in range(nj)], axis=1)
        rows.append(row + b_ref[0])
    rows.append(jnp.zeros((MOD_ROWS - n_rows, tn), f32))
    o_ref[0] = jnp.concatenate(rows, axis=0)


def _modulation(cond, w_mod, b_mod):
    n_rows, d = cond.shape
    depth, _, n6 = w_mod.shape
    c_b = jnp.broadcast_to(cond[:, :, None], (n_rows, d, LANES))
    return pl.pallas_call(
        functools.partial(_mod_kernel, n_rows=n_rows),
        out_shape=jax.ShapeDtypeStruct((depth, MOD_ROWS, n6), f32),
        grid=(depth, n6 // MOD_TN),
        in_specs=[
            pl.BlockSpec((n_rows, d, LANES), lambda l, j: (0, 0, 0)),
            pl.BlockSpec((1, d, MOD_TN), lambda l, j: (l, 0, j)),
            pl.BlockSpec((1, 1, MOD_TN), lambda l, j: (l, 0, j)),
        ],
        out_specs=pl.BlockSpec((1, MOD_ROWS, MOD_TN), lambda l, j: (l, 0, j)),
        scratch_shapes=[pltpu.VMEM((n_rows, d, LANES), f32)],
        compiler_params=_cparams("arbitrary", "arbitrary"),
        name="adaln_mod",
    )(c_b, w_mod, b_mod.reshape(depth, 1, n6))


class _Rows:
    def __init__(self, n_batch, ctx_len, seq_len):
        self.n_batch = n_batch
        self.ctx_len = ctx_len
        self.seq_len = seq_len
        self.n_ctx_tiles = ctx_len // ROW_TILE
        self.n_seq_tiles = seq_len // ROW_TILE
        self.tiles_per_batch = self.n_ctx_tiles + self.n_seq_tiles
        self.n_tiles = n_batch * self.tiles_per_batch
        self.t_all = self.n_tiles * ROW_TILE

    def is_ctx(self, i):
        return i % self.tiles_per_batch < self.n_ctx_tiles

    def seg(self, i):
        return jnp.where(self.is_ctx(i), 0, 1 + i // self.tiles_per_batch)

    def ctx_tile(self, i):
        r = i % self.tiles_per_batch
        return (i // self.tiles_per_batch) * self.n_ctx_tiles + jnp.minimum(r, self.n_ctx_tiles - 1)

    def lat_tile(self, i):
        r = i % self.tiles_per_batch
        return (i // self.tiles_per_batch) * self.n_seq_tiles + jnp.maximum(r - self.n_ctx_tiles, 0)


def _qkv_kernel(xc_ref, xl_ref, mod_ref, w_ref, qg_ref, kg_ref, cos_ref, sin_ref, q_ref, k_ref, v_ref, *, rows):
    tm, d = xl_ref.shape
    kv_dim = k_ref.shape[1]
    x = jnp.where(rows.is_ctx(pl.program_id(0)), xc_ref[...], xl_ref[...])
    sh1 = mod_ref[0, 0:1, :]
    sc1 = mod_ref[0, 1:2, :]
    u = (x * (1.0 + sc1) + sh1).astype(bf16)
    qkv = jnp.dot(u, w_ref[...], preferred_element_type=f32)
    cos = cos_ref[...]
    sin = sin_ref[...]
    lane = lax.broadcasted_iota(i32, (tm, HEAD_DIM), 1)
    first = (lane % (2 * ROPE_PAIRS)) < ROPE_PAIRS

    def norm_rope(xh, gain, scale):
        ms = jnp.mean(xh * xh, axis=-1, keepdims=True)
        xn = xh * lax.rsqrt(ms + RMS_EPS) * gain
        rot = jnp.where(first, pltpu.roll(xn, HEAD_DIM - ROPE_PAIRS, 1), pltpu.roll(xn, ROPE_PAIRS, 1))
        y = xn * cos + rot * sin
        return y * scale if scale != 1.0 else y

    qg = qg_ref[...]
    kg = kg_ref[...]
    for h in range(N_HEADS):
        sl = slice(h * HEAD_DIM, (h + 1) * HEAD_DIM)
        q_ref[:, sl] = norm_rope(qkv[:, sl], qg, Q_SCALE).astype(bf16)
    ones = jnp.ones((tm, HEAD_DIM), bf16)
    for h in range(N_KV_HEADS):
        sl = slice(h * HEAD_DIM, (h + 1) * HEAD_DIM)
        k_ref[:, sl] = norm_rope(qkv[:, d + h * HEAD_DIM:d + (h + 1) * HEAD_DIM], kg, 1.0).astype(bf16)
        v0 = d + kv_dim + h * HEAD_DIM
        v_ref[:, 2 * h * HEAD_DIM:(2 * h + 1) * HEAD_DIM] = qkv[:, v0:v0 + HEAD_DIM].astype(bf16)
        v_ref[:, (2 * h + 1) * HEAD_DIM:(2 * h + 2) * HEAD_DIM] = ones


def _qkv_proj(ctx2, x2, mod_l, w_qkv_bf, q_gain, k_gain, cos_t, sin_t, rows):
    d = x2.shape[1]
    kv_dim = N_KV_HEADS * HEAD_DIM
    tpb = rows.tiles_per_batch
    return pl.pallas_call(
        functools.partial(_qkv_kernel, rows=rows),
        out_shape=(jax.ShapeDtypeStruct((rows.t_all, d), bf16),
                   jax.ShapeDtypeStruct((rows.t_all, kv_dim), bf16),
                   jax.ShapeDtypeStruct((rows.t_all, 2 * kv_dim), bf16)),
        grid=(rows.n_tiles,),
        in_specs=[
            pl.BlockSpec((ROW_TILE, d), lambda i: (rows.ctx_tile(i), 0)),
            pl.BlockSpec((ROW_TILE, d), lambda i: (rows.lat_tile(i), 0)),
            pl.BlockSpec((1, 6, d), lambda i: (rows.seg(i), 0, 0)),
            pl.BlockSpec((d, d + 2 * kv_dim), lambda i: (0, 0)),
            pl.BlockSpec((1, HEAD_DIM), lambda i: (0, 0)),
            pl.BlockSpec((1, HEAD_DIM), lambda i: (0, 0)),
            pl.BlockSpec((ROW_TILE, HEAD_DIM), lambda i: (i % tpb, 0)),
            pl.BlockSpec((ROW_TILE, HEAD_DIM), lambda i: (i % tpb, 0)),
        ],
        out_specs=(pl.BlockSpec((ROW_TILE, d), lambda i: (i, 0)),
                   pl.BlockSpec((ROW_TILE, kv_dim), lambda i: (i, 0)),
                   pl.BlockSpec((ROW_TILE, 2 * kv_dim), lambda i: (i, 0))),
        compiler_params=_cparams("arbitrary"),
        name="qkv_rope",
    )(ctx2, x2, mod_l, w_qkv_bf, q_gain, k_gain, cos_t, sin_t)


def _attn_kernel(bound_ref, q_ref, k_ref, v_ref, wg_ref, wu_ref, wd_ref, o_ref, wgo_ref, wuo_ref, wdo_ref,
                 *, ctx_len, tk, rs, n_ctx_tiles, bounded):
    tq = q_ref.shape[0]
    n_lat_chunks = (k_ref.shape[0] - ctx_len) // tk
    qi = pl.program_id(2)
    subs = [(h, r0) for h in range(Q_PER_KV) for r0 in range(0, tq, rs)]

    def attend(chunks):
        states = [None] * len(subs)
        for ci, (off, size) in enumerate(chunks):
            for src, dst in ((wg_ref, wgo_ref), (wu_ref, wuo_ref), (wd_ref, wdo_ref)):
                n = src.shape[0] // len(chunks)
                dst[ci * n:(ci + 1) * n, :] = src[ci * n:(ci + 1) * n, :].astype(bf16)
            kc = k_ref[off:off + size, :]
            vc = v_ref[off:off + size, :]
            for i, (h, r0) in enumerate(subs):
                s = _nt_dot(q_ref[r0:r0 + rs, h * HEAD_DIM:(h + 1) * HEAD_DIM], kc)
                if bounded:
                    pv = jnp.dot(jnp.exp2(s - bound_ref[0, 0]).astype(bf16), vc, preferred_element_type=f32)
                    states[i] = pv if states[i] is None else states[i] + pv
                    continue
                mx = s[:, 0:LANES]
                for j in range(1, size // LANES):
                    mx = jnp.maximum(mx, s[:, j * LANES:(j + 1) * LANES])
                m_cur = jnp.max(mx, axis=-1, keepdims=True)
                if states[i] is None:
                    m_new = jnp.broadcast_to(m_cur, (rs, LANES))
                else:
                    m_prev, acc_prev = states[i]
                    m_new = jnp.maximum(m_prev, m_cur)
                p = jnp.concatenate(
                    [jnp.exp2(s[:, j * LANES:(j + 1) * LANES] - m_new) for j in range(size // LANES)],
                    axis=1).astype(bf16)
                acc = jnp.dot(p, vc, preferred_element_type=f32)
                if states[i] is not None:
                    alpha = jnp.exp2(m_prev - m_new)
                    acc = acc_prev * jnp.concatenate([alpha, alpha], axis=1) + acc
                states[i] = (m_new, acc)
        for i, (h, r0) in enumerate(subs):
            acc = states[i] if bounded else states[i][1]
            o_ref[r0:r0 + rs, h * HEAD_DIM:(h + 1) * HEAD_DIM] = (
                acc[:, :HEAD_DIM] / acc[:, HEAD_DIM:]).astype(bf16)

    @pl.when(qi < n_ctx_tiles)
    def _():
        attend([(0, ctx_len)])

    @pl.when(qi >= n_ctx_tiles)
    def _():
        attend([(0, ctx_len + tk)] + [(ctx_len + c * tk, tk) for c in range(1, n_lat_chunks)])


def _attention(bound, q, k, v, w_gate, w_up, w_down, rows, bounded):
    t_all, d = q.shape
    tpb = rows.tiles_per_batch
    rows_per_batch = t_all // rows.n_batch
    gw = Q_PER_KV * HEAD_DIM
    tk = min(ATTN_TK, rows.seq_len)
    assert rows.seq_len % tk == 0

    n_steps = rows.n_batch * N_KV_HEADS * tpb
    wg2 = w_gate.reshape(-1, w_gate.shape[-1])
    wu2 = w_up.reshape(-1, w_up.shape[-1])
    wd2 = w_down.reshape(-1, w_down.shape[-1])
    n_slabs = 1 << (n_steps.bit_length() - 1)
    slab_g = wg2.shape[0] // n_slabs
    slab_d = wd2.shape[0] // n_slabs
    assert wg2.shape[0] == n_slabs * slab_g and wd2.shape[0] == n_slabs * slab_d and wu2.shape == wg2.shape
    assert slab_g % SUBLANES == 0 and slab_d % SUBLANES == 0

    def slab(b, g, i):
        return (jnp.minimum((b * N_KV_HEADS + g) * tpb + i, n_slabs - 1), 0)

    kern = functools.partial(_attn_kernel, ctx_len=rows.ctx_len, tk=tk, rs=ATTN_ROWS,
                             n_ctx_tiles=rows.n_ctx_tiles, bounded=bounded)
    o, wg_bf, wu_bf, wd_bf = pl.pallas_call(
        kern,
        out_shape=(jax.ShapeDtypeStruct((t_all, d), bf16),
                   jax.ShapeDtypeStruct(wg2.shape, bf16),
                   jax.ShapeDtypeStruct(wu2.shape, bf16),
                   jax.ShapeDtypeStruct(wd2.shape, bf16)),
        grid=(rows.n_batch, N_KV_HEADS, tpb),
        in_specs=[
            pl.BlockSpec((1, 1), lambda b, g, i: (0, 0), memory_space=pltpu.SMEM),
            pl.BlockSpec((ROW_TILE, gw), lambda b, g, i: (b * tpb + i, g)),
            pl.BlockSpec((rows_per_batch, HEAD_DIM), lambda b, g, i: (b, g)),
            pl.BlockSpec((rows_per_batch, 2 * HEAD_DIM), lambda b, g, i: (b, g)),
            pl.BlockSpec((slab_g, wg2.shape[1]), slab),
            pl.BlockSpec((slab_g, wu2.shape[1]), slab),
            pl.BlockSpec((slab_d, wd2.shape[1]), slab),
        ],
        out_specs=(pl.BlockSpec((ROW_TILE, gw), lambda b, g, i: (b * tpb + i, g)),
                   pl.BlockSpec((slab_g, wg2.shape[1]), slab),
                   pl.BlockSpec((slab_g, wu2.shape[1]), slab),
                   pl.BlockSpec((slab_d, wd2.shape[1]), slab)),
        compiler_params=_cparams("arbitrary", "arbitrary", "arbitrary"),
        name="gqa_attention_bounded" if bounded else "gqa_attention_online",
    )(bound, q, k, v, wg2, wu2, wd2)
    return o, wg_bf.reshape(w_gate.shape), wu_bf.reshape(w_up.shape), wd_bf.reshape(w_down.shape)


def _post_epilogue(x, y, r0, mod_ref, lng_ref, lnb_ref, wr_ref, rb_ref, x1_ref, hp_ref, e_ref, g_ref):
    tm, d = x.shape
    rsl = slice(r0, r0 + tm)
    g1 = mod_ref[0, 2:3, :]
    sh2 = mod_ref[0, 3:4, :]
    sc2 = mod_ref[0, 4:5, :]
    z = DEEPNORM_ALPHA * x + g1 * y
    mu = jnp.mean(z, axis=-1, keepdims=True)
    zc = z - mu
    var = jnp.mean(zc * zc, axis=-1, keepdims=True)
    x1 = zc * lax.rsqrt(var + LN_EPS) * lng_ref[...] + lnb_ref[...]
    h = x1 * (1.0 + sc2) + sh2
    x1_ref[rsl, :] = x1

    h_hi = h.astype(bf16)
    h_hi32 = h_hi.astype(f32)
    bits = lax.bitcast_convert_type(h_hi32, u32)
    hp_ref[rsl, :] = (bits[:, :d // 2] >> 16) | bits[:, d // 2:]

    h_lo = (h - h_hi32).astype(bf16)
    wr = wr_ref[...]
    a2 = _nt_dot(wr, h_hi)
    lt = a2[:N_EXPERTS] + a2[N_EXPERTS:] + _nt_dot(wr[:N_EXPERTS], h_lo)
    aff = jax.nn.sigmoid(lt)
    sel = aff + rb_ref[...]
    ng = N_EXPERT_GROUPS
    a = [aff[j * ng:(j + 1) * ng] for j in range(EXPERTS_PER_GROUP)]
    s = [sel[j * ng:(j + 1) * ng] for j in range(EXPERTS_PER_GROUP)]
    gs = None
    for i in range(EXPERTS_PER_GROUP):
        for j in range(i + 1, EXPERTS_PER_GROUP):
            pij = s[i] + s[j]
            gs = pij if gs is None else jnp.maximum(gs, pij)
    gio = lax.broadcasted_iota(i32, gs.shape, 0)
    gmax = jnp.max(gs, axis=0, keepdims=True)
    g_idx = jnp.min(jnp.where(gs == gmax, gio, ng), axis=0, keepdims=True)
    onehot = gio == g_idx
    sin_g = [jnp.sum(jnp.where(onehot, s[j], 0.0), axis=0, keepdims=True) for j in range(EXPERTS_PER_GROUP)]
    aff_g = [jnp.sum(jnp.where(onehot, a[j], 0.0), axis=0, keepdims=True) for j in range(EXPERTS_PER_GROUP)]
    loc = [jnp.zeros_like(g_idx), jnp.zeros_like(g_idx)]
    gat = [jnp.zeros_like(gmax), jnp.zeros_like(gmax)]
    for j in range(EXPERTS_PER_GROUP):
        rank = jnp.zeros_like(g_idx)
        for i in range(EXPERTS_PER_GROUP):
            if i == j:
                continue
            ahead = (sin_g[i] >= sin_g[j]) if i < j else (sin_g[i] > sin_g[j])
            rank = rank + ahead.astype(i32)
        for kk in range(2):
            hit = rank == kk
            loc[kk] = jnp.where(hit, j, loc[kk])
            gat[kk] = jnp.where(hit, aff_g[j], gat[kk])
    den = gat[0] + gat[1]
    e_ref[0, :, rsl] = jnp.concatenate(
        [g_idx * EXPERTS_PER_GROUP + loc[0], g_idx * EXPERTS_PER_GROUP + loc[1]], axis=0)
    g8 = jnp.concatenate([gat[0] / den, gat[1] / den, jnp.zeros((GATE_COLS - 2, tm), f32)], axis=0)
    g_ref[rsl, :] = g8.T


def _attn_post_kernel(xc_ref, xl_ref, o_ref, wo_ref, mod_ref, lng_ref, lnb_ref, wr_ref, rb_ref,
                      x1_ref, hp_ref, e_ref, g_ref, ya_sc, yb_sc, *, rows):
    s = pl.program_id(0)
    tile = jnp.maximum(s - 1, 0)

    @pl.when(s == 0)
    def _():
        yb_sc[...] = jnp.zeros(yb_sc.shape, f32)

    def body(y_write, y_read):
        y_write[...] = jnp.dot(o_ref[...], wo_ref[...], preferred_element_type=f32)
        x = jnp.where(rows.is_ctx(tile), xc_ref[...], xl_ref[...])
        _post_epilogue(x, y_read[...], 0, mod_ref, lng_ref, lnb_ref, wr_ref, rb_ref,
                       x1_ref, hp_ref, e_ref, g_ref)

    @pl.when(s % 2 == 0)
    def _():
        body(ya_sc, yb_sc)

    @pl.when(s % 2 == 1)
    def _():
        body(yb_sc, ya_sc)


def _post_out(n_rows, d, tile_of):
    shapes = (jax.ShapeDtypeStruct((n_rows, d), f32),
              jax.ShapeDtypeStruct((n_rows, d // 2), u32),
              jax.ShapeDtypeStruct((n_rows // ROW_TILE, 2, ROW_TILE), i32),
              jax.ShapeDtypeStruct((n_rows, GATE_COLS), f32))
    specs = (pl.BlockSpec((ROW_TILE, d), lambda i: (tile_of(i), 0)),
             pl.BlockSpec((ROW_TILE, d // 2), lambda i: (tile_of(i), 0)),
             pl.BlockSpec((1, 2, ROW_TILE), lambda i: (tile_of(i), 0, 0)),
             pl.BlockSpec((ROW_TILE, GATE_COLS), lambda i: (tile_of(i), 0)))
    return shapes, specs


def _router_specs(d):
    return [pl.BlockSpec((2 * N_EXPERTS, d), lambda i: (0, 0)),
            pl.BlockSpec((N_EXPERTS, 1), lambda i: (0, 0))]


def _attn_post(ctx2, x2, o, w_o_bf, mod_l, ln_g, ln_b, wr_hl, rb_p, rows):
    d = x2.shape[1]
    n = rows.n_tiles

    def prev(i):
        return jnp.maximum(i - 1, 0)

    shapes, specs = _post_out(rows.t_all, d, prev)
    return pl.pallas_call(
        functools.partial(_attn_post_kernel, rows=rows),
        out_shape=shapes,
        grid=(n + 1,),
        in_specs=[
            pl.BlockSpec((ROW_TILE, d), lambda i: (rows.ctx_tile(prev(i)), 0)),
            pl.BlockSpec((ROW_TILE, d), lambda i: (rows.lat_tile(prev(i)), 0)),
            pl.BlockSpec((ROW_TILE, d), lambda i: (jnp.minimum(i, n - 1), 0)),
            pl.BlockSpec((d, d), lambda i: (0, 0)),
            pl.BlockSpec((1, 6, d), lambda i: (rows.seg(prev(i)), 0, 0)),
            pl.BlockSpec((1, d), lambda i: (0, 0)),
            pl.BlockSpec((1, d), lambda i: (0, 0)),
        ] + _router_specs(d),
        out_specs=specs,
        scratch_shapes=[pltpu.VMEM((ROW_TILE, d), f32), pltpu.VMEM((ROW_TILE, d), f32)],
        compiler_params=_cparams("arbitrary"),
        name="attn_out_norm_route",
    )(ctx2, x2, o, w_o_bf, mod_l, ln_g, ln_b, wr_hl, rb_p)


def _pool_post_kernel(x_ref, xp_ref, xn_ref, wp_ref, ps_ref, mod_ref, lng_ref, lnb_ref, wr_ref, rb_ref,
                      x1_ref, hp_ref, e_ref, g_ref, *, seq_len, tiles_per_seq):
    tm, d = x_ref.shape
    gd = d // len(POOL_WINDOWS)
    r = pl.program_id(0) % tiles_per_seq
    sh1 = mod_ref[0, 0:1, :]
    sc1 = mod_ref[0, 1:2, :]
    x = x_ref[...]
    ext = jnp.concatenate([xp_ref[...], x, xn_ref[...]], axis=0)
    n_ext = tm + 2 * POOL_HALO
    pos = r * tm - POOL_HALO + lax.broadcasted_iota(i32, (n_ext, 1), 0)
    inside = (pos >= 0) & (pos < seq_len)
    u_ext = jnp.where(inside, ext * (1.0 + sc1) + sh1, 0.0)
    t = pos[POOL_HALO:POOL_HALO + tm]
    dlts = []
    for g, w in enumerate(POOL_WINDOWS):
        e = u_ext[:, g * gd:(g + 1) * gd]
        p = e + pltpu.roll(e, 1, 0)
        step = 1
        while 2 * step < w:
            p = pltpu.roll(p, step, 0) + pltpu.roll(p, n_ext - step, 0)
            step *= 2
        cnt = (jnp.clip(t + w // 2, 0, seq_len) - jnp.clip(t - w // 2, 0, seq_len)).astype(f32)
        mean = p[POOL_HALO:POOL_HALO + tm] / cnt
        dlts.append((mean - e[POOL_HALO:POOL_HALO + tm]).astype(bf16))
    for r0 in range(0, tm, POST_ROWS):
        rsl = slice(r0, r0 + POST_ROWS)
        y = jnp.concatenate(
            [jnp.dot(dlts[g][rsl, :], wp_ref[g], preferred_element_type=f32) for g in range(len(POOL_WINDOWS))],
            axis=-1) * ps_ref[...]
        _post_epilogue(x[rsl, :], y, r0, mod_ref, lng_ref, lnb_ref, wr_ref, rb_ref, x1_ref, hp_ref, e_ref, g_ref)


def _pool_post(x_all, w_pool_bf, pool_scale, mod_l, ln_g, ln_b, wr_hl, rb_p, rows):
    t_all, d = x_all.shape
    tiles_per_seq = rows.n_seq_tiles
    n_rows = rows.n_batch * rows.seq_len
    halo_per_tile = ROW_TILE // POOL_HALO
    last_halo = t_all // POOL_HALO - 1
    ng = len(POOL_WINDOWS)

    def src_tile(i):
        return (i // tiles_per_seq) * rows.tiles_per_batch + rows.n_ctx_tiles + i % tiles_per_seq

    shapes, specs = _post_out(n_rows, d, lambda i: i)
    kern = functools.partial(_pool_post_kernel, seq_len=rows.seq_len, tiles_per_seq=tiles_per_seq)
    return pl.pallas_call(
        kern,
        out_shape=shapes,
        grid=(n_rows // ROW_TILE,),
        in_specs=[
            pl.BlockSpec((ROW_TILE, d), lambda i: (src_tile(i), 0)),
            pl.BlockSpec((POOL_HALO, d), lambda i: (jnp.maximum(src_tile(i) * halo_per_tile - 1, 0), 0)),
            pl.BlockSpec((POOL_HALO, d), lambda i: (jnp.minimum((src_tile(i) + 1) * halo_per_tile, last_halo), 0)),
            pl.BlockSpec((ng, d // ng, d // ng), lambda i: (0, 0, 0)),
            pl.BlockSpec((1, d), lambda i: (0, 0)),
            pl.BlockSpec((1, 6, d), lambda i: (1 + i // tiles_per_seq, 0, 0)),
            pl.BlockSpec((1, d), lambda i: (0, 0)),
            pl.BlockSpec((1, d), lambda i: (0, 0)),
        ] + _router_specs(d),
        out_specs=specs,
        compiler_params=_cparams("arbitrary"),
        name="pool_norm_route",
    )(x_all, x_all, x_all, w_pool_bf, pool_scale, mod_l, ln_g, ln_b, wr_hl, rb_p)


def _plan_kernel(e_ref, dest_ref, be_ref, nu_ref, carry_sc, start_sc):
    phase = pl.program_id(0)
    i = pl.program_id(1)
    tm = e_ref.shape[2]
    sub = lax.broadcasted_iota(i32, (N_EXPERTS, tm), 0)
    shift = MOE_BLOCK.bit_length() - 1

    @pl.when((phase == 0) & (i == 0))
    def _():
        carry_sc[...] = jnp.zeros(carry_sc.shape, f32)

    @pl.when(phase == 0)
    def _():
        carry = carry_sc[...]
        for j in range(e_ref.shape[0]):
            for kk in range(2):
                hit = sub == e_ref[j, kk:kk + 1, :]
                carry = carry + jnp.sum(jnp.where(hit, 1.0, 0.0), axis=1, keepdims=True)
        carry_sc[...] = carry

    @pl.when((phase == 1) & (i == 0))
    def _():
        counts = carry_sc[...].astype(i32)
        padded = ((counts + (MOE_BLOCK - 1)) >> shift) << shift
        esub = lax.broadcasted_iota(i32, padded.shape, 0)
        used = jnp.where(padded > 0, 1, 0)
        start = jnp.zeros(padded.shape, i32)
        order = jnp.zeros(padded.shape, i32)
        for e in range(N_EXPERTS - 1):
            start = start + jnp.where(esub > e, padded[e:e + 1, :], 0)
            order = order + jnp.where(esub > e, used[e:e + 1, :], 0)
        ends = start + padded
        nxt = jnp.full(padded.shape, NO_EXPERT, i32)
        cur = jnp.full((1, LANES), NO_EXPERT, i32)
        for e in range(N_EXPERTS - 1, -1, -1):
            nxt = jnp.where(esub == e, cur, nxt)
            cur = jnp.where(used[e:e + 1, :] > 0, e, cur)
        nb_pad = be_ref.shape[1]
        rep = nb_pad // LANES
        lane = lax.broadcasted_iota(i32, (1, nb_pad), 1)
        be = jnp.zeros((1, nb_pad), i32)
        for e in range(N_EXPERTS):
            be = be + jnp.where(jnp.concatenate([ends[e:e + 1, :]] * rep, axis=1) <= lane * MOE_BLOCK, 1, 0)
        be = jnp.minimum(be, N_EXPERTS - 1)
        par_b = jnp.zeros((1, nb_pad), i32)
        nxt_b = jnp.zeros((1, nb_pad), i32)
        for e in range(N_EXPERTS):
            hit = be == e
            par_b = jnp.where(hit, jnp.concatenate([order[e:e + 1, :]] * rep, axis=1) & 1, par_b)
            nxt_b = jnp.where(hit, jnp.concatenate([nxt[e:e + 1, :]] * rep, axis=1), nxt_b)
        first_b = jnp.where((be != pltpu.roll(be, 1, 1)) | (lane == 0), 1, 0)
        be_ref[...] = be | (first_b << INFO_FIRST) | (par_b << INFO_SLOT) | (nxt_b << INFO_NEXT)
        nu_ref[...] = ends[N_EXPERTS - 1:N_EXPERTS, :] >> shift
        start_sc[...] = start.astype(f32)
        carry_sc[...] = jnp.zeros(carry_sc.shape, f32)

    @pl.when(phase == 1)
    def _():
        before = lax.broadcasted_iota(i32, (tm, tm), 0) < lax.broadcasted_iota(i32, (tm, tm), 1)
        tri = jnp.where(before, 1.0, 0.0).astype(bf16)
        carry = carry_sc[...]
        start = start_sc[...]
        for j in range(e_ref.shape[0]):
            out = []
            for kk in range(2):
                hit = sub == e_ref[j, kk:kk + 1, :]
                onehot = jnp.where(hit, 1.0, 0.0)
                cum = jnp.dot(onehot.astype(bf16), tri, preferred_element_type=f32)
                base = (carry + start)[:, 0:1]
                out.append(jnp.sum(jnp.where(hit, cum + base, 0.0), axis=0, keepdims=True))
                carry = carry + jnp.sum(onehot, axis=1, keepdims=True)
            dest_ref[j] = jnp.concatenate(out, axis=0).astype(i32)
        carry_sc[...] = carry


def _dispatch_plan(e3):
    n_tiles, _, tm = e3.shape
    n_blocks = (2 * n_tiles * tm + N_EXPERTS * (MOE_BLOCK - 1) + MOE_BLOCK - 1) // MOE_BLOCK
    nb_pad = (n_blocks + LANES - 1) // LANES * LANES
    tps = max(c for c in PLAN_TILES_PER_STEP if n_tiles % c == 0)
    dest3, be, nu = pl.pallas_call(
        _plan_kernel,
        out_shape=(jax.ShapeDtypeStruct(e3.shape, i32),
                   jax.ShapeDtypeStruct((1, nb_pad), i32),
                   jax.ShapeDtypeStruct((1, LANES), i32)),
        grid=(2, n_tiles // tps),
        in_specs=[pl.BlockSpec((tps, 2, tm), lambda p, i: (i, 0, 0))],
        out_specs=(pl.BlockSpec((tps, 2, tm), lambda p, i: (p * i, 0, 0)),
                   pl.BlockSpec((1, nb_pad), lambda p, i: (0, 0)),
                   pl.BlockSpec((1, LANES), lambda p, i: (0, 0))),
        scratch_shapes=[pltpu.VMEM((N_EXPERTS, LANES), f32), pltpu.VMEM((N_EXPERTS, LANES), f32)],
        compiler_params=_cparams("arbitrary", "arbitrary"),
        name="route_plan",
    )(e3)
    return be[0, :n_blocks], nu[0, :1], dest3, n_blocks


def _dispatch_kernel(info_ref, nu_ref, dest_ref, hp_ref, xs_hbm, buf, zbuf, sem, zsem):
    i = pl.program_id(0)
    n = pl.num_programs(0)
    tm = hp_ref.shape[0]
    n_blocks = xs_hbm.shape[0] // MOE_BLOCK

    @pl.when(i == 0)
    def _():
        zbuf[...] = jnp.zeros(zbuf.shape, u32)

        def fill(b):
            return pltpu.make_async_copy(
                zbuf, xs_hbm.at[pl.ds(pl.multiple_of(b * MOE_BLOCK, MOE_BLOCK), MOE_BLOCK), :], zsem.at[0])

        def needs_fill(b):
            next_first = (info_ref[jnp.minimum(b + 1, n_blocks - 1)] >> INFO_FIRST) & 1
            return (b >= nu_ref[0] - 1) | (next_first == 1)

        def start(b, c):
            @pl.when(needs_fill(b))
            def _():
                fill(b).start()
            return c

        def wait(b, c):
            @pl.when(needs_fill(b))
            def _():
                fill(b).wait()
            return c

        lax.fori_loop(0, n_blocks, start, 0)
        lax.fori_loop(0, n_blocks, wait, 0)

    def issue(s):
        buf[s] = hp_ref[...]
        for kk in range(2):
            for r in range(tm):
                pltpu.make_async_copy(
                    buf.at[s, pl.ds(r, 1), :], xs_hbm.at[pl.ds(dest_ref[0, kk, r], 1), :], sem.at[s]
                ).start(priority=r % 2)

    def drain(s):
        for _ in range(2 * tm):
            pltpu.make_async_copy(buf.at[s, pl.ds(0, 1), :], xs_hbm.at[pl.ds(0, 1), :], sem.at[s]).wait()

    for s in range(2):
        @pl.when(i % 2 == s)
        def _():
            issue(s)

            @pl.when(i > 0)
            def _():
                drain(1 - s)

            @pl.when(i == n - 1)
            def _():
                drain(s)


def _dispatch(hp, dest3, blk_info, n_used):
    t, dw = hp.shape
    n_tiles = t // ROW_TILE
    slots = blk_info.shape[0] * MOE_BLOCK
    grid_spec = pltpu.PrefetchScalarGridSpec(
        num_scalar_prefetch=2,
        grid=(n_tiles,),
        in_specs=[
            pl.BlockSpec((1, 2, ROW_TILE), lambda i, info, nu: (i, 0, 0), memory_space=pltpu.SMEM),
            pl.BlockSpec((ROW_TILE, dw), lambda i, info, nu: (i, 0)),
        ],
        out_specs=pl.BlockSpec(memory_space=pl.ANY),
        scratch_shapes=[pltpu.VMEM((2, ROW_TILE, dw), u32), pltpu.VMEM((MOE_BLOCK, dw), u32),
                        pltpu.SemaphoreType.DMA((2,)), pltpu.SemaphoreType.DMA((1,))],
    )
    return pl.pallas_call(
        _dispatch_kernel,
        out_shape=jax.ShapeDtypeStruct((slots, dw), u32),
        grid_spec=grid_spec,
        compiler_params=_cparams("arbitrary"),
        name="moe_dispatch",
    )(blk_info, n_used, dest3, hp)


def _ffn_kernel(info_ref, nu_ref, xs_ref, wg_hbm, wu_hbm, wd_hbm, y_ref, wg_sc, wu_sc, wd_sc, sem, *, layer):
    b = pl.program_id(0)
    info = info_ref[b]
    expert = info & ((1 << INFO_FIRST) - 1)
    first = (info >> INFO_FIRST) & 1
    slot = (info >> INFO_SLOT) & 1
    nxt = info >> INFO_NEXT
    live = b < nu_ref[0]

    def weight_copies(e, s):
        return [pltpu.make_async_copy(src.at[layer, e], dst.at[s], sem.at[s, j])
                for j, (src, dst) in enumerate(((wg_hbm, wg_sc), (wu_hbm, wu_sc), (wd_hbm, wd_sc)))]

    @pl.when(b == 0)
    def _():
        for c in weight_copies(expert, 0):
            c.start()

    @pl.when(live & (first == 1) & (nxt != NO_EXPERT))
    def _():
        for c in weight_copies(nxt, 1 - slot):
            c.start()

    @pl.when(live & (first == 1))
    def _():
        for c in weight_copies(expert, slot):
            c.wait()

    def compute(s):
        w = xs_ref[...]
        lo = lax.bitcast_convert_type(w << 16, f32)
        hi = lax.bitcast_convert_type(w & jnp.uint32(0xFFFF0000), f32)
        xb = jnp.concatenate([lo, hi], axis=1).astype(bf16)
        a = jnp.dot(xb, wg_sc[s], preferred_element_type=f32)
        u = jnp.dot(xb, wu_sc[s], preferred_element_type=f32)
        z = (a * jax.nn.sigmoid(a) * u).astype(bf16)
        y_ref[...] = jnp.dot(z, wd_sc[s], preferred_element_type=f32)

    for s in range(2):
        @pl.when(live & (slot == s))
        def _():
            compute(s)

    @pl.when(jnp.logical_not(live))
    def _():
        y_ref[...] = jnp.zeros(y_ref.shape, f32)


def _expert_ffn(xs, blk_expert, n_used, layer, wg, wu, wd):
    slots, dw = xs.shape
    d = 2 * dw
    de = wg.shape[3]
    n_blocks = slots // MOE_BLOCK
    grid_spec = pltpu.PrefetchScalarGridSpec(
        num_scalar_prefetch=2,
        grid=(n_blocks,),
        in_specs=[
            pl.BlockSpec((MOE_BLOCK, dw), lambda b, be, nu: (b, 0)),
            pl.BlockSpec(memory_space=pl.ANY),
            pl.BlockSpec(memory_space=pl.ANY),
            pl.BlockSpec(memory_space=pl.ANY),
        ],
        out_specs=pl.BlockSpec((MOE_BLOCK, d), lambda b, be, nu: (b, 0)),
        scratch_shapes=[pltpu.VMEM((2, d, de), bf16), pltpu.VMEM((2, d, de), bf16), pltpu.VMEM((2, de, d), bf16),
                        pltpu.SemaphoreType.DMA((2, 3))],
    )
    return pl.pallas_call(
        functools.partial(_ffn_kernel, layer=layer),
        out_shape=jax.ShapeDtypeStruct((slots, d), f32),
        grid_spec=grid_spec,
        compiler_params=_cparams("arbitrary"),
        name="expert_ffn",
    )(blk_expert, n_used, xs, wg, wu, wd)


def _fin_kernel(pos_cur, pos_nxt, x1_ref, gate_ref, mod_ref, lng_ref, lnb_ref, y_hbm, o_ref, ybuf, sem):
    i = pl.program_id(0)
    n = pl.num_programs(0)
    tm = x1_ref.shape[0]
    slot = i % 2

    def copy(pos_ref, kk, r, s):
        return pltpu.make_async_copy(
            y_hbm.at[pl.ds(pos_ref[0, kk, r], 1), :], ybuf.at[s, kk, pl.ds(r, 1), :], sem.at[s])

    def start_gather(pos_ref, s):
        for kk in range(2):
            for r in range(tm):
                copy(pos_ref, kk, r, s).start(priority=r % 2)

    @pl.when(i == 0)
    def _():
        start_gather(pos_cur, 0)

    @pl.when(i + 1 < n)
    def _():
        start_gather(pos_nxt, 1 - slot)

    for kk in range(2):
        for r in range(tm):
            copy(pos_cur, kk, r, slot).wait()
    g2 = mod_ref[0, 5:6, :]
    gate = gate_ref[...]
    f = gate[:, 0:1] * ybuf[slot, 0] + gate[:, 1:2] * ybuf[slot, 1]
    z = DEEPNORM_ALPHA * x1_ref[...] + g2 * f
    mu = jnp.mean(z, axis=-1, keepdims=True)
    zc = z - mu
    var = jnp.mean(zc * zc, axis=-1, keepdims=True)
    o_ref[...] = zc * lax.rsqrt(var + LN_EPS) * lng_ref[...] + lnb_ref[...]


def _combine_norm(x1, gate, pos3, y_sorted, mod_l, ln_g, ln_b, seg_of_tile):
    t, d = x1.shape
    n_tiles = t // ROW_TILE
    return pl.pallas_call(
        _fin_kernel,
        out_shape=jax.ShapeDtypeStruct((t, d), f32),
        grid=(n_tiles,),
        in_specs=[
            pl.BlockSpec((1, 2, ROW_TILE), lambda i: (i, 0, 0), memory_space=pltpu.SMEM),
            pl.BlockSpec((1, 2, ROW_TILE), lambda i: (jnp.minimum(i + 1, n_tiles - 1), 0, 0),
                         memory_space=pltpu.SMEM),
            pl.BlockSpec((ROW_TILE, d), lambda i: (i, 0)),
            pl.BlockSpec((ROW_TILE, GATE_COLS), lambda i: (i, 0)),
            pl.BlockSpec((1, 6, d), lambda i: (seg_of_tile(i), 0, 0)),
            pl.BlockSpec((1, d), lambda i: (0, 0)),
            pl.BlockSpec((1, d), lambda i: (0, 0)),
            pl.BlockSpec(memory_space=pl.ANY),
        ],
        out_specs=pl.BlockSpec((ROW_TILE, d), lambda i: (i, 0)),
        scratch_shapes=[pltpu.VMEM((2, 2, ROW_TILE, d), f32), pltpu.SemaphoreType.DMA((2,))],
        compiler_params=_cparams("arbitrary"),
        name="combine_norm",
    )(pos3, pos3, x1, gate, mod_l, ln_g, ln_b, y_sorted)


def _moe_combine(x1, hp, e_idx, gate, layer, wg_bf, wu_bf, wd_bf, mod_l, ln_g, ln_b, seg_of_tile):
    blk_expert, n_used, dest3, n_blocks = _dispatch_plan(e_idx)
    xs = _dispatch(hp, dest3, blk_expert, n_used)
    y_sorted = _expert_ffn(xs, blk_expert, n_used, layer, wg_bf, wu_bf, wd_bf)
    return _combine_norm(x1, gate, dest3, y_sorted, mod_l, ln_g, ln_b, seg_of_tile)


def _rope_tables(ctx_len, seq_len):
    t = jnp.arange(seq_len)
    row = (t // GRID_W).astype(f32)
    col = (t % GRID_W).astype(f32)
    inv_freq = ROPE_THETA ** (-jnp.arange(ROPE_PAIRS, dtype=f32) / ROPE_PAIRS)
    ang_r = row[:, None] * inv_freq
    ang_c = col[:, None] * inv_freq
    ang = jnp.concatenate([ang_r, ang_r, ang_c, ang_c], axis=-1)
    cos = jnp.cos(ang)
    sin = jnp.sin(ang)
    lane = jnp.arange(HEAD_DIM)
    sign = jnp.where((lane % (2 * ROPE_PAIRS)) < ROPE_PAIRS, -1.0, 1.0).astype(f32)
    cos_t = jnp.concatenate([jnp.ones((ctx_len, HEAD_DIM), f32), cos], axis=0)
    sin_t = jnp.concatenate([jnp.zeros((ctx_len, HEAD_DIM), f32), sin * sign], axis=0)
    return cos_t, sin_t


def kernel(x, c, ctx, c_ctx, w_mod, b_mod, ln_g, ln_b, w_qkv, q_gain, k_gain, w_o, w_pool, pool_scale,
           w_router, router_bias, w_gate, w_up, w_down):
    n_batch, seq_len, d = x.shape
    ctx_len = ctx.shape[1]
    assert w_mod.shape[0] == DEPTH and d == N_HEADS * HEAD_DIM
    assert ctx_len % ROW_TILE == 0 and seq_len % ROW_TILE == 0 and 1 + n_batch <= MOD_ROWS
    rows = _Rows(n_batch, ctx_len, seq_len)

    cond = jnp.concatenate([c_ctx[None, :], c], axis=0)
    mod = _modulation(cond, w_mod, b_mod).reshape(DEPTH, MOD_ROWS, 6, d)

    wr_t = w_router.reshape(d, N_EXPERT_GROUPS, EXPERTS_PER_GROUP).transpose(2, 1, 0).reshape(N_EXPERTS, d)
    wr_hi = wr_t.astype(bf16)
    wr_hl = jnp.concatenate([wr_hi, (wr_t - wr_hi.astype(f32)).astype(bf16)], axis=0)
    rb_p = router_bias.reshape(N_EXPERT_GROUPS, EXPERTS_PER_GROUP).T.reshape(N_EXPERTS, 1)
    cos_t, sin_t = _rope_tables(ctx_len, seq_len)

    ctx2 = ctx.reshape(n_batch * ctx_len, d)
    x2 = x.reshape(n_batch * seq_len, d)
    q, k, v = _qkv_proj(ctx2, x2, mod[0], w_qkv[0].astype(bf16), q_gain[0:1], k_gain[0:1], cos_t, sin_t, rows)
    bound = (Q_SCALE * HEAD_DIM * SCORE_BOUND_SLACK) * jnp.max(jnp.abs(q_gain[0])) * jnp.max(jnp.abs(k_gain[0]))
    attn_args = (bound.reshape(1, 1), q, k, v, w_gate, w_up, w_down)
    o, wg_bf, wu_bf, wd_bf = lax.cond(
        bound <= SCORE_BOUND_MAX,
        lambda a: _attention(*a, rows, True),
        lambda a: _attention(*a, rows, False),
        attn_args)
    x1, hp, e_idx, gate = _attn_post(ctx2, x2, o, w_o[0].astype(bf16), mod[0], ln_g[0, 0:1], ln_b[0, 0:1],
                                     wr_hl, rb_p, rows)
    x_all = _moe_combine(x1, hp, e_idx, gate, 0, wg_bf, wu_bf, wd_bf, mod[0], ln_g[0, 1:2], ln_b[0, 1:2],
                         rows.seg)

    x1, hp, e_idx, gate = _pool_post(x_all, w_pool[0].astype(bf16), pool_scale[0:1], mod[1],
                                     ln_g[1, 0:1], ln_b[1, 0:1], wr_hl, rb_p, rows)
    out = _moe_combine(x1, hp, e_idx, gate, 1, wg_bf, wu_bf, wd_bf, mod[1], ln_g[1, 1:2], ln_b[1, 1:2],
                       lambda i: 1 + i // rows.n_seq_tiles)
    return out.reshape(n_batch, seq_len, d)
```

```python
import functools

import jax
import jax.numpy as jnp
from jax import lax
from jax.experimental import pallas as pl
from jax.experimental.pallas import tpu as pltpu

N_HEADS = 16
N_KV_HEADS = 4
HEAD_DIM = 128
Q_PER_KV = N_HEADS // N_KV_HEADS
GRID_W = 64
ROPE_THETA = 10000.0
ROPE_PAIRS = HEAD_DIM // 4
ATTN_SCALE = HEAD_DIM ** -0.5
Q_SCALE = ATTN_SCALE * 1.4426950408889634
POOL_WINDOWS = (2, 4, 8, 16)
POOL_HALO = 8
N_EXPERTS = 32
N_EXPERT_GROUPS = 8
EXPERTS_PER_GROUP = N_EXPERTS // N_EXPERT_GROUPS
DEPTH = 2
DEEPNORM_ALPHA = (2 * DEPTH) ** 0.25
LN_EPS = 1e-6
RMS_EPS = 1e-6

LANES = 128
SUBLANES = 8
ROW_TILE = 256
ATTN_TK = 2048
ATTN_ROWS = 256
SCORE_BOUND_SLACK = 1.02
SCORE_BOUND_MAX = 56.0
POST_ROWS = 128
MOE_BLOCK = 256
PLAN_TILES_PER_STEP = (11, 8, 6, 4, 3, 2, 1)
NO_EXPERT = N_EXPERTS
INFO_FIRST, INFO_SLOT, INFO_NEXT = 8, 9, 10
GATE_COLS = 8
MOD_ROWS = 8
MOD_TN = 1024
VMEM_LIMIT = 56 * 1024 * 1024

f32 = jnp.float32
bf16 = jnp.bfloat16
u32 = jnp.uint32
i32 = jnp.int32


def _cparams(*sem):
    return pltpu.CompilerParams(dimension_semantics=sem, vmem_limit_bytes=VMEM_LIMIT)


def _nt_dot(a, b):
    return lax.dot_general(a, b, (((1,), (1,)), ((), ())), preferred_element_type=f32)


def _mod_kernel(c_ref, w_ref, b_ref, o_ref, s_sc, *, n_rows):
    d = w_ref.shape[1]
    tn = w_ref.shape[2]
    nj = tn // LANES

    @pl.when((pl.program_id(0) == 0) & (pl.program_id(1) == 0))
    def _():
        c = c_ref[...]
        s_sc[...] = c * jax.nn.sigmoid(c)

    def body(k, accs):
        r0 = pl.multiple_of(k * SUBLANES, SUBLANES)
        w8 = w_ref[0, pl.ds(r0, SUBLANES), :]
        out = []
        for r in range(n_rows):
            s8 = s_sc[r, pl.ds(r0, SUBLANES), :]
            for j in range(nj):
                out.append(accs[r * nj + j] + s8 * w8[:, j * LANES:(j + 1) * LANES])
        return tuple(out)

    init = tuple(jnp.zeros((SUBLANES, LANES), f32) for _ in range(n_rows * nj))
    accs = lax.fori_loop(0, d // SUBLANES, body, init, unroll=4)
    rows = []
    for r in range(n_rows):
        row = jnp.concatenate(
            [jnp.sum(accs[r * nj + j], axis=0, keepdims=True) for j in range(nj)], axis=1)
        rows.append(row + b_ref[0])
    rows.append(jnp.zeros((MOD_ROWS - n_rows, tn), f32))
    o_ref[0] = jnp.concatenate(rows, axis=0)


def _modulation(cond, w_mod, b_mod):
    n_rows, d = cond.shape
    depth, _, n6 = w_mod.shape
    c_b = jnp.broadcast_to(cond[:, :, None], (n_rows, d, LANES))
    return pl.pallas_call(
        functools.partial(_mod_kernel, n_rows=n_rows),
        out_shape=jax.ShapeDtypeStruct((depth, MOD_ROWS, n6), f32),
        grid=(depth, n6 // MOD_TN),
        in_specs=[
            pl.BlockSpec((n_rows, d, LANES), lambda l, j: (0, 0, 0)),
            pl.BlockSpec((1, d, MOD_TN), lambda l, j: (l, 0, j)),
            pl.BlockSpec((1, 1, MOD_TN), lambda l, j: (l, 0, j)),
        ],
        out_specs=pl.BlockSpec((1, MOD_ROWS, MOD_TN), lambda l, j: (l, 0, j)),
        scratch_shapes=[pltpu.VMEM((n_rows, d, LANES), f32)],
        compiler_params=_cparams("arbitrary", "arbitrary"),
        name="adaln_mod",
    )(c_b, w_mod, b_mod.reshape(depth, 1, n6))


class _Rows:
    def __init__(self, n_batch, ctx_len, seq_len):
        self.n_batch = n_batch
        self.ctx_len = ctx_len
        self.seq_len = seq_len
        self.n_ctx_tiles = ctx_len // ROW_TILE
        self.n_seq_tiles = seq_len // ROW_TILE
        self.tiles_per_batch = self.n_ctx_tiles + self.n_seq_tiles
        self.n_tiles = n_batch * self.tiles_per_batch
        self.t_all = self.n_tiles * ROW_TILE

    def is_ctx(self, i):
        return i % self.tiles_per_batch < self.n_ctx_tiles

    def seg(self, i):
        return jnp.where(self.is_ctx(i), 0, 1 + i // self.tiles_per_batch)

    def ctx_tile(self, i):
        r = i % self.tiles_per_batch
        return (i // self.tiles_per_batch) * self.n_ctx_tiles + jnp.minimum(r, self.n_ctx_tiles - 1)

    def lat_tile(self, i):
        r = i % self.tiles_per_batch
        return (i // self.tiles_per_batch) * self.n_seq_tiles + jnp.maximum(r - self.n_ctx_tiles, 0)


def _qkv_kernel(xc_ref, xl_ref, mod_ref, w_ref, qg_ref, kg_ref, cos_ref, sin_ref, q_ref, k_ref, v_ref, *, rows):
    tm, d = xl_ref.shape
    kv_dim = k_ref.shape[1]
    x = jnp.where(rows.is_ctx(pl.program_id(0)), xc_ref[...], xl_ref[...])
    sh1 = mod_ref[0, 0:1, :]
    sc1 = mod_ref[0, 1:2, :]
    u = (x * (1.0 + sc1) + sh1).astype(bf16)
    qkv = jnp.dot(u, w_ref[...], preferred_element_type=f32)
    cos = cos_ref[...]
    sin = sin_ref[...]
    lane = lax.broadcasted_iota(i32, (tm, HEAD_DIM), 1)
    first = (lane % (2 * ROPE_PAIRS)) < ROPE_PAIRS

    def norm_rope(xh, gain, scale):
        ms = jnp.mean(xh * xh, axis=-1, keepdims=True)
        xn = xh * lax.rsqrt(ms + RMS_EPS) * gain
        rot = jnp.where(first, pltpu.roll(xn, HEAD_DIM - ROPE_PAIRS, 1), pltpu.roll(xn, ROPE_PAIRS, 1))
        y = xn * cos + rot * sin
        return y * scale if scale != 1.0 else y

    qg = qg_ref[...]
    kg = kg_ref[...]
    for h in range(N_HEADS):
        sl = slice(h * HEAD_DIM, (h + 1) * HEAD_DIM)
        q_ref[:, sl] = norm_rope(qkv[:, sl], qg, Q_SCALE).astype(bf16)
    ones = jnp.ones((tm, HEAD_DIM), bf16)
    for h in range(N_KV_HEADS):
        sl = slice(h * HEAD_DIM, (h + 1) * HEAD_DIM)
        k_ref[:, sl] = norm_rope(qkv[:, d + h * HEAD_DIM:d + (h + 1) * HEAD_DIM], kg, 1.0).astype(bf16)
        v0 = d + kv_dim + h * HEAD_DIM
        v_ref[:, 2 * h * HEAD_DIM:(2 * h + 1) * HEAD_DIM] = qkv[:, v0:v0 + HEAD_DIM].astype(bf16)
        v_ref[:, (2 * h + 1) * HEAD_DIM:(2 * h + 2) * HEAD_DIM] = ones


def _qkv_proj(ctx2, x2, mod_l, w_qkv_bf, q_gain, k_gain, cos_t, sin_t, rows):
    d = x2.shape[1]
    kv_dim = N_KV_HEADS * HEAD_DIM
    tpb = rows.tiles_per_batch
    return pl.pallas_call(
        functools.partial(_qkv_kernel, rows=rows),
        out_shape=(jax.ShapeDtypeStruct((rows.t_all, d), bf16),
                   jax.ShapeDtypeStruct((rows.t_all, kv_dim), bf16),
                   jax.ShapeDtypeStruct((rows.t_all, 2 * kv_dim), bf16)),
        grid=(rows.n_tiles,),
        in_specs=[
            pl.BlockSpec((ROW_TILE, d), lambda i: (rows.ctx_tile(i), 0)),
            pl.BlockSpec((ROW_TILE, d), lambda i: (rows.lat_tile(i), 0)),
            pl.BlockSpec((1, 6, d), lambda i: (rows.seg(i), 0, 0)),
            pl.BlockSpec((d, d + 2 * kv_dim), lambda i: (0, 0)),
            pl.BlockSpec((1, HEAD_DIM), lambda i: (0, 0)),
            pl.BlockSpec((1, HEAD_DIM), lambda i: (0, 0)),
            pl.BlockSpec((ROW_TILE, HEAD_DIM), lambda i: (i % tpb, 0)),
            pl.BlockSpec((ROW_TILE, HEAD_DIM), lambda i: (i % tpb, 0)),
        ],
        out_specs=(pl.BlockSpec((ROW_TILE, d), lambda i: (i, 0)),
                   pl.BlockSpec((ROW_TILE, kv_dim), lambda i: (i, 0)),
                   pl.BlockSpec((ROW_TILE, 2 * kv_dim), lambda i: (i, 0))),
        compiler_params=_cparams("arbitrary"),
        name="qkv_rope",
    )(ctx2, x2, mod_l, w_qkv_bf, q_gain, k_gain, cos_t, sin_t)


def _attn_kernel(bound_ref, q_ref, k_ref, v_ref, wg_ref, wu_ref, wd_ref, o_ref, wgo_ref, wuo_ref, wdo_ref,
                 *, ctx_len, tk, rs, n_ctx_tiles, bounded):
    tq = q_ref.shape[0]
    n_lat_chunks = (k_ref.shape[0] - ctx_len) // tk
    qi = pl.program_id(2)
    subs = [(h, r0) for h in range(Q_PER_KV) for r0 in range(0, tq, rs)]

    def attend(chunks):
        states = [None] * len(subs)
        for ci, (off, size) in enumerate(chunks):
            for src, dst in ((wg_ref, wgo_ref), (wu_ref, wuo_ref), (wd_ref, wdo_ref)):
                n = src.shape[0] // len(chunks)
                dst[ci * n:(ci + 1) * n, :] = src[ci * n:(ci + 1) * n, :].astype(bf16)
            kc = k_ref[off:off + size, :]
            vc = v_ref[off:off + size, :]
            for i, (h, r0) in enumerate(subs):
                s = _nt_dot(q_ref[r0:r0 + rs, h * HEAD_DIM:(h + 1) * HEAD_DIM], kc)
                if bounded:
                    pv = jnp.dot(jnp.exp2(s - bound_ref[0, 0]).astype(bf16), vc, preferred_element_type=f32)
                    states[i] = pv if states[i] is None else states[i] + pv
                    continue
                mx = s[:, 0:LANES]
                for j in range(1, size // LANES):
                    mx = jnp.maximum(mx, s[:, j * LANES:(j + 1) * LANES])
                m_cur = jnp.max(mx, axis=-1, keepdims=True)
                if states[i] is None:
                    m_new = jnp.broadcast_to(m_cur, (rs, LANES))
                else:
                    m_prev, acc_prev = states[i]
                    m_new = jnp.maximum(m_prev, m_cur)
                p = jnp.concatenate(
                    [jnp.exp2(s[:, j * LANES:(j + 1) * LANES] - m_new) for j in range(size // LANES)],
                    axis=1).astype(bf16)
                acc = jnp.dot(p, vc, preferred_element_type=f32)
                if states[i] is not None:
                    alpha = jnp.exp2(m_prev - m_new)
                    acc = acc_prev * jnp.concatenate([alpha, alpha], axis=1) + acc
                states[i] = (m_new, acc)
        for i, (h, r0) in enumerate(subs):
            acc = states[i] if bounded else states[i][1]
            o_ref[r0:r0 + rs, h * HEAD_DIM:(h + 1) * HEAD_DIM] = (
                acc[:, :HEAD_DIM] / acc[:, HEAD_DIM:]).astype(bf16)

    @pl.when(qi < n_ctx_tiles)
    def _():
        attend([(0, ctx_len)])

    @pl.when(qi >= n_ctx_tiles)
    def _():
        attend([(0, ctx_len + tk)] + [(ctx_len + c * tk, tk) for c in range(1, n_lat_chunks)])


def _attention(bound, q, k, v, w_gate, w_up, w_down, rows, bounded):
    t_all, d = q.shape
    tpb = rows.tiles_per_batch
    rows_per_batch = t_all // rows.n_batch
    gw = Q_PER_KV * HEAD_DIM
    tk = min(ATTN_TK, rows.seq_len)
    assert rows.seq_len % tk == 0

    n_steps = rows.n_batch * N_KV_HEADS * tpb
    wg2 = w_gate.reshape(-1, w_gate.shape[-1])
    wu2 = w_up.reshape(-1, w_up.shape[-1])
    wd2 = w_down.reshape(-1, w_down.shape[-1])
    n_slabs = 1 << (n_steps.bit_length() - 1)
    slab_g = wg2.shape[0] // n_slabs
    slab_d = wd2.shape[0] // n_slabs
    assert wg2.shape[0] == n_slabs * slab_g and wd2.shape[0] == n_slabs * slab_d and wu2.shape == wg2.shape
    assert slab_g % SUBLANES == 0 and slab_d % SUBLANES == 0

    def slab(b, g, i):
        return (jnp.minimum((b * N_KV_HEADS + g) * tpb + i, n_slabs - 1), 0)

    kern = functools.partial(_attn_kernel, ctx_len=rows.ctx_len, tk=tk, rs=ATTN_ROWS,
                             n_ctx_tiles=rows.n_ctx_tiles, bounded=bounded)
    o, wg_bf, wu_bf, wd_bf = pl.pallas_call(
        kern,
        out_shape=(jax.ShapeDtypeStruct((t_all, d), bf16),
                   jax.ShapeDtypeStruct(wg2.shape, bf16),
                   jax.ShapeDtypeStruct(wu2.shape, bf16),
                   jax.ShapeDtypeStruct(wd2.shape, bf16)),
        grid=(rows.n_batch, N_KV_HEADS, tpb),
        in_specs=[
            pl.BlockSpec((1, 1), lambda b, g, i: (0, 0), memory_space=pltpu.SMEM),
            pl.BlockSpec((ROW_TILE, gw), lambda b, g, i: (b * tpb + i, g)),
            pl.BlockSpec((rows_per_batch, HEAD_DIM), lambda b, g, i: (b, g)),
            pl.BlockSpec((rows_per_batch, 2 * HEAD_DIM), lambda b, g, i: (b, g)),
            pl.BlockSpec((slab_g, wg2.shape[1]), slab),
            pl.BlockSpec((slab_g, wu2.shape[1]), slab),
            pl.BlockSpec((slab_d, wd2.shape[1]), slab),
        ],
        out_specs=(pl.BlockSpec((ROW_TILE, gw), lambda b, g, i: (b * tpb + i, g)),
                   pl.BlockSpec((slab_g, wg2.shape[1]), slab),
                   pl.BlockSpec((slab_g, wu2.shape[1]), slab),
                   pl.BlockSpec((slab_d, wd2.shape[1]), slab)),
        compiler_params=_cparams("arbitrary", "arbitrary", "arbitrary"),
        name="gqa_attention_bounded" if bounded else "gqa_attention_online",
    )(bound, q, k, v, wg2, wu2, wd2)
    return o, wg_bf.reshape(w_gate.shape), wu_bf.reshape(w_up.shape), wd_bf.reshape(w_down.shape)


def _post_epilogue(x, y, r0, mod_ref, lng_ref, lnb_ref, wr_ref, rb_ref, x1_ref, hp_ref, e_ref, g_ref):
    tm, d = x.shape
    rsl = slice(r0, r0 + tm)
    g1 = mod_ref[0, 2:3, :]
    sh2 = mod_ref[0, 3:4, :]
    sc2 = mod_ref[0, 4:5, :]
    z = DEEPNORM_ALPHA * x + g1 * y
    mu = jnp.mean(z, axis=-1, keepdims=True)
    zc = z - mu
    var = jnp.mean(zc * zc, axis=-1, keepdims=True)
    x1 = zc * lax.rsqrt(var + LN_EPS) * lng_ref[...] + lnb_ref[...]
    h = x1 * (1.0 + sc2) + sh2
    x1_ref[rsl, :] = x1

    h_hi = h.astype(bf16)
    h_hi32 = h_hi.astype(f32)
    bits = lax.bitcast_convert_type(h_hi32, u32)
    hp_ref[rsl, :] = (bits[:, :d // 2] >> 16) | bits[:, d // 2:]

    h_lo = (h - h_hi32).astype(bf16)
    wr = wr_ref[...]
    a2 = _nt_dot(wr, h_hi)
    lt = a2[:N_EXPERTS] + a2[N_EXPERTS:] + _nt_dot(wr[:N_EXPERTS], h_lo)
    aff = jax.nn.sigmoid(lt)
    sel = aff + rb_ref[...]
    ng = N_EXPERT_GROUPS
    a = [aff[j * ng:(j + 1) * ng] for j in range(EXPERTS_PER_GROUP)]
    s = [sel[j * ng:(j + 1) * ng] for j in range(EXPERTS_PER_GROUP)]
    gs = None
    for i in range(EXPERTS_PER_GROUP):
        for j in range(i + 1, EXPERTS_PER_GROUP):
            pij = s[i] + s[j]
            gs = pij if gs is None else jnp.maximum(gs, pij)
    gio = lax.broadcasted_iota(i32, gs.shape, 0)
    gmax = jnp.max(gs, axis=0, keepdims=True)
    g_idx = jnp.min(jnp.where(gs == gmax, gio, ng), axis=0, keepdims=True)
    onehot = gio == g_idx
    sin_g = [jnp.sum(jnp.where(onehot, s[j], 0.0), axis=0, keepdims=True) for j in range(EXPERTS_PER_GROUP)]
    aff_g = [jnp.sum(jnp.where(onehot, a[j], 0.0), axis=0, keepdims=True) for j in range(EXPERTS_PER_GROUP)]
    loc = [jnp.zeros_like(g_idx), jnp.zeros_like(g_idx)]
    gat = [jnp.zeros_like(gmax), jnp.zeros_like(gmax)]
    for j in range(EXPERTS_PER_GROUP):
        rank = jnp.zeros_like(g_idx)
        for i in range(EXPERTS_PER_GROUP):
            if i == j:
                continue
            ahead = (sin_g[i] >= sin_g[j]) if i < j else (sin_g[i] > sin_g[j])
            rank = rank + ahead.astype(i32)
        for kk in range(2):
            hit = rank == kk
            loc[kk] = jnp.where(hit, j, loc[kk])
            gat[kk] = jnp.where(hit, aff_g[j], gat[kk])
    den = gat[0] + gat[1]
    e_ref[0, :, rsl] = jnp.concatenate(
        [g_idx * EXPERTS_PER_GROUP + loc[0], g_idx * EXPERTS_PER_GROUP + loc[1]], axis=0)
    g8 = jnp.concatenate([gat[0] / den, gat[1] / den, jnp.zeros((GATE_COLS - 2, tm), f32)], axis=0)
    g_ref[rsl, :] = g8.T


def _attn_post_kernel(xc_ref, xl_ref, o_ref, wo_ref, mod_ref, lng_ref, lnb_ref, wr_ref, rb_ref,
                      x1_ref, hp_ref, e_ref, g_ref, ya_sc, yb_sc, *, rows):
    s = pl.program_id(0)
    tile = jnp.maximum(s - 1, 0)

    @pl.when(s == 0)
    def _():
        yb_sc[...] = jnp.zeros(yb_sc.shape, f32)

    def body(y_write, y_read):
        y_write[...] = jnp.dot(o_ref[...], wo_ref[...], preferred_element_type=f32)
        x = jnp.where(rows.is_ctx(tile), xc_ref[...], xl_ref[...])
        _post_epilogue(x, y_read[...], 0, mod_ref, lng_ref, lnb_ref, wr_ref, rb_ref,
                       x1_ref, hp_ref, e_ref, g_ref)

    @pl.when(s % 2 == 0)
    def _():
        body(ya_sc, yb_sc)

    @pl.when(s % 2 == 1)
    def _():
        body(yb_sc, ya_sc)


def _post_out(n_rows, d, tile_of):
    shapes = (jax.ShapeDtypeStruct((n_rows, d), f32),
              jax.ShapeDtypeStruct((n_rows, d // 2), u32),
              jax.ShapeDtypeStruct((n_rows // ROW_TILE, 2, ROW_TILE), i32),
              jax.ShapeDtypeStruct((n_rows, GATE_COLS), f32))
    specs = (pl.BlockSpec((ROW_TILE, d), lambda i: (tile_of(i), 0)),
             pl.BlockSpec((ROW_TILE, d // 2), lambda i: (tile_of(i), 0)),
             pl.BlockSpec((1, 2, ROW_TILE), lambda i: (tile_of(i), 0, 0)),
             pl.BlockSpec((ROW_TILE, GATE_COLS), lambda i: (tile_of(i), 0)))
    return shapes, specs


def _router_specs(d):
    return [pl.BlockSpec((2 * N_EXPERTS, d), lambda i: (0, 0)),
            pl.BlockSpec((N_EXPERTS, 1), lambda i: (0, 0))]


def _attn_post(ctx2, x2, o, w_o_bf, mod_l, ln_g, ln_b, wr_hl, rb_p, rows):
    d = x2.shape[1]
    n = rows.n_tiles

    def prev(i):
        return jnp.maximum(i - 1, 0)

    shapes, specs = _post_out(rows.t_all, d, prev)
    return pl.pallas_call(
        functools.partial(_attn_post_kernel, rows=rows),
        out_shape=shapes,
        grid=(n + 1,),
        in_specs=[
            pl.BlockSpec((ROW_TILE, d), lambda i: (rows.ctx_tile(prev(i)), 0)),
            pl.BlockSpec((ROW_TILE, d), lambda i: (rows.lat_tile(prev(i)), 0)),
            pl.BlockSpec((ROW_TILE, d), lambda i: (jnp.minimum(i, n - 1), 0)),
            pl.BlockSpec((d, d), lambda i: (0, 0)),
            pl.BlockSpec((1, 6, d), lambda i: (rows.seg(prev(i)), 0, 0)),
            pl.BlockSpec((1, d), lambda i: (0, 0)),
            pl.BlockSpec((1, d), lambda i: (0, 0)),
        ] + _router_specs(d),
        out_specs=specs,
        scratch_shapes=[pltpu.VMEM((ROW_TILE, d), f32), pltpu.VMEM((ROW_TILE, d), f32)],
        compiler_params=_cparams("arbitrary"),
        name="attn_out_norm_route",
    )(ctx2, x2, o, w_o_bf, mod_l, ln_g, ln_b, wr_hl, rb_p)


def _pool_post_kernel(x_ref, xp_ref, xn_ref, wp_ref, ps_ref, mod_ref, lng_ref, lnb_ref, wr_ref, rb_ref,
                      x1_ref, hp_ref, e_ref, g_ref, *, seq_len, tiles_per_seq):
    tm, d = x_ref.shape
    gd = d // len(POOL_WINDOWS)
    r = pl.program_id(0) % tiles_per_seq
    sh1 = mod_ref[0, 0:1, :]
    sc1 = mod_ref[0, 1:2, :]
    x = x_ref[...]
    ext = jnp.concatenate([xp_ref[...], x, xn_ref[...]], axis=0)
    n_ext = tm + 2 * POOL_HALO
    pos = r * tm - POOL_HALO + lax.broadcasted_iota(i32, (n_ext, 1), 0)
    inside = (pos >= 0) & (pos < seq_len)
    u_ext = jnp.where(inside, ext * (1.0 + sc1) + sh1, 0.0)
    t = pos[POOL_HALO:POOL_HALO + tm]
    dlts = []
    for g, w in enumerate(POOL_WINDOWS):
        e = u_ext[:, g * gd:(g + 1) * gd]
        p = e + pltpu.roll(e, 1, 0)
        step = 1
        while 2 * step < w:
            p = pltpu.roll(p, step, 0) + pltpu.roll(p, n_ext - step, 0)
            step *= 2
        cnt = (jnp.clip(t + w // 2, 0, seq_len) - jnp.clip(t - w // 2, 0, seq_len)).astype(f32)
        mean = p[POOL_HALO:POOL_HALO + tm] / cnt
        dlts.append((mean - e[POOL_HALO:POOL_HALO + tm]).astype(bf16))
    for r0 in range(0, tm, POST_ROWS):
        rsl = slice(r0, r0 + POST_ROWS)
        y = jnp.concatenate(
            [jnp.dot(dlts[g][rsl, :], wp_ref[g], preferred_element_type=f32) for g in range(len(POOL_WINDOWS))],
            axis=-1) * ps_ref[...]
        _post_epilogue(x[rsl, :], y, r0, mod_ref, lng_ref, lnb_ref, wr_ref, rb_ref, x1_ref, hp_ref, e_ref, g_ref)


def _pool_post(x_all, w_pool_bf, pool_scale, mod_l, ln_g, ln_b, wr_hl, rb_p, rows):
    t_all, d = x_all.shape
    tiles_per_seq = rows.n_seq_tiles
    n_rows = rows.n_batch * rows.seq_len
    halo_per_tile = ROW_TILE // POOL_HALO
    last_halo = t_all // POOL_HALO - 1
    ng = len(POOL_WINDOWS)

    def src_tile(i):
        return (i // tiles_per_seq) * rows.tiles_per_batch + rows.n_ctx_tiles + i % tiles_per_seq

    shapes, specs = _post_out(n_rows, d, lambda i: i)
    kern = functools.partial(_pool_post_kernel, seq_len=rows.seq_len, tiles_per_seq=tiles_per_seq)
    return pl.pallas_call(
        kern,
        out_shape=shapes,
        grid=(n_rows // ROW_TILE,),
        in_specs=[
            pl.BlockSpec((ROW_TILE, d), lambda i: (src_tile(i), 0)),
            pl.BlockSpec((POOL_HALO, d), lambda i: (jnp.maximum(src_tile(i) * halo_per_tile - 1, 0), 0)),
            pl.BlockSpec((POOL_HALO, d), lambda i: (jnp.minimum((src_tile(i) + 1) * halo_per_tile, last_halo), 0)),
            pl.BlockSpec((ng, d // ng, d // ng), lambda i: (0, 0, 0)),
            pl.BlockSpec((1, d), lambda i: (0, 0)),
            pl.BlockSpec((1, 6, d), lambda i: (1 + i // tiles_per_seq, 0, 0)),
            pl.BlockSpec((1, d), lambda i: (0, 0)),
            pl.BlockSpec((1, d), lambda i: (0, 0)),
        ] + _router_specs(d),
        out_specs=specs,
        compiler_params=_cparams("arbitrary"),
        name="pool_norm_route",
    )(x_all, x_all, x_all, w_pool_bf, pool_scale, mod_l, ln_g, ln_b, wr_hl, rb_p)


def _plan_kernel(e_ref, dest_ref, be_ref, nu_ref, carry_sc, start_sc):
    phase = pl.program_id(0)
    i = pl.program_id(1)
    tm = e_ref.shape[2]
    sub = lax.broadcasted_iota(i32, (N_EXPERTS, tm), 0)
    shift = MOE_BLOCK.bit_length() - 1

    @pl.when((phase == 0) & (i == 0))
    def _():
        carry_sc[...] = jnp.zeros(carry_sc.shape, f32)

    @pl.when(phase == 0)
    def _():
        carry = carry_sc[...]
        for j in range(e_ref.shape[0]):
            for kk in range(2):
                hit = sub == e_ref[j, kk:kk + 1, :]
                carry = carry + jnp.sum(jnp.where(hit, 1.0, 0.0), axis=1, keepdims=True)
        carry_sc[...] = carry

    @pl.when((phase == 1) & (i == 0))
    def _():
        counts = carry_sc[...].astype(i32)
        padded = ((counts + (MOE_BLOCK - 1)) >> shift) << shift
        esub = lax.broadcasted_iota(i32, padded.shape, 0)
        used = jnp.where(padded > 0, 1, 0)
        start = jnp.zeros(padded.shape, i32)
        order = jnp.zeros(padded.shape, i32)
        for e in range(N_EXPERTS - 1):
            start = start + jnp.where(esub > e, padded[e:e + 1, :], 0)
            order = order + jnp.where(esub > e, used[e:e + 1, :], 0)
        ends = start + padded
        nxt = jnp.full(padded.shape, NO_EXPERT, i32)
        cur = jnp.full((1, LANES), NO_EXPERT, i32)
        for e in range(N_EXPERTS - 1, -1, -1):
            nxt = jnp.where(esub == e, cur, nxt)
            cur = jnp.where(used[e:e + 1, :] > 0, e, cur)
        nb_pad = be_ref.shape[1]
        rep = nb_pad // LANES
        lane = lax.broadcasted_iota(i32, (1, nb_pad), 1)
        be = jnp.zeros((1, nb_pad), i32)
        for e in range(N_EXPERTS):
            be = be + jnp.where(jnp.concatenate([ends[e:e + 1, :]] * rep, axis=1) <= lane * MOE_BLOCK, 1, 0)
        be = jnp.minimum(be, N_EXPERTS - 1)
        par_b = jnp.zeros((1, nb_pad), i32)
        nxt_b = jnp.zeros((1, nb_pad), i32)
        for e in range(N_EXPERTS):
            hit = be == e
            par_b = jnp.where(hit, jnp.concatenate([order[e:e + 1, :]] * rep, axis=1) & 1, par_b)
            nxt_b = jnp.where(hit, jnp.concatenate([nxt[e:e + 1, :]] * rep, axis=1), nxt_b)
        first_b = jnp.where((be != pltpu.roll(be, 1, 1)) | (lane == 0), 1, 0)
        be_ref[...] = be | (first_b << INFO_FIRST) | (par_b << INFO_SLOT) | (nxt_b << INFO_NEXT)
        nu_ref[...] = ends[N_EXPERTS - 1:N_EXPERTS, :] >> shift
        start_sc[...] = start.astype(f32)
        carry_sc[...] = jnp.zeros(carry_sc.shape, f32)

    @pl.when(phase == 1)
    def _():
        before = lax.broadcasted_iota(i32, (tm, tm), 0) < lax.broadcasted_iota(i32, (tm, tm), 1)
        tri = jnp.where(before, 1.0, 0.0).astype(bf16)
        carry = carry_sc[...]
        start = start_sc[...]
        for j in range(e_ref.shape[0]):
            out = []
            for kk in range(2):
                hit = sub == e_ref[j, kk:kk + 1, :]
                onehot = jnp.where(hit, 1.0, 0.0)
                cum = jnp.dot(onehot.astype(bf16), tri, preferred_element_type=f32)
                base = (carry + start)[:, 0:1]
                out.append(jnp.sum(jnp.where(hit, cum + base, 0.0), axis=0, keepdims=True))
                carry = carry + jnp.sum(onehot, axis=1, keepdims=True)
            dest_ref[j] = jnp.concatenate(out, axis=0).astype(i32)
        carry_sc[...] = carry


def _dispatch_plan(e3):
    n_tiles, _, tm = e3.shape
    n_blocks = (2 * n_tiles * tm + N_EXPERTS * (MOE_BLOCK - 1) + MOE_BLOCK - 1) // MOE_BLOCK
    nb_pad = (n_blocks + LANES - 1) // LANES * LANES
    tps = max(c for c in PLAN_TILES_PER_STEP if n_tiles % c == 0)
    dest3, be, nu = pl.pallas_call(
        _plan_kernel,
        out_shape=(jax.ShapeDtypeStruct(e3.shape, i32),
                   jax.ShapeDtypeStruct((1, nb_pad), i32),
                   jax.ShapeDtypeStruct((1, LANES), i32)),
        grid=(2, n_tiles // tps),
        in_specs=[pl.BlockSpec((tps, 2, tm), lambda p, i: (i, 0, 0))],
        out_specs=(pl.BlockSpec((tps, 2, tm), lambda p, i: (p * i, 0, 0)),
                   pl.BlockSpec((1, nb_pad), lambda p, i: (0, 0)),
                   pl.BlockSpec((1, LANES), lambda p, i: (0, 0))),
        scratch_shapes=[pltpu.VMEM((N_EXPERTS, LANES), f32), pltpu.VMEM((N_EXPERTS, LANES), f32)],
        compiler_params=_cparams("arbitrary", "arbitrary"),
        name="route_plan",
    )(e3)
    return be[0, :n_blocks], nu[0, :1], dest3, n_blocks


def _dispatch_kernel(info_ref, nu_ref, dest_ref, hp_ref, xs_hbm, buf, zbuf, sem, zsem):
    i = pl.program_id(0)
    n = pl.num_programs(0)
    tm = hp_ref.shape[0]
    n_blocks = xs_hbm.shape[0] // MOE_BLOCK

    @pl.when(i == 0)
    def _():
        zbuf[...] = jnp.zeros(zbuf.shape, u32)

        def fill(b):
            return pltpu.make_async_copy(
                zbuf, xs_hbm.at[pl.ds(pl.multiple_of(b * MOE_BLOCK, MOE_BLOCK), MOE_BLOCK), :], zsem.at[0])

        def needs_fill(b):
            next_first = (info_ref[jnp.minimum(b + 1, n_blocks - 1)] >> INFO_FIRST) & 1
            return (b >= nu_ref[0] - 1) | (next_first == 1)

        def start(b, c):
            @pl.when(needs_fill(b))
            def _():
                fill(b).start()
            return c

        def wait(b, c):
            @pl.when(needs_fill(b))
            def _():
                fill(b).wait()
            return c

        lax.fori_loop(0, n_blocks, start, 0)
        lax.fori_loop(0, n_blocks, wait, 0)

    def issue(s):
        buf[s] = hp_ref[...]
        for kk in range(2):
            for r in range(tm):
                pltpu.make_async_copy(
                    buf.at[s, pl.ds(r, 1), :], xs_hbm.at[pl.ds(dest_ref[0, kk, r], 1), :], sem.at[s]
                ).start(priority=r % 2)

    def drain(s):
        for _ in range(2 * tm):
            pltpu.make_async_copy(buf.at[s, pl.ds(0, 1), :], xs_hbm.at[pl.ds(0, 1), :], sem.at[s]).wait()

    for s in range(2):
        @pl.when(i % 2 == s)
        def _():
            issue(s)

            @pl.when(i > 0)
            def _():
                drain(1 - s)

            @pl.when(i == n - 1)
            def _():
                drain(s)


def _dispatch(hp, dest3, blk_info, n_used):
    t, dw = hp.shape
    n_tiles = t // ROW_TILE
    slots = blk_info.shape[0] * MOE_BLOCK
    grid_spec = pltpu.PrefetchScalarGridSpec(
        num_scalar_prefetch=2,
        grid=(n_tiles,),
        in_specs=[
            pl.BlockSpec((1, 2, ROW_TILE), lambda i, info, nu: (i, 0, 0), memory_space=pltpu.SMEM),
            pl.BlockSpec((ROW_TILE, dw), lambda i, info, nu: (i, 0)),
        ],
        out_specs=pl.BlockSpec(memory_space=pl.ANY),
        scratch_shapes=[pltpu.VMEM((2, ROW_TILE, dw), u32), pltpu.VMEM((MOE_BLOCK, dw), u32),
                        pltpu.SemaphoreType.DMA((2,)), pltpu.SemaphoreType.DMA((1,))],
    )
    return pl.pallas_call(
        _dispatch_kernel,
        out_shape=jax.ShapeDtypeStruct((slots, dw), u32),
        grid_spec=grid_spec,
        compiler_params=_cparams("arbitrary"),
        name="moe_dispatch",
    )(blk_info, n_used, dest3, hp)


def _ffn_kernel(info_ref, nu_ref, xs_ref, wg_hbm, wu_hbm, wd_hbm, y_ref, wg_sc, wu_sc, wd_sc, sem, *, layer):
    b = pl.program_id(0)
    info = info_ref[b]
    expert = info & ((1 << INFO_FIRST) - 1)
    first = (info >> INFO_FIRST) & 1
    slot = (info >> INFO_SLOT) & 1
    nxt = info >> INFO_NEXT
    live = b < nu_ref[0]

    def weight_copies(e, s):
        return [pltpu.make_async_copy(src.at[layer, e], dst.at[s], sem.at[s, j])
                for j, (src, dst) in enumerate(((wg_hbm, wg_sc), (wu_hbm, wu_sc), (wd_hbm, wd_sc)))]

    @pl.when(b == 0)
    def _():
        for c in weight_copies(expert, 0):
            c.start()

    @pl.when(live & (first == 1) & (nxt != NO_EXPERT))
    def _():
        for c in weight_copies(nxt, 1 - slot):
            c.start(priority=1)

    @pl.when(live & (first == 1))
    def _():
        for c in weight_copies(expert, slot):
            c.wait()

    def compute(s):
        w = xs_ref[...]
        lo = lax.bitcast_convert_type(w << 16, f32)
        hi = lax.bitcast_convert_type(w & jnp.uint32(0xFFFF0000), f32)
        xb = jnp.concatenate([lo, hi], axis=1).astype(bf16)
        a = jnp.dot(xb, wg_sc[s], preferred_element_type=f32)
        u = jnp.dot(xb, wu_sc[s], preferred_element_type=f32)
        z = (a * jax.nn.sigmoid(a) * u).astype(bf16)
        y_ref[...] = jnp.dot(z, wd_sc[s], preferred_element_type=f32)

    for s in range(2):
        @pl.when(live & (slot == s))
        def _():
            compute(s)

    @pl.when(jnp.logical_not(live))
    def _():
        y_ref[...] = jnp.zeros(y_ref.shape, f32)


def _expert_ffn(xs, blk_expert, n_used, layer, wg, wu, wd):
    slots, dw = xs.shape
    d = 2 * dw
    de = wg.shape[3]
    n_blocks = slots // MOE_BLOCK
    grid_spec = pltpu.PrefetchScalarGridSpec(
        num_scalar_prefetch=2,
        grid=(n_blocks,),
        in_specs=[
            pl.BlockSpec((MOE_BLOCK, dw), lambda b, be, nu: (b, 0)),
            pl.BlockSpec(memory_space=pl.ANY),
            pl.BlockSpec(memory_space=pl.ANY),
            pl.BlockSpec(memory_space=pl.ANY),
        ],
        out_specs=pl.BlockSpec((MOE_BLOCK, d), lambda b, be, nu: (b, 0)),
        scratch_shapes=[pltpu.VMEM((2, d, de), bf16), pltpu.VMEM((2, d, de), bf16), pltpu.VMEM((2, de, d), bf16),
                        pltpu.SemaphoreType.DMA((2, 3))],
    )
    return pl.pallas_call(
        functools.partial(_ffn_kernel, layer=layer),
        out_shape=jax.ShapeDtypeStruct((slots, d), f32),
        grid_spec=grid_spec,
        compiler_params=_cparams("arbitrary"),
        name="expert_ffn",
    )(blk_expert, n_used, xs, wg, wu, wd)


def _fin_kernel(pos_cur, pos_nxt, x1_ref, gate_ref, mod_ref, lng_ref, lnb_ref, y_hbm, o_ref, ybuf, sem):
    i = pl.program_id(0)
    n = pl.num_programs(0)
    tm = x1_ref.shape[0]
    slot = i % 2

    def copy(pos_ref, kk, r, s):
        return pltpu.make_async_copy(
            y_hbm.at[pl.ds(pos_ref[0, kk, r], 1), :], ybuf.at[s, kk, pl.ds(r, 1), :], sem.at[s])

    def start_gather(pos_ref, s):
        for kk in range(2):
            for r in range(tm):
                copy(pos_ref, kk, r, s).start(priority=r % 2)

    @pl.when(i == 0)
    def _():
        start_gather(pos_cur, 0)

    @pl.when(i + 1 < n)
    def _():
        start_gather(pos_nxt, 1 - slot)

    for kk in range(2):
        for r in range(tm):
            copy(pos_cur, kk, r, slot).wait()
    g2 = mod_ref[0, 5:6, :]
    gate = gate_ref[...]
    f = gate[:, 0:1] * ybuf[slot, 0] + gate[:, 1:2] * ybuf[slot, 1]
    z = DEEPNORM_ALPHA * x1_ref[...] + g2 * f
    mu = jnp.mean(z, axis=-1, keepdims=True)
    zc = z - mu
    var = jnp.mean(zc * zc, axis=-1, keepdims=True)
    o_ref[...] = zc * lax.rsqrt(var + LN_EPS) * lng_ref[...] + lnb_ref[...]


def _combine_norm(x1, gate, pos3, y_sorted, mod_l, ln_g, ln_b, seg_of_tile):
    t, d = x1.shape
    n_tiles = t // ROW_TILE
    return pl.pallas_call(
        _fin_kernel,
        out_shape=jax.ShapeDtypeStruct((t, d), f32),
        grid=(n_tiles,),
        in_specs=[
            pl.BlockSpec((1, 2, ROW_TILE), lambda i: (i, 0, 0), memory_space=pltpu.SMEM),
            pl.BlockSpec((1, 2, ROW_TILE), lambda i: (jnp.minimum(i + 1, n_tiles - 1), 0, 0),
                         memory_space=pltpu.SMEM),
            pl.BlockSpec((ROW_TILE, d), lambda i: (i, 0)),
            pl.BlockSpec((ROW_TILE, GATE_COLS), lambda i: (i, 0)),
            pl.BlockSpec((1, 6, d), lambda i: (seg_of_tile(i), 0, 0)),
            pl.BlockSpec((1, d), lambda i: (0, 0)),
            pl.BlockSpec((1, d), lambda i: (0, 0)),
            pl.BlockSpec(memory_space=pl.ANY),
        ],
        out_specs=pl.BlockSpec((ROW_TILE, d), lambda i: (i, 0)),
        scratch_shapes=[pltpu.VMEM((2, 2, ROW_TILE, d), f32), pltpu.SemaphoreType.DMA((2,))],
        compiler_params=_cparams("arbitrary"),
        name="combine_norm",
    )(pos3, pos3, x1, gate, mod_l, ln_g, ln_b, y_sorted)


def _moe_combine(x1, hp, e_idx, gate, layer, wg_bf, wu_bf, wd_bf, mod_l, ln_g, ln_b, seg_of_tile):
    blk_expert, n_used, dest3, n_blocks = _dispatch_plan(e_idx)
    xs = _dispatch(hp, dest3, blk_expert, n_used)
    y_sorted = _expert_ffn(xs, blk_expert, n_used, layer, wg_bf, wu_bf, wd_bf)
    return _combine_norm(x1, gate, dest3, y_sorted, mod_l, ln_g, ln_b, seg_of_tile)


def _rope_tables(ctx_len, seq_len):
    t = jnp.arange(seq_len)
    row = (t // GRID_W).astype(f32)
    col = (t % GRID_W).astype(f32)
    inv_freq = ROPE_THETA ** (-jnp.arange(ROPE_PAIRS, dtype=f32) / ROPE_PAIRS)
    ang_r = row[:, None] * inv_freq
    ang_c = col[:, None] * inv_freq
    ang = jnp.concatenate([ang_r, ang_r, ang_c, ang_c], axis=-1)
    cos = jnp.cos(ang)
    sin = jnp.sin(ang)
    lane = jnp.arange(HEAD_DIM)
    sign = jnp.where((lane % (2 * ROPE_PAIRS)) < ROPE_PAIRS, -1.0, 1.0).astype(f32)
    cos_t = jnp.concatenate([jnp.ones((ctx_len, HEAD_DIM), f32), cos], axis=0)
    sin_t = jnp.concatenate([jnp.zeros((ctx_len, HEAD_DIM), f32), sin * sign], axis=0)
    return cos_t, sin_t


def kernel(x, c, ctx, c_ctx, w_mod, b_mod, ln_g, ln_b, w_qkv, q_gain, k_gain, w_o, w_pool, pool_scale,
           w_router, router_bias, w_gate, w_up, w_down):
    n_batch, seq_len, d = x.shape
    ctx_len = ctx.shape[1]
    assert w_mod.shape[0] == DEPTH and d == N_HEADS * HEAD_DIM
    assert ctx_len % ROW_TILE == 0 and seq_len % ROW_TILE == 0 and 1 + n_batch <= MOD_ROWS
    rows = _Rows(n_batch, ctx_len, seq_len)

    cond = jnp.concatenate([c_ctx[None, :], c], axis=0)
    mod = _modulation(cond, w_mod, b_mod).reshape(DEPTH, MOD_ROWS, 6, d)

    wr_t = w_router.reshape(d, N_EXPERT_GROUPS, EXPERTS_PER_GROUP).transpose(2, 1, 0).reshape(N_EXPERTS, d)
    wr_hi = wr_t.astype(bf16)
    wr_hl = jnp.concatenate([wr_hi, (wr_t - wr_hi.astype(f32)).astype(bf16)], axis=0)
    rb_p = router_bias.reshape(N_EXPERT_GROUPS, EXPERTS_PER_GROUP).T.reshape(N_EXPERTS, 1)
    cos_t, sin_t = _rope_tables(ctx_len, seq_len)

    ctx2 = ctx.reshape(n_batch * ctx_len, d)
    x2 = x.reshape(n_batch * seq_len, d)
    q, k, v = _qkv_proj(ctx2, x2, mod[0], w_qkv[0].astype(bf16), q_gain[0:1], k_gain[0:1], cos_t, sin_t, rows)
    bound = (Q_SCALE * HEAD_DIM * SCORE_BOUND_SLACK) * jnp.max(jnp.abs(q_gain[0])) * jnp.max(jnp.abs(k_gain[0]))
    attn_args = (bound.reshape(1, 1), q, k, v, w_gate, w_up, w_down)
    o, wg_bf, wu_bf, wd_bf = lax.cond(
        bound <= SCORE_BOUND_MAX,
        lambda a: _attention(*a, rows, True),
        lambda a: _attention(*a, rows, False),
        attn_args)
    x1, hp, e_idx, gate = _attn_post(ctx2, x2, o, w_o[0].astype(bf16), mod[0], ln_g[0, 0:1], ln_b[0, 0:1],
                                     wr_hl, rb_p, rows)
    x_all = _moe_combine(x1, hp, e_idx, gate, 0, wg_bf, wu_bf, wd_bf, mod[0], ln_g[0, 1:2], ln_b[0, 1:2],
                         rows.seg)

    x1, hp, e_idx, gate = _pool_post(x_all, w_pool[0].astype(bf16), pool_scale[0:1], mod[1],
                                     ln_g[1, 0:1], ln_b[1, 0:1], wr_hl, rb_p, rows)
    out = _moe_combine(x1, hp, e_idx, gate, 1, wg_bf, wu_bf, wd_bf, mod[1], ln_g[1, 1:2], ln_b[1, 1:2],
                       lambda i: 1 + i // rows.n_seq_tiles)
    return out.reshape(n_batch, seq_len, d)
```
